```python
import jax, jax.numpy as jnp
from jax import lax
import numpy as np

D_MODEL = 1024
BATCH = 4
SEQ = 4096
DEPTH = 1

ATT_HEADS = 8
ATT_HEAD_DIM = 64
IDX_HEADS = 8
IDX_HEAD_DIM = 64
TOPK_MAX = 256
Q_BLOCK = 128
RET_HEADS = 4
RET_HEAD_DIM = 128
RET_CHUNK = 128
ROPE_THETA = 10000.0
ATT_WIDTH = ATT_HEADS * ATT_HEAD_DIM
RET_WIDTH = RET_HEADS * RET_HEAD_DIM
MIX_WIDTH = ATT_WIDTH + RET_WIDTH
IN_SPLITS = (ATT_WIDTH, ATT_WIDTH, ATT_WIDTH, IDX_HEADS * IDX_HEAD_DIM, IDX_HEAD_DIM, IDX_HEADS,
             RET_WIDTH, RET_WIDTH, RET_WIDTH, RET_WIDTH)
IN_WIDTH = sum(IN_SPLITS)
N_GROUPS = 4
EXPERTS_PER_GROUP = 8
N_EXPERTS = N_GROUPS * EXPERTS_PER_GROUP
EXPERT_TOPK = 2
D_EXPERT = 256
N_MOD = 6
EPS = 1e-6

kernel_name = 'hybrid_dsa_retention_hmoe_block'


def _rms_norm(x, g=None):
    xf = x.astype(jnp.float32)
    y = xf * lax.rsqrt(jnp.mean(xf * xf, axis=-1, keepdims=True) + EPS)
    if g is not None:
        y = y * g.astype(jnp.float32)
    return y.astype(x.dtype)


def _rope(x, positions):
    d = x.shape[-1]
    half = d // 2
    freqs = ROPE_THETA ** (-jnp.arange(half, dtype=jnp.float32) / half)
    ang = positions.astype(jnp.float32)[..., None] * freqs
    cos = jnp.cos(ang)[:, :, None, :]
    sin = jnp.sin(ang)[:, :, None, :]
    xf = x.astype(jnp.float32)
    x1, x2 = xf[..., :half], xf[..., half:]
    return jnp.concatenate([x1 * cos - x2 * sin, x2 * cos + x1 * sin], axis=-1).astype(x.dtype)


def _dsa_sparse_attention(q, k, v, q_idx, k_idx, w_idx):
    B, S, H, dh = q.shape
    topk = min(TOPK_MAX, S // 4)
    nblk = S // Q_BLOCK
    key_pos = jnp.arange(S)
    bidx = jnp.arange(B)[:, None, None]
    k_idx_f = k_idx.astype(jnp.float32)

    def to_blocks(a):
        return a.reshape(B, nblk, Q_BLOCK, *a.shape[2:]).swapaxes(0, 1)

    def block_fn(args):
        qb, qib, wb, blk = args
        qpos = blk * Q_BLOCK + jnp.arange(Q_BLOCK)
        logits = jnp.einsum('bqhd,bsd->bqhs', qib.astype(jnp.float32), k_idx_f)
        score = jnp.einsum('bqh,bqhs->bqs', wb.astype(jnp.float32), jax.nn.relu(logits))
        causal = key_pos[None, :] <= qpos[:, None]
        score = jnp.where(causal[None], score, -jnp.inf)
        _, sel = lax.top_k(score, topk)
        valid = sel <= qpos[None, :, None]
        k_sel = k[bidx, sel].astype(jnp.float32)
        v_sel = v[bidx, sel].astype(jnp.float32)
        att = jnp.einsum('bqhd,bqkhd->bqhk', qb.astype(jnp.float32), k_sel) * (dh ** -0.5)
        att = jnp.where(valid[:, :, None, :], att, -jnp.inf)
        p = jax.nn.softmax(att, axis=-1)
        return jnp.einsum('bqhk,bqkhd->bqhd', p, v_sel).astype(q.dtype)

    out = lax.map(block_fn, (to_blocks(q), to_blocks(q_idx), to_blocks(w_idx), jnp.arange(nblk)))
    return out.swapaxes(0, 1).reshape(B, S, H, dh)


def _retention(q, k, v):
    B, S, H, d = q.shape
    C = RET_CHUNK
    N = S // C
    log_gamma = jnp.log(1.0 - 2.0 ** (-5.0 - jnp.arange(H, dtype=jnp.float32)))
    qf = q.astype(jnp.float32).reshape(B, N, C, H, d)
    kf = (k.astype(jnp.float32) * (d ** -0.5)).reshape(B, N, C, H, d)
    vf = v.astype(jnp.float32).reshape(B, N, C, H, d)
    idx = jnp.arange(C, dtype=jnp.float32)
    diff = idx[:, None] - idx[None, :]
    causal = diff >= 0
    decay = jnp.where(causal[None], jnp.exp(log_gamma[:, None, None] * jnp.where(causal, diff, 0.0)[None]), 0.0)
    inner = jnp.einsum('bnihd,bnjhd->bnhij', qf, kf) * decay[None, None]
    o_inner = jnp.einsum('bnhij,bnjhe->bnihe', inner, vf)
    k_decay = jnp.exp(log_gamma[None, :] * (C - 1 - idx)[:, None])
    kv = jnp.einsum('bnjhd,bnjhe->nbhde', kf * k_decay[:, :, None], vf)
    chunk_decay = jnp.exp(log_gamma * C)[None, :, None, None]

    def step(R, kv_n):
        return R * chunk_decay + kv_n, R

    _, R_prev = lax.scan(step, jnp.zeros((B, H, d, d), jnp.float32), kv)
    q_decay = jnp.exp(log_gamma[None, :] * (idx + 1.0)[:, None])
    o_cross = jnp.einsum('bnihd,nbhde->bnihe', qf * q_decay[:, :, None], R_prev)
    return (o_inner + o_cross).reshape(B, S, H, d)


def _hybrid_mixer(h, positions, w_in, w_out):
    B, S, _ = h.shape
    proj = jnp.einsum('bsd,de->bse', h, w_in)
    offsets = [int(o) for o in np.cumsum(IN_SPLITS)[:-1]]
    q_a, k_a, v_a, q_i, k_i, w_i, q_r, k_r, v_r, g_r = jnp.split(proj, offsets, axis=-1)
    q_a = _rope(q_a.reshape(B, S, ATT_HEADS, ATT_HEAD_DIM), positions)
    k_a = _rope(k_a.reshape(B, S, ATT_HEADS, ATT_HEAD_DIM), positions)
    v_a = v_a.reshape(B, S, ATT_HEADS, ATT_HEAD_DIM)
    q_i = _rope(q_i.reshape(B, S, IDX_HEADS, IDX_HEAD_DIM), positions)
    k_i = _rope(k_i[:, :, None, :], positions)[:, :, 0, :]
    w_i = w_i * ((IDX_HEADS ** -0.5) * (IDX_HEAD_DIM ** -0.5))
    att = _dsa_sparse_attention(q_a, k_a, v_a, q_i, k_i, w_i).reshape(B, S, ATT_WIDTH)
    q_r = _rope(q_r.reshape(B, S, RET_HEADS, RET_HEAD_DIM), positions)
    k_r = _rope(k_r.reshape(B, S, RET_HEADS, RET_HEAD_DIM), positions)
    v_r = v_r.reshape(B, S, RET_HEADS, RET_HEAD_DIM)
    ret = _rms_norm(_retention(q_r, k_r, v_r)).reshape(B, S, RET_WIDTH)
    ret = (jax.nn.silu(g_r.astype(jnp.float32)) * ret).astype(h.dtype)
    mixed = jnp.concatenate([att.astype(h.dtype), ret], axis=-1)
    return jnp.einsum('bse,ed->bsd', mixed, w_out)


def _hier_moe(h, w_group, b_group, w_expert, b_expert, w_gate_exp, w_up_exp, w_down_exp):
    B, S, D = h.shape
    T = B * S
    t = h.reshape(T, D)
    g_prob = jax.nn.softmax((t @ w_group + b_group).astype(jnp.float32), axis=-1)
    g_top, g_sel = lax.top_k(g_prob, 1)
    e_logits = (t @ w_expert + b_expert).astype(jnp.float32).reshape(T, N_GROUPS, EXPERTS_PER_GROUP)
    e_logits = jnp.take_along_axis(e_logits, g_sel[:, :, None], axis=1)[:, 0]
    e_top, e_sel = lax.top_k(jax.nn.softmax(e_logits, axis=-1), EXPERT_TOPK)
    weights = g_top * e_top / jnp.sum(e_top, axis=-1, keepdims=True)
    expert_id = g_sel * EXPERTS_PER_GROUP + e_sel
    combine = jnp.einsum('tk,tke->te', weights, jax.nn.one_hot(expert_id, N_EXPERTS, dtype=jnp.float32))
    y = jnp.zeros((T, D), jnp.float32)
    for e in range(N_EXPERTS):
        hid = jax.nn.silu(t @ w_gate_exp[e]) * (t @ w_up_exp[e])
        y = y + combine[:, e:e + 1] * (hid @ w_down_exp[e]).astype(jnp.float32)
    return y.astype(h.dtype).reshape(B, S, D)


def setup_inputs(seed: int = 0) -> dict:
    key = jax.random.key(seed)
    ks = jax.random.split(key, 20)
    f32 = jnp.float32
    nrm = lambda k, shape, s: jax.random.normal(k, shape, f32) * s
    L = DEPTH
    x = jax.random.normal(ks[0], (BATCH, SEQ, D_MODEL), f32)
    c = jax.random.normal(ks[1], (BATCH, D_MODEL), f32)
    positions = (jax.random.randint(ks[2], (BATCH, 1), 0, 1024) + jnp.arange(SEQ)[None, :]).astype(jnp.int32)
    return {
        'x': x,
        'c': c,
        'positions': positions,
        'g_pre_mix': 1.0 + nrm(ks[3], (L, D_MODEL), 0.05),
        'w_ada': nrm(ks[4], (L, D_MODEL, N_MOD * D_MODEL), 0.5 * D_MODEL ** -0.5),
        'b_ada': nrm(ks[5], (L, N_MOD * D_MODEL), 0.02),
        'w_in': nrm(ks[6], (L, D_MODEL, IN_WIDTH), D_MODEL ** -0.5),
        'w_out': nrm(ks[7], (L, MIX_WIDTH, D_MODEL), MIX_WIDTH ** -0.5),
        'g_post_mix': 1.0 + nrm(ks[8], (L, D_MODEL), 0.05),
        'g_pre_ffn': 1.0 + nrm(ks[9], (L, D_MODEL), 0.05),
        'w_group': nrm(ks[10], (L, D_MODEL, N_GROUPS), D_MODEL ** -0.5),
        'b_group': nrm(ks[11], (L, N_GROUPS), 0.01),
        'w_expert': nrm(ks[12], (L, D_MODEL, N_EXPERTS), D_MODEL ** -0.5),
        'b_expert': nrm(ks[13], (L, N_EXPERTS), 0.01),
        'w_gate_exp': nrm(ks[14], (L, N_EXPERTS, D_MODEL, D_EXPERT), D_MODEL ** -0.5),
        'w_up_exp': nrm(ks[15], (L, N_EXPERTS, D_MODEL, D_EXPERT), D_MODEL ** -0.5),
        'w_down_exp': nrm(ks[16], (L, N_EXPERTS, D_EXPERT, D_MODEL), D_EXPERT ** -0.5),
        'g_post_ffn': 1.0 + nrm(ks[17], (L, D_MODEL), 0.05),
    }


def reference(x, c, positions, g_pre_mix, w_ada, b_ada, w_in, w_out, g_post_mix, g_pre_ffn,
              w_group, b_group, w_expert, b_expert, w_gate_exp, w_up_exp, w_down_exp, g_post_ffn):
    h_res = x
    for l in range(DEPTH):
        mod = jax.nn.silu(c) @ w_ada[l] + b_ada[l]
        shift1, scale1, gate1, shift2, scale2, gate2 = [m[:, None, :] for m in jnp.split(mod, N_MOD, axis=-1)]
        h = _rms_norm(h_res, g_pre_mix[l]) * (1.0 + scale1) + shift1
        mix = _hybrid_mixer(h, positions, w_in[l], w_out[l])
        h_res = (h_res + gate1 * _rms_norm(mix, g_post_mix[l])).astype(x.dtype)
        h2 = _rms_norm(h_res, g_pre_ffn[l]) * (1.0 + scale2) + shift2
        ffn = _hier_moe(h2, w_group[l], b_group[l], w_expert[l], b_expert[l],
                        w_gate_exp[l], w_up_exp[l], w_down_exp[l])
        h_res = (h_res + gate2 * _rms_norm(ffn, g_post_ffn[l])).astype(x.dtype)
    return h_res
```

```python
import functools
import math

import numpy as np
import jax
import jax.numpy as jnp
from jax import lax
from jax.experimental import pallas as pl
from jax.experimental.pallas import tpu as pltpu

D_MODEL = 1024
ATT_HEADS = 8
ATT_HEAD_DIM = 64
IDX_HEADS = 8
IDX_HEAD_DIM = 64
TOPK_MAX = 256
RET_HEADS = 4
RET_HEAD_DIM = 128
ROPE_THETA = 10000.0
ATT_WIDTH = ATT_HEADS * ATT_HEAD_DIM
RET_WIDTH = RET_HEADS * RET_HEAD_DIM
N_GROUPS = 4
EXPERTS_PER_GROUP = 8
N_EXPERTS = N_GROUPS * EXPERTS_PER_GROUP
D_EXPERT = 256
N_MOD = 6
EPS = 1e-6

LANES = 128
VMEM_LIMIT = 56 * 1024 * 1024

F32 = jnp.float32
BF16 = jnp.bfloat16
NEG_INF = float("-inf")
F32_LOWEST = float(np.finfo(np.float32).min)

_NT = (((1,), (1,)), ((), ()))
_TN = (((0,), (0,)), ((), ()))


def _params(*sem):
    return pltpu.CompilerParams(dimension_semantics=sem, vmem_limit_bytes=VMEM_LIMIT)


def _rms(x):
    return x * lax.rsqrt(jnp.mean(x * x, axis=-1, keepdims=True) + EPS)


def _silu(x):
    return x * (1.0 / (1.0 + jnp.exp(-x)))


def _ada_kernel(c_ref, w_ref, b_ref, o_ref):
    a = _silu(c_ref[...]).astype(BF16)
    o_ref[...] = jnp.dot(a, w_ref[...].astype(BF16), preferred_element_type=F32) + b_ref[...]


def _ada(c, w, b):
    bsz = c.shape[0]
    n = w.shape[1]
    tn = D_MODEL
    return pl.pallas_call(
        _ada_kernel,
        grid=(n // tn,),
        in_specs=[
            pl.BlockSpec((bsz, D_MODEL), lambda j: (0, 0)),
            pl.BlockSpec((D_MODEL, tn), lambda j: (0, j)),
            pl.BlockSpec((1, tn), lambda j: (0, j)),
        ],
        out_specs=pl.BlockSpec((bsz, tn), lambda j: (0, j)),
        out_shape=jax.ShapeDtypeStruct((bsz, n), F32),
        compiler_params=_params("arbitrary"),
        name="ada",
    )(c, w, b.reshape(1, n))


def _tables_kernel(pos_ref, f_ref, cs64_ref, sn64_ref, cs128_ref, sn128_ref):
    ang = pos_ref[0].astype(F32) * f_ref[...]
    c = jnp.cos(ang)
    s = jnp.sin(ang)
    c_sw = pltpu.roll(c, 64, 1)
    s_sw = pltpu.roll(s, 64, 1)
    lane = lax.broadcasted_iota(jnp.int32, c.shape, 1)
    lo = lane < 64
    cs128_ref[0] = jnp.where(lo, c, c_sw)
    sn128_ref[0] = jnp.where(lo, -s, s_sw)
    cs64_ref[0] = jnp.where(lo, c_sw, c)
    s64 = jnp.where(lo, s_sw, s)
    sn64_ref[0] = jnp.where((lane % 64) < 32, -s64, s64)


def _rope_tables(positions, ts):
    bsz, seq = positions.shape
    f64 = ROPE_THETA ** (-jnp.arange(32, dtype=F32) / 32)
    f128 = ROPE_THETA ** (-jnp.arange(64, dtype=F32) / 64)
    frow = jnp.concatenate([f128, f64, f64]).reshape(1, LANES)
    tab = jax.ShapeDtypeStruct((bsz, seq, LANES), F32)
    tspec = pl.BlockSpec((1, ts, LANES), lambda b, i: (b, i, 0))
    return pl.pallas_call(
        _tables_kernel,
        grid=(bsz, seq // ts),
        in_specs=[
            pl.BlockSpec((1, ts, 1), lambda b, i: (b, i, 0)),
            pl.BlockSpec((1, LANES), lambda b, i: (0, 0)),
        ],
        out_specs=[tspec] * 4,
        out_shape=[tab] * 4,
        compiler_params=_params("parallel", "parallel"),
        name="tables",
    )(positions.reshape(bsz, seq, 1), frow)


def _rope64(y, cs, sn):
    lane = lax.broadcasted_iota(jnp.int32, y.shape, 1)
    rot = jnp.where((lane % 64) < 32, pltpu.roll(y, 96, 1), pltpu.roll(y, 32, 1))
    return y * cs + rot * sn


def _rope128(y, cs, sn):
    return y * cs + pltpu.roll(y, 64, 1) * sn


def _inproj_kernel(x_ref, shift_ref, scale_ref, g_ref,
                   wqa_ref, wka_ref, wvx_ref, wqi_ref, wki_ref,
                   wqr_ref, wkr_ref, wvr_ref, wgr_ref,
                   cs64_ref, sn64_ref, cs128_ref, sn128_ref,
                   qa_ref, ka_ref, vx_ref, qi_ref, ki_ref, wi_ref,
                   qr_ref, kr_ref, vr_ref, gr_ref):
    x = x_ref[0]
    h = _rms(x) * g_ref[...] * (1.0 + scale_ref[0]) + shift_ref[0]
    hb = h.astype(BF16)
    cs64, sn64 = cs64_ref[0], sn64_ref[0]
    cs128, sn128 = cs128_ref[0], sn128_ref[0]

    def heads64(w_ref, o_ref, mult):
        for c in range(ATT_WIDTH // LANES):
            y = jnp.dot(hb, w_ref[:, c * LANES:(c + 1) * LANES], preferred_element_type=F32)
            y = (_rope64(y, cs64, sn64) * mult).astype(BF16)
            o_ref[0, 2 * c] = y[:, :64]
            o_ref[0, 2 * c + 1] = y[:, 64:]

    heads64(wqa_ref, qa_ref, ATT_HEAD_DIM ** -0.5)
    heads64(wka_ref, ka_ref, 1.0)
    heads64(wqi_ref, qi_ref, 1.0)

    for c in range(ATT_HEADS):
        y = jnp.dot(hb, wvx_ref[:, c * LANES:(c + 1) * LANES], preferred_element_type=F32)
        lane = lax.broadcasted_iota(jnp.int32, y.shape, 1)
        vx_ref[0, :, c * LANES:(c + 1) * LANES] = jnp.where(lane < 64, y, 1.0).astype(BF16)

    yk = jnp.dot(hb, wki_ref[...], preferred_element_type=F32)
    ki_ref[0] = _rope64(yk, cs64, sn64)[:, :64].astype(BF16)
    wi_ref[0] = yk[:, 64:64 + IDX_HEADS] * ((IDX_HEADS ** -0.5) * (IDX_HEAD_DIM ** -0.5))

    for c in range(RET_HEADS):
        sl = slice(c * LANES, (c + 1) * LANES)
        y = jnp.dot(hb, wqr_ref[:, sl], preferred_element_type=F32)
        qr_ref[0, :, sl] = _rope128(y, cs128, sn128).astype(BF16)
        y = jnp.dot(hb, wkr_ref[:, sl], preferred_element_type=F32)
        kr_ref[0, :, sl] = (_rope128(y, cs128, sn128) * (RET_HEAD_DIM ** -0.5)).astype(BF16)
    vr_ref[0] = jnp.dot(hb, wvr_ref[...], preferred_element_type=F32).astype(BF16)
    gr_ref[0] = jnp.dot(hb, wgr_ref[...], preferred_element_type=F32).astype(BF16)


def _inproj(x, shift, scale, g, weights, tables, ts):
    bsz, seq, _ = x.shape
    row = lambda b, i: (b, i, 0)
    per_b = lambda b, i: (b, 0, 0)
    const2 = lambda b, i: (0, 0)
    head_major = lambda b, i: (b, 0, i, 0)

    def wspec(w):
        return pl.BlockSpec(w.shape, const2)

    hm_shape = jax.ShapeDtypeStruct((bsz, ATT_HEADS, seq, 64), BF16)
    hm_spec = pl.BlockSpec((1, ATT_HEADS, ts, 64), head_major)
    wide = lambda n, dt=BF16: jax.ShapeDtypeStruct((bsz, seq, n), dt)
    wspec_out = lambda n: pl.BlockSpec((1, ts, n), row)
    return pl.pallas_call(
        _inproj_kernel,
        grid=(bsz, seq // ts),
        in_specs=[
            pl.BlockSpec((1, ts, D_MODEL), row),
            pl.BlockSpec((1, 1, D_MODEL), per_b),
            pl.BlockSpec((1, 1, D_MODEL), per_b),
            pl.BlockSpec((1, D_MODEL), const2),
            *[wspec(w) for w in weights],
            *[pl.BlockSpec((1, ts, LANES), row)] * 4,
        ],
        out_specs=[hm_spec, hm_spec, wspec_out(ATT_HEADS * LANES), hm_spec,
                   wspec_out(64), wspec_out(IDX_HEADS),
                   wspec_out(RET_WIDTH), wspec_out(RET_WIDTH), wspec_out(RET_WIDTH), wspec_out(RET_WIDTH)],
        out_shape=[hm_shape, hm_shape, wide(ATT_HEADS * LANES), hm_shape,
                   wide(64), wide(IDX_HEADS, F32),
                   wide(RET_WIDTH), wide(RET_WIDTH), wide(RET_WIDTH), wide(RET_WIDTH)],
        compiler_params=_params("parallel", "parallel"),
        name="inproj",
    )(x, shift, scale, g, *weights, *tables)


DSA_TQ = 256
DSA_TK = 256
DSA_STRIP = 64
INT_MIN = -2 ** 31


def _key_to_float(u):
    key = u ^ INT_MIN
    bits = jnp.where(key >= 0, key, key ^ 0x7FFFFFFF)
    return lax.bitcast_convert_type(bits, F32)


def _dsa_kernel(qi_ref, ki_ref, wi_ref, qa_ref, ka_ref, vx_ref, o_ref,
                sc_ref, wb_ref, m_ref, acc_ref, *, topk):
    tq, tk, strip = DSA_TQ, DSA_TK, DSA_STRIP
    qb = pl.program_id(1)
    nk = (qb + 1) * (tq // tk)

    w = wi_ref[0]
    for h in range(IDX_HEADS):
        wb_ref[h] = jnp.broadcast_to(w[:, h:h + 1], (tq, LANES))

    def score_chunk(kc, carry):
        k0 = pl.multiple_of(kc * tk, tk)
        kch = ki_ref[0, pl.ds(k0, tk), :]
        acc = jnp.zeros((tq, tk), F32)
        for h in range(IDX_HEADS):
            logit = lax.dot_general(qi_ref[0, h], kch, _NT, preferred_element_type=F32)
            wbh = wb_ref[h]
            acc = acc + jnp.maximum(logit, 0.0) * jnp.concatenate([wbh] * (tk // LANES), axis=1)
        row_pos = qb * tq + lax.broadcasted_iota(jnp.int32, (tq, tk), 0)
        col_pos = k0 + lax.broadcasted_iota(jnp.int32, (tq, tk), 1)
        sc_ref[kc] = jnp.where(col_pos <= row_pos, acc, NEG_INF)
        return carry

    lax.fori_loop(0, nk, score_chunk, 0)

    def strip_body(st, carry):
        r0 = pl.multiple_of(st * strip, strip)

        def count(pred):
            def body(kc, acc):
                return acc + jnp.where(pred(sc_ref[kc, pl.ds(r0, strip), :], kc), 1.0, 0.0)
            acc = lax.fori_loop(0, nk, body, jnp.zeros((strip, tk), F32))
            return jnp.sum(acc, axis=1, keepdims=True)

        def bit_body(i, u):
            trial = u | lax.shift_left(jnp.int32(1), 31 - i)
            cand = jnp.broadcast_to(_key_to_float(trial), (strip, tk))
            cnt = count(lambda x, kc: x >= cand)
            return jnp.where(cnt >= topk, trial, u)

        u = lax.fori_loop(0, 32, bit_body, jnp.zeros((strip, 1), jnp.int32))
        qpos = qb * tq + r0 + lax.broadcasted_iota(jnp.int32, (strip, 1), 0)
        thr1 = jnp.where(qpos < topk - 1, F32_LOWEST, _key_to_float(u))
        thr = jnp.broadcast_to(thr1, (strip, tk))
        cnt_ge = count(lambda x, kc: x >= thr)
        has_excess_ties = jnp.max(cnt_ge) > topk

        def col_index(kc):
            return kc * tk + lax.broadcasted_iota(jnp.int32, (strip, tk), 1)

        def tie_cut():
            need = topk - count(lambda x, kc: x > thr)

            def cut_body(i, c):
                trial = c | lax.shift_left(jnp.int32(1), 12 - i)
                tb = jnp.broadcast_to(trial, (strip, tk))
                f = count(lambda x, kc: (x == thr) & (col_index(kc) < tb))
                return jnp.where(f <= need, trial, c)

            return lax.fori_loop(0, 13, cut_body, jnp.zeros((strip, 1), jnp.int32))

        cut1 = lax.cond(has_excess_ties, tie_cut,
                        lambda: jnp.full((strip, 1), 2 ** 13 - 1, jnp.int32))
        cut = jnp.broadcast_to(cut1, (strip, tk))

        def mask_body(kc, c):
            x = sc_ref[kc, pl.ds(r0, strip), :]
            keep = (x > thr) | ((x == thr) & (col_index(kc) < cut))
            sc_ref[kc, pl.ds(r0, strip), :] = jnp.where(keep, 0.0, NEG_INF)
            return c

        lax.fori_loop(0, nk, mask_body, 0)
        return carry

    lax.fori_loop(0, tq // strip, strip_body, 0)

    for h in range(ATT_HEADS):
        q = qa_ref[0, h]
        m_ref[...] = jnp.full((tq, LANES), -1e30, F32)
        acc_ref[...] = jnp.zeros((tq, LANES), F32)

        def att_chunk(kc, carry):
            k0 = pl.multiple_of(kc * tk, tk)
            kch = ka_ref[0, h, pl.ds(k0, tk), :]
            s = lax.dot_general(q, kch, _NT, preferred_element_type=F32) + sc_ref[kc]
            m_prev = m_ref[...]
            m_new = jnp.maximum(m_prev, jnp.max(s, axis=1, keepdims=True))
            alpha = jnp.exp(m_prev - m_new)
            p = jnp.exp(s - jnp.concatenate([m_new] * (tk // LANES), axis=1))
            vch = vx_ref[0, pl.ds(k0, tk), h * LANES:(h + 1) * LANES]
            acc_ref[...] = alpha * acc_ref[...] + jnp.dot(p.astype(BF16), vch, preferred_element_type=F32)
            m_ref[...] = m_new
            return carry

        lax.fori_loop(0, nk, att_chunk, 0)
        acc = acc_ref[...]
        o = acc / pltpu.roll(acc, 64, 1)
        o_ref[0, :, h * 64:(h + 1) * 64] = o[:, :64].astype(BF16)


def _dsa(qi, ki, wi, qa, ka, vx):
    bsz, _, seq, _ = qi.shape
    tq, tk = DSA_TQ, DSA_TK
    topk = min(TOPK_MAX, seq // 4)
    assert seq % tq == 0 and tq % tk == 0 and tk >= topk
    blk_q = lambda b, i: (b, 0, i, 0)
    resident = dict(pipeline_mode=pl.Buffered(1))
    return pl.pallas_call(
        functools.partial(_dsa_kernel, topk=topk),
        grid=(bsz, seq // tq),
        in_specs=[
            pl.BlockSpec((1, IDX_HEADS, tq, 64), blk_q),
            pl.BlockSpec((1, seq, 64), lambda b, i: (b, 0, 0), **resident),
            pl.BlockSpec((1, tq, IDX_HEADS), lambda b, i: (b, i, 0)),
            pl.BlockSpec((1, ATT_HEADS, tq, 64), blk_q),
            pl.BlockSpec((1, ATT_HEADS, seq, 64), lambda b, i: (b, 0, 0, 0), **resident),
            pl.BlockSpec((1, seq, ATT_HEADS * LANES), lambda b, i: (b, 0, 0), **resident),
        ],
        out_specs=pl.BlockSpec((1, tq, ATT_WIDTH), lambda b, i: (b, i, 0)),
        out_shape=jax.ShapeDtypeStruct((bsz, seq, ATT_WIDTH), BF16),
        scratch_shapes=[
            pltpu.VMEM((seq // tk, tq, tk), F32),
            pltpu.VMEM((IDX_HEADS, tq, LANES), F32),
            pltpu.VMEM((tq, LANES), F32),
            pltpu.VMEM((tq, LANES), F32),
        ],
        compiler_params=_params("parallel", "arbitrary"),
        name="dsa",
    )(qi, ki, wi, qa, ka, vx)


RET_CHUNK = 256


def _ret_kernel(q_ref, k_ref, v_ref, g_ref, o_ref, state_ref, decay_ref):
    c = RET_CHUNK
    n = pl.program_id(1)
    row = lax.broadcasted_iota(jnp.int32, (c, c), 0)
    col = lax.broadcasted_iota(jnp.int32, (c, c), 1)
    idx = lax.broadcasted_iota(jnp.int32, (c, 1), 0).astype(F32)

    @pl.when(n == 0)
    def _():
        state_ref[...] = jnp.zeros_like(state_ref)
        for h in range(RET_HEADS):
            lg = math.log(1.0 - 2.0 ** (-5.0 - h))
            diff = (row - col).astype(F32)
            decay_ref[h] = jnp.where(row >= col, jnp.exp(lg * jnp.maximum(diff, 0.0)), 0.0)

    for h in range(RET_HEADS):
        lg = math.log(1.0 - 2.0 ** (-5.0 - h))
        sl = slice(h * LANES, (h + 1) * LANES)
        q = q_ref[0, :, sl]
        k = k_ref[0, :, sl]
        v = v_ref[0, :, sl]
        inner = lax.dot_general(q, k, _NT, preferred_element_type=F32) * decay_ref[h]
        o = jnp.dot(inner.astype(BF16), v, preferred_element_type=F32)
        qd = (q.astype(F32) * jnp.exp(lg * (idx + 1.0))).astype(BF16)
        state = state_ref[h]
        o = o + jnp.dot(qd, state.astype(BF16), preferred_element_type=F32)
        kd = (k.astype(F32) * jnp.exp(lg * (c - 1.0 - idx))).astype(BF16)
        kv = lax.dot_general(kd, v, _TN, preferred_element_type=F32)
        state_ref[h] = state * math.exp(lg * c) + kv
        gate = _silu(g_ref[0, :, sl].astype(F32))
        o_ref[0, :, sl] = (gate * _rms(o)).astype(BF16)


def _retention(qr, kr, vr, gr):
    bsz, seq, _ = qr.shape
    c = RET_CHUNK
    spec = pl.BlockSpec((1, c, RET_WIDTH), lambda b, i: (b, i, 0))
    return pl.pallas_call(
        _ret_kernel,
        grid=(bsz, seq // c),
        in_specs=[spec] * 4,
        out_specs=spec,
        out_shape=jax.ShapeDtypeStruct((bsz, seq, RET_WIDTH), BF16),
        scratch_shapes=[
            pltpu.VMEM((RET_HEADS, RET_HEAD_DIM, RET_HEAD_DIM), F32),
            pltpu.VMEM((RET_HEADS, c, c), F32),
        ],
        compiler_params=_params("arbitrary", "arbitrary"),
        name="ret",
    )(qr, kr, vr, gr)


ROUTER_E0 = N_GROUPS


def _mid_kernel(att_ref, ret_ref, x_ref, woa_ref, wor_ref, gpost_ref, gate1_ref,
                gpre_ref, scale2_ref, shift2_ref, wr_ref, br_ref,
                h1_ref, h2_ref, comb_ref):
    mix = jnp.dot(att_ref[0], woa_ref[...], preferred_element_type=F32)
    mix = mix + jnp.dot(ret_ref[0], wor_ref[...], preferred_element_type=F32)
    h1 = x_ref[0] + gate1_ref[0] * (_rms(mix) * gpost_ref[...])
    h1_ref[0] = h1
    h2 = (_rms(h1) * gpre_ref[...] * (1.0 + scale2_ref[0]) + shift2_ref[0]).astype(BF16)
    h2_ref[0] = h2

    logits = jnp.dot(h2, wr_ref[...], preferred_element_type=F32) + br_ref[...]
    lane = lax.broadcasted_iota(jnp.int32, logits.shape, 1)
    big = jnp.int32(LANES)

    def first_argmax(vals, vmax):
        return jnp.min(jnp.where(vals == vmax, lane, big), axis=1, keepdims=True)

    gl = jnp.where(lane < N_GROUPS, logits, NEG_INF)
    gexp = jnp.exp(gl - jnp.max(gl, axis=1, keepdims=True))
    gprob = gexp / jnp.sum(gexp, axis=1, keepdims=True)
    g_top = jnp.max(gprob, axis=1, keepdims=True)
    g_sel = first_argmax(gprob, g_top)
    e_lo = ROUTER_E0 + g_sel * EXPERTS_PER_GROUP
    in_group = (lane >= e_lo) & (lane < e_lo + EXPERTS_PER_GROUP)
    el = jnp.where(in_group, logits, NEG_INF)
    eexp = jnp.exp(el - jnp.max(el, axis=1, keepdims=True))
    eprob = jnp.where(in_group, eexp / jnp.sum(eexp, axis=1, keepdims=True), -1.0)
    top1 = jnp.max(eprob, axis=1, keepdims=True)
    idx1 = first_argmax(eprob, top1)
    rest = jnp.where(lane == idx1, -1.0, eprob)
    top2 = jnp.max(rest, axis=1, keepdims=True)
    idx2 = first_argmax(rest, top2)
    norm = g_top / (top1 + top2)
    comb_ref[0] = jnp.where(lane == idx1, top1 * norm, 0.0) + jnp.where(lane == idx2, top2 * norm, 0.0)


def _mid(att, ret, x, woa, wor, gpost, gate1, gpre, scale2, shift2, wr, br, ts):
    bsz, seq, _ = x.shape
    row = lambda b, i: (b, i, 0)
    per_b = lambda b, i: (b, 0, 0)
    const2 = lambda b, i: (0, 0)
    vec = pl.BlockSpec((1, D_MODEL), const2)
    bvec = pl.BlockSpec((1, 1, D_MODEL), per_b)
    return pl.pallas_call(
        _mid_kernel,
        grid=(bsz, seq // ts),
        in_specs=[
            pl.BlockSpec((1, ts, ATT_WIDTH), row),
            pl.BlockSpec((1, ts, RET_WIDTH), row),
            pl.BlockSpec((1, ts, D_MODEL), row),
            pl.BlockSpec(woa.shape, const2),
            pl.BlockSpec(wor.shape, const2),
            vec, bvec, vec, bvec, bvec,
            pl.BlockSpec(wr.shape, const2),
            pl.BlockSpec((1, LANES), const2),
        ],
        out_specs=[pl.BlockSpec((1, ts, D_MODEL), row),
                   pl.BlockSpec((1, ts, D_MODEL), row),
                   pl.BlockSpec((1, ts, LANES), row)],
        out_shape=[jax.ShapeDtypeStruct((bsz, seq, D_MODEL), F32),
                   jax.ShapeDtypeStruct((bsz, seq, D_MODEL), BF16),
                   jax.ShapeDtypeStruct((bsz, seq, LANES), F32)],
        compiler_params=_params("parallel", "parallel"),
        name="mid",
    )(att, ret, x, woa, wor, gpost, gate1, gpre, scale2, shift2, wr, br)


def _moe_kernel(h2_ref, comb_ref, h1_ref, gate2_ref, gpost_ref, wgu_ref, wd_ref, o_ref, acc_ref):
    e = pl.program_id(2)

    @pl.when(e == 0)
    def _():
        acc_ref[...] = jnp.zeros_like(acc_ref)

    t = h2_ref[0]
    gu = jnp.dot(t, wgu_ref[0], preferred_element_type=F32)
    hid = _silu(gu[:, :D_EXPERT]) * gu[:, D_EXPERT:]
    comb = comb_ref[0]
    lane = lax.broadcasted_iota(jnp.int32, comb.shape, 1)
    cw = jnp.sum(jnp.where(lane == ROUTER_E0 + e, comb, 0.0), axis=1, keepdims=True)
    acc_ref[...] += jnp.dot((hid * cw).astype(BF16), wd_ref[0], preferred_element_type=F32)

    @pl.when(e == N_EXPERTS - 1)
    def _():
        o_ref[0] = h1_ref[0] + gate2_ref[0] * (_rms(acc_ref[...]) * gpost_ref[...])


def _moe(h2, comb, h1, gate2, gpost, wgu, wd, tm):
    bsz, seq, _ = h2.shape
    row = lambda b, i, e: (b, i, 0)
    return pl.pallas_call(
        _moe_kernel,
        grid=(bsz, seq // tm, N_EXPERTS),
        in_specs=[
            pl.BlockSpec((1, tm, D_MODEL), row),
            pl.BlockSpec((1, tm, LANES), row),
            pl.BlockSpec((1, tm, D_MODEL), row),
            pl.BlockSpec((1, 1, D_MODEL), lambda b, i, e: (b, 0, 0)),
            pl.BlockSpec((1, D_MODEL), lambda b, i, e: (0, 0)),
            pl.BlockSpec((1, D_MODEL, 2 * D_EXPERT), lambda b, i, e: (e, 0, 0)),
            pl.BlockSpec((1, D_EXPERT, D_MODEL), lambda b, i, e: (e, 0, 0)),
        ],
        out_specs=pl.BlockSpec((1, tm, D_MODEL), row),
        out_shape=jax.ShapeDtypeStruct((bsz, seq, D_MODEL), F32),
        scratch_shapes=[pltpu.VMEM((tm, D_MODEL), F32)],
        compiler_params=_params("parallel", "parallel", "arbitrary"),
        name="moe",
    )(h2, comb, h1, gate2, gpost, wgu, wd)


def _layer(x, positions, mod, g_pre_mix, w_in, w_out, g_post_mix, g_pre_ffn,
           w_group, b_group, w_expert, b_expert, w_gate_exp, w_up_exp, w_down_exp, g_post_ffn):
    bsz, seq, _ = x.shape
    shift1, scale1, gate1, shift2, scale2, gate2 = [
        m.reshape(bsz, 1, D_MODEL) for m in jnp.split(mod, N_MOD, axis=-1)]
    vec = lambda g: g.reshape(1, D_MODEL)

    a, i8, r = ATT_WIDTH, IDX_HEADS, RET_WIDTH
    offs = np.cumsum([0, a, a, a, a, IDX_HEAD_DIM, i8, r, r, r, r])
    seg = lambda j: w_in[:, offs[j]:offs[j + 1]].astype(BF16)
    wv = w_in[:, offs[2]:offs[3]].reshape(D_MODEL, ATT_HEADS, ATT_HEAD_DIM)
    wvx = jnp.pad(wv, ((0, 0), (0, 0), (0, LANES - ATT_HEAD_DIM))).reshape(D_MODEL, ATT_HEADS * LANES)
    wki = jnp.pad(w_in[:, offs[4]:offs[6]], ((0, 0), (0, LANES - IDX_HEAD_DIM - i8)))
    weights = [seg(0), seg(1), wvx.astype(BF16), seg(3), wki.astype(BF16),
               seg(6), seg(7), seg(8), seg(9)]

    tables = _rope_tables(positions, 512)
    qa, ka, vx, qi, ki, wi, qr, kr, vr, gr = _inproj(
        x, shift1, scale1, vec(g_pre_mix), weights, tables, 512)
    att = _dsa(qi, ki, wi, qa, ka, vx)
    ret = _retention(qr, kr, vr, gr)

    w_router = jnp.pad(jnp.concatenate([w_group, w_expert], axis=1),
                       ((0, 0), (0, LANES - N_GROUPS - N_EXPERTS))).astype(BF16)
    b_router = jnp.pad(jnp.concatenate([b_group, b_expert]),
                       (0, LANES - N_GROUPS - N_EXPERTS)).reshape(1, LANES)
    h1, h2, comb = _mid(att, ret, x, w_out[:ATT_WIDTH].astype(BF16), w_out[ATT_WIDTH:].astype(BF16),
                        vec(g_post_mix), gate1, vec(g_pre_ffn), scale2, shift2, w_router, b_router, 512)

    wgu = jnp.concatenate([w_gate_exp, w_up_exp], axis=2).astype(BF16)
    return _moe(h2, comb, h1, gate2, vec(g_post_ffn), wgu, w_down_exp.astype(BF16), 1024)


def kernel(x, c, positions, g_pre_mix, w_ada, b_ada, w_in, w_out, g_post_mix, g_pre_ffn,
           w_group, b_group, w_expert, b_expert, w_gate_exp, w_up_exp, w_down_exp, g_post_ffn):
    h = x
    for l in range(w_in.shape[0]):
        mod = _ada(c, w_ada[l], b_ada[l])
        h = _layer(h, positions, mod, g_pre_mix[l], w_in[l], w_out[l], g_post_mix[l], g_pre_ffn[l],
                   w_group[l], b_group[l], w_expert[l], b_expert[l],
                   w_gate_exp[l], w_up_exp[l], w_down_exp[l], g_post_ffn[l])
    return h
```

```python
import functools
import math

import numpy as np
import jax
import jax.numpy as jnp
from jax import lax
from jax.experimental import pallas as pl
from jax.experimental.pallas import tpu as pltpu

D_MODEL = 1024
ATT_HEADS = 8
ATT_HEAD_DIM = 64
IDX_HEADS = 8
IDX_HEAD_DIM = 64
TOPK_MAX = 256
RET_HEADS = 4
RET_HEAD_DIM = 128
ROPE_THETA = 10000.0
ATT_WIDTH = ATT_HEADS * ATT_HEAD_DIM
RET_WIDTH = RET_HEADS * RET_HEAD_DIM
N_GROUPS = 4
EXPERTS_PER_GROUP = 8
N_EXPERTS = N_GROUPS * EXPERTS_PER_GROUP
D_EXPERT = 256
N_MOD = 6
EPS = 1e-6

LANES = 128
SUB = 8
VMEM_LIMIT = 56 * 1024 * 1024

DSA_TQ = 256
DSA_TK = 256
V_ROWS = 80

F32 = jnp.float32
BF16 = jnp.bfloat16
NEG_INF = float("-inf")
F32_LOWEST = float(np.finfo(np.float32).min)
INT_MIN = -2 ** 31

_NT = (((1,), (1,)), ((), ()))
_TN = (((0,), (0,)), ((), ()))


def _params(*sem):
    return pltpu.CompilerParams(dimension_semantics=sem, vmem_limit_bytes=VMEM_LIMIT)


def _rms(x):
    return x * lax.rsqrt(jnp.mean(x * x, axis=-1, keepdims=True) + EPS)


def _silu(x):
    return x * (1.0 / (1.0 + jnp.exp(-x)))


def _ada_kernel(c_ref, w_ref, b_ref, o_ref):
    a = _silu(c_ref[...]).astype(BF16)
    o_ref[...] = jnp.dot(a, w_ref[...].astype(BF16), preferred_element_type=F32) + b_ref[...]


def _ada(c, w, b):
    bsz = c.shape[0]
    n = w.shape[1]
    tn = D_MODEL
    return pl.pallas_call(
        _ada_kernel,
        grid=(n // tn,),
        in_specs=[
            pl.BlockSpec((bsz, D_MODEL), lambda j: (0, 0)),
            pl.BlockSpec((D_MODEL, tn), lambda j: (0, j)),
            pl.BlockSpec((1, tn), lambda j: (0, j)),
        ],
        out_specs=pl.BlockSpec((bsz, tn), lambda j: (0, j)),
        out_shape=jax.ShapeDtypeStruct((bsz, n), F32),
        compiler_params=_params("arbitrary"),
        name="ada",
    )(c, w, b.reshape(1, n))


def _tables_kernel(pos_ref, f_ref, cs64_ref, sn64_ref, cs128_ref, sn128_ref):
    ang = pos_ref[0].astype(F32) * f_ref[...]
    c = jnp.cos(ang)
    s = jnp.sin(ang)
    c_sw = pltpu.roll(c, 64, 1)
    s_sw = pltpu.roll(s, 64, 1)
    lane = lax.broadcasted_iota(jnp.int32, c.shape, 1)
    lo = lane < 64
    cs128_ref[0] = jnp.where(lo, c, c_sw)
    sn128_ref[0] = jnp.where(lo, -s, s_sw)
    cs64_ref[0] = jnp.where(lo, c_sw, c)
    s64 = jnp.where(lo, s_sw, s)
    sn64_ref[0] = jnp.where((lane % 64) < 32, -s64, s64)


def _rope_tables(positions, ts):
    bsz, seq = positions.shape
    f64 = ROPE_THETA ** (-jnp.arange(32, dtype=F32) / 32)
    f128 = ROPE_THETA ** (-jnp.arange(64, dtype=F32) / 64)
    frow = jnp.concatenate([f128, f64, f64]).reshape(1, LANES)
    tab = jax.ShapeDtypeStruct((bsz, seq, LANES), F32)
    tspec = pl.BlockSpec((1, ts, LANES), lambda b, i: (b, i, 0))
    return pl.pallas_call(
        _tables_kernel,
        grid=(bsz, seq // ts),
        in_specs=[
            pl.BlockSpec((1, ts, 1), lambda b, i: (b, i, 0)),
            pl.BlockSpec((1, LANES), lambda b, i: (0, 0)),
        ],
        out_specs=[tspec] * 4,
        out_shape=[tab] * 4,
        compiler_params=_params("parallel", "parallel"),
        name="tables",
    )(positions.reshape(bsz, seq, 1), frow)


def _rope64(y, cs, sn):
    lane = lax.broadcasted_iota(jnp.int32, y.shape, 1)
    rot = jnp.where((lane % 64) < 32, pltpu.roll(y, 96, 1), pltpu.roll(y, 32, 1))
    return y * cs + rot * sn


def _rope128(y, cs, sn):
    return y * cs + pltpu.roll(y, 64, 1) * sn


def _inproj_kernel(x_ref, shift_ref, scale_ref, g_ref,
                   wqa_ref, wka_ref, wv_ref, wqi_ref, wki_ref,
                   wqr_ref, wkr_ref, wvr_ref, wgr_ref,
                   cs64_ref, sn64_ref, cs128_ref, sn128_ref,
                   qa_ref, ka_ref, vt_ref, qi_ref, ki_ref, wit_ref,
                   qr_ref, kr_ref, vr_ref, gr_ref):
    x = x_ref[0]
    h = _rms(x) * g_ref[...] * (1.0 + scale_ref[0]) + shift_ref[0]
    hb = h.astype(BF16)
    cs64, sn64 = cs64_ref[0], sn64_ref[0]
    cs128, sn128 = cs128_ref[0], sn128_ref[0]

    def heads64(w_ref, o_ref, mult):
        for c in range(ATT_WIDTH // LANES):
            y = jnp.dot(hb, w_ref[:, c * LANES:(c + 1) * LANES], preferred_element_type=F32)
            y = (_rope64(y, cs64, sn64) * mult).astype(BF16)
            o_ref[0, 2 * c] = y[:, :64]
            o_ref[0, 2 * c + 1] = y[:, 64:]

    heads64(wqa_ref, qa_ref, ATT_HEAD_DIM ** -0.5)
    heads64(wka_ref, ka_ref, 1.0)
    heads64(wqi_ref, qi_ref, 1.0)

    ts = hb.shape[0]
    vt = jnp.dot(hb, wv_ref[...], preferred_element_type=F32).T
    for h in range(ATT_HEADS):
        for j in range(ts // DSA_TK):
            blk = vt[h * 64:(h + 1) * 64, j * DSA_TK:(j + 1) * DSA_TK]
            vt_ref[0, h, j, :64, :] = blk.astype(BF16)
            vt_ref[0, h, j, 64:, :] = jnp.ones((V_ROWS - 64, DSA_TK), BF16)

    yk = jnp.dot(hb, wki_ref[...], preferred_element_type=F32)
    ki_ref[0] = _rope64(yk, cs64, sn64)[:, :64].astype(BF16)
    wit_ref[0] = yk.T[64:64 + IDX_HEADS] * ((IDX_HEADS ** -0.5) * (IDX_HEAD_DIM ** -0.5))

    for c in range(RET_HEADS):
        sl = slice(c * LANES, (c + 1) * LANES)
        y = jnp.dot(hb, wqr_ref[:, sl], preferred_element_type=F32)
        qr_ref[0, :, sl] = _rope128(y, cs128, sn128).astype(BF16)
        y = jnp.dot(hb, wkr_ref[:, sl], preferred_element_type=F32)
        kr_ref[0, :, sl] = (_rope128(y, cs128, sn128) * (RET_HEAD_DIM ** -0.5)).astype(BF16)
    vr_ref[0] = jnp.dot(hb, wvr_ref[...], preferred_element_type=F32).astype(BF16)
    gr_ref[0] = jnp.dot(hb, wgr_ref[...], preferred_element_type=F32).astype(BF16)


def _inproj(x, shift, scale, g, weights, tables, ts):
    bsz, seq, _ = x.shape
    row = lambda b, i: (b, i, 0)
    per_b = lambda b, i: (b, 0, 0)
    const2 = lambda b, i: (0, 0)
    head_major = lambda b, i: (b, 0, i, 0)

    def wspec(w):
        return pl.BlockSpec(w.shape, const2)

    hm_shape = jax.ShapeDtypeStruct((bsz, ATT_HEADS, seq, 64), BF16)
    hm_spec = pl.BlockSpec((1, ATT_HEADS, ts, 64), head_major)
    wide = lambda n, dt=BF16: jax.ShapeDtypeStruct((bsz, seq, n), dt)
    wspec_out = lambda n: pl.BlockSpec((1, ts, n), row)
    return pl.pallas_call(
        _inproj_kernel,
        grid=(bsz, seq // ts),
        in_specs=[
            pl.BlockSpec((1, ts, D_MODEL), row),
            pl.BlockSpec((1, 1, D_MODEL), per_b),
            pl.BlockSpec((1, 1, D_MODEL), per_b),
            pl.BlockSpec((1, D_MODEL), const2),
            *[wspec(w) for w in weights],
            *[pl.BlockSpec((1, ts, LANES), row)] * 4,
        ],
        out_specs=[hm_spec, hm_spec,
                   pl.BlockSpec((1, ATT_HEADS, ts // DSA_TK, V_ROWS, DSA_TK), lambda b, i: (b, 0, i, 0, 0)),
                   hm_spec, wspec_out(64),
                   pl.BlockSpec((1, IDX_HEADS, ts), lambda b, i: (b, 0, i)),
                   wspec_out(RET_WIDTH), wspec_out(RET_WIDTH), wspec_out(RET_WIDTH), wspec_out(RET_WIDTH)],
        out_shape=[hm_shape, hm_shape,
                   jax.ShapeDtypeStruct((bsz, ATT_HEADS, seq // DSA_TK, V_ROWS, DSA_TK), BF16),
                   hm_shape, wide(64),
                   jax.ShapeDtypeStruct((bsz, IDX_HEADS, seq), F32),
                   wide(RET_WIDTH), wide(RET_WIDTH), wide(RET_WIDTH), wide(RET_WIDTH)],
        compiler_params=_params("parallel", "parallel"),
        name="inproj",
    )(x, shift, scale, g, *weights, *tables)


def _key_to_float(u):
    key = u ^ INT_MIN
    bits = jnp.where(key >= 0, key, key ^ 0x7FFFFFFF)
    return lax.bitcast_convert_type(bits, F32)


def _dsa_kernel(qi_ref, ki_ref, wit_ref, qa_ref, ka_ref, vt_ref, o_ref,
                sc_ref, m_ref, acc_ref, *, topk):
    tq, tk = DSA_TQ, DSA_TK
    qb = pl.program_id(1)
    nk = (qb + 1) * (tq // tk)
    grp = (tk // SUB, SUB, tq)
    q_pos = qb * tq + lax.broadcasted_iota(jnp.int32, (SUB, tq), 1)

    def key_index(kc):
        return (kc * tk + lax.broadcasted_iota(jnp.int32, grp, 0) * SUB
                + lax.broadcasted_iota(jnp.int32, grp, 1))

    wt = wit_ref[0]

    def score_chunk(kc, carry):
        k0 = pl.multiple_of(kc * tk, tk)
        kch = ki_ref[0, pl.ds(k0, tk), :]
        acc = jnp.zeros((tk, tq), F32)
        for h in range(IDX_HEADS):
            logit = lax.dot_general(kch, qi_ref[0, h], _NT, preferred_element_type=F32)
            acc = acc + jnp.maximum(logit, 0.0) * wt[h:h + 1, :]
        sc_ref[kc] = jnp.where(key_index(kc) <= q_pos[None], acc.reshape(grp), NEG_INF)
        return carry

    lax.fori_loop(0, nk, score_chunk, 0)

    def count(pred):
        def body(kc, acc):
            return acc + jnp.sum(jnp.where(pred(sc_ref[kc], kc), 1.0, 0.0), axis=0)
        acc = lax.fori_loop(0, nk, body, jnp.zeros((SUB, tq), F32))
        return jnp.sum(acc, axis=0, keepdims=True)

    def rows(v):
        return jnp.broadcast_to(v, (SUB, tq))[None]

    def bit_body(i, u):
        trial = u | lax.shift_left(jnp.int32(1), 31 - i)
        cand = rows(_key_to_float(trial))
        cnt = count(lambda x, kc: x >= cand)
        return jnp.where(cnt >= topk, trial, u)

    u = lax.fori_loop(0, 32, bit_body, jnp.zeros((1, tq), jnp.int32))
    thr1 = jnp.where(q_pos[:1] < topk - 1, F32_LOWEST, _key_to_float(u))
    thr = rows(thr1)
    cnt_ge = count(lambda x, kc: x >= thr)
    has_excess_ties = jnp.max(cnt_ge) > topk

    def tie_cut():
        need = topk - count(lambda x, kc: x > thr)

        def cut_body(i, c):
            trial = c | lax.shift_left(jnp.int32(1), 12 - i)
            tb = rows(trial)
            f = count(lambda x, kc: (x == thr) & (key_index(kc) < tb))
            return jnp.where(f <= need, trial, c)

        return lax.fori_loop(0, 13, cut_body, jnp.zeros((1, tq), jnp.int32))

    cut = rows(lax.cond(has_excess_ties, tie_cut,
                        lambda: jnp.full((1, tq), 2 ** 13 - 1, jnp.int32)))

    def mask_body(kc, c):
        x = sc_ref[kc]
        keep = (x > thr) | ((x == thr) & (key_index(kc) < cut))
        sc_ref[kc] = jnp.where(keep, 0.0, NEG_INF)
        return c

    lax.fori_loop(0, nk, mask_body, 0)

    def logits(kc, h):
        k0 = pl.multiple_of(kc * tk, tk)
        kch = ka_ref[0, h, pl.ds(k0, tk), :]
        s = lax.dot_general(kch, qa_ref[0, h], _NT, preferred_element_type=F32)
        return s.reshape(grp) + sc_ref[kc]

    def max_chunk(kc, m):
        return jnp.stack([jnp.maximum(m[h], jnp.max(logits(kc, h), axis=0))
                          for h in range(ATT_HEADS)])

    m8 = lax.fori_loop(0, nk, max_chunk, jnp.full((ATT_HEADS, SUB, tq), NEG_INF, F32))
    m_ref[...] = jnp.broadcast_to(jnp.max(m8, axis=1, keepdims=True), m8.shape)
    acc_ref[...] = jnp.zeros(acc_ref.shape, F32)

    def pv_chunk(kc, carry):
        ahead = 8
        ss = [logits(kc, h) for h in range(ahead)]
        for h in range(ATT_HEADS):
            if h + ahead < ATT_HEADS:
                ss.append(logits(kc, h + ahead))
            p = jnp.exp(ss[h] - m_ref[h][None]).reshape(tk, tq).astype(BF16)
            acc_ref[h] += jnp.dot(vt_ref[0, h, kc], p, preferred_element_type=F32)
        return carry

    lax.fori_loop(0, nk, pv_chunk, 0)
    outs = []
    for h in range(ATT_HEADS):
        acc = acc_ref[h]
        outs.append(acc[:64] / acc[64:65])
    o_ref[0] = jnp.concatenate(outs, axis=0).T.astype(BF16)


def _dsa(qi, ki, wit, qa, ka, vt):
    bsz, _, seq, _ = qi.shape
    tq, tk = DSA_TQ, DSA_TK
    topk = min(TOPK_MAX, seq // 4)
    assert seq % tq == 0 and tq % tk == 0 and tk >= topk
    blk_q = lambda b, i: (b, 0, i, 0)
    resident = dict(pipeline_mode=pl.Buffered(1))
    return pl.pallas_call(
        functools.partial(_dsa_kernel, topk=topk),
        grid=(bsz, seq // tq),
        in_specs=[
            pl.BlockSpec((1, IDX_HEADS, tq, 64), blk_q),
            pl.BlockSpec((1, seq, 64), lambda b, i: (b, 0, 0), **resident),
            pl.BlockSpec((1, IDX_HEADS, tq), lambda b, i: (b, 0, i)),
            pl.BlockSpec((1, ATT_HEADS, tq, 64), blk_q),
            pl.BlockSpec((1, ATT_HEADS, seq, 64), lambda b, i: (b, 0, 0, 0), **resident),
            pl.BlockSpec((1, ATT_HEADS, seq // tk, V_ROWS, tk), lambda b, i: (b, 0, 0, 0, 0), **resident),
        ],
        out_specs=pl.BlockSpec((1, tq, ATT_WIDTH), lambda b, i: (b, i, 0)),
        out_shape=jax.ShapeDtypeStruct((bsz, seq, ATT_WIDTH), BF16),
        scratch_shapes=[
            pltpu.VMEM((seq // tk, tk // SUB, SUB, tq), F32),
            pltpu.VMEM((ATT_HEADS, SUB, tq), F32),
            pltpu.VMEM((ATT_HEADS, V_ROWS, tq), F32),
        ],
        compiler_params=_params("parallel", "arbitrary"),
        name="dsa",
    )(qi, ki, wit, qa, ka, vt)


RET_CHUNK = 256


def _ret_kernel(q_ref, k_ref, v_ref, g_ref, o_ref, state_ref, decay_ref):
    c = RET_CHUNK
    n = pl.program_id(1)
    row = lax.broadcasted_iota(jnp.int32, (c, c), 0)
    col = lax.broadcasted_iota(jnp.int32, (c, c), 1)
    idx = lax.broadcasted_iota(jnp.int32, (c, 1), 0).astype(F32)

    @pl.when(n == 0)
    def _():
        state_ref[...] = jnp.zeros_like(state_ref)
        for h in range(RET_HEADS):
            lg = math.log(1.0 - 2.0 ** (-5.0 - h))
            diff = (row - col).astype(F32)
            decay_ref[h] = jnp.where(row >= col, jnp.exp(lg * jnp.maximum(diff, 0.0)), 0.0)

    for h in range(RET_HEADS):
        lg = math.log(1.0 - 2.0 ** (-5.0 - h))
        sl = slice(h * LANES, (h + 1) * LANES)
        q = q_ref[0, :, sl]
        k = k_ref[0, :, sl]
        v = v_ref[0, :, sl]
        inner = lax.dot_general(q, k, _NT, preferred_element_type=F32) * decay_ref[h]
        o = jnp.dot(inner.astype(BF16), v, preferred_element_type=F32)
        qd = (q.astype(F32) * jnp.exp(lg * (idx + 1.0))).astype(BF16)
        state = state_ref[h]
        o = o + jnp.dot(qd, state.astype(BF16), preferred_element_type=F32)
        kd = (k.astype(F32) * jnp.exp(lg * (c - 1.0 - idx))).astype(BF16)
        kv = lax.dot_general(kd, v, _TN, preferred_element_type=F32)
        state_ref[h] = state * math.exp(lg * c) + kv
        gate = _silu(g_ref[0, :, sl].astype(F32))
        o_ref[0, :, sl] = (gate * _rms(o)).astype(BF16)


def _retention(qr, kr, vr, gr):
    bsz, seq, _ = qr.shape
    c = RET_CHUNK
    spec = pl.BlockSpec((1, c, RET_WIDTH), lambda b, i: (b, i, 0))
    return pl.pallas_call(
        _ret_kernel,
        grid=(bsz, seq // c),
        in_specs=[spec] * 4,
        out_specs=spec,
        out_shape=jax.ShapeDtypeStruct((bsz, seq, RET_WIDTH), BF16),
        scratch_shapes=[
            pltpu.VMEM((RET_HEADS, RET_HEAD_DIM, RET_HEAD_DIM), F32),
            pltpu.VMEM((RET_HEADS, c, c), F32),
        ],
        compiler_params=_params("arbitrary", "arbitrary"),
        name="ret",
    )(qr, kr, vr, gr)


ROUTER_E0 = N_GROUPS


def _mid_kernel(att_ref, ret_ref, x_ref, woa_ref, wor_ref, gpost_ref, gate1_ref,
                gpre_ref, scale2_ref, shift2_ref, wr_ref, br_ref,
                h1_ref, h2_ref, comb_ref):
    mix = jnp.dot(att_ref[0], woa_ref[...], preferred_element_type=F32)
    mix = mix + jnp.dot(ret_ref[0], wor_ref[...], preferred_element_type=F32)
    h1 = x_ref[0] + gate1_ref[0] * (_rms(mix) * gpost_ref[...])
    h1_ref[0] = h1
    h2 = (_rms(h1) * gpre_ref[...] * (1.0 + scale2_ref[0]) + shift2_ref[0]).astype(BF16)
    h2_ref[0] = h2

    logits = jnp.dot(h2, wr_ref[...], preferred_element_type=F32) + br_ref[...]
    lane = lax.broadcasted_iota(jnp.int32, logits.shape, 1)
    big = jnp.int32(LANES)

    def first_argmax(vals, vmax):
        return jnp.min(jnp.where(vals == vmax, lane, big), axis=1, keepdims=True)

    gl = jnp.where(lane < N_GROUPS, logits, NEG_INF)
    gexp = jnp.exp(gl - jnp.max(gl, axis=1, keepdims=True))
    gprob = gexp / jnp.sum(gexp, axis=1, keepdims=True)
    g_top = jnp.max(gprob, axis=1, keepdims=True)
    g_sel = first_argmax(gprob, g_top)
    e_lo = ROUTER_E0 + g_sel * EXPERTS_PER_GROUP
    in_group = (lane >= e_lo) & (lane < e_lo + EXPERTS_PER_GROUP)
    el = jnp.where(in_group, logits, NEG_INF)
    eexp = jnp.exp(el - jnp.max(el, axis=1, keepdims=True))
    eprob = jnp.where(in_group, eexp / jnp.sum(eexp, axis=1, keepdims=True), -1.0)
    top1 = jnp.max(eprob, axis=1, keepdims=True)
    idx1 = first_argmax(eprob, top1)
    rest = jnp.where(lane == idx1, -1.0, eprob)
    top2 = jnp.max(rest, axis=1, keepdims=True)
    idx2 = first_argmax(rest, top2)
    norm = g_top / (top1 + top2)
    comb_ref[0] = jnp.where(lane == idx1, top1 * norm, 0.0) + jnp.where(lane == idx2, top2 * norm, 0.0)


def _mid(att, ret, x, woa, wor, gpost, gate1, gpre, scale2, shift2, wr, br, ts):
    bsz, seq, _ = x.shape
    row = lambda b, i: (b, i, 0)
    per_b = lambda b, i: (b, 0, 0)
    const2 = lambda b, i: (0, 0)
    vec = pl.BlockSpec((1, D_MODEL), const2)
    bvec = pl.BlockSpec((1, 1, D_MODEL), per_b)
    return pl.pallas_call(
        _mid_kernel,
        grid=(bsz, seq // ts),
        in_specs=[
            pl.BlockSpec((1, ts, ATT_WIDTH), row),
            pl.BlockSpec((1, ts, RET_WIDTH), row),
            pl.BlockSpec((1, ts, D_MODEL), row),
            pl.BlockSpec(woa.shape, const2),
            pl.BlockSpec(wor.shape, const2),
            vec, bvec, vec, bvec, bvec,
            pl.BlockSpec(wr.shape, const2),
            pl.BlockSpec((1, LANES), const2),
        ],
        out_specs=[pl.BlockSpec((1, ts, D_MODEL), row),
                   pl.BlockSpec((1, ts, D_MODEL), row),
                   pl.BlockSpec((1, ts, LANES), row)],
        out_shape=[jax.ShapeDtypeStruct((bsz, seq, D_MODEL), F32),
                   jax.ShapeDtypeStruct((bsz, seq, D_MODEL), BF16),
                   jax.ShapeDtypeStruct((bsz, seq, LANES), F32)],
        compiler_params=_params("parallel", "parallel"),
        name="mid",
    )(att, ret, x, woa, wor, gpost, gate1, gpre, scale2, shift2, wr, br)


def _moe_kernel(h2_ref, comb_ref, h1_ref, gate2_ref, gpost_ref, wgu_ref, wd_ref, o_ref, acc_ref):
    e = pl.program_id(2)

    @pl.when(e == 0)
    def _():
        acc_ref[...] = jnp.zeros_like(acc_ref)

    t = h2_ref[0]
    gu = jnp.dot(t, wgu_ref[0], preferred_element_type=F32)
    hid = _silu(gu[:, :D_EXPERT]) * gu[:, D_EXPERT:]
    comb = comb_ref[0]
    lane = lax.broadcasted_iota(jnp.int32, comb.shape, 1)
    cw = jnp.sum(jnp.where(lane == ROUTER_E0 + e, comb, 0.0), axis=1, keepdims=True)
    acc_ref[...] += jnp.dot((hid * cw).astype(BF16), wd_ref[0], preferred_element_type=F32)

    @pl.when(e == N_EXPERTS - 1)
    def _():
        o_ref[0] = h1_ref[0] + gate2_ref[0] * (_rms(acc_ref[...]) * gpost_ref[...])


def _moe(h2, comb, h1, gate2, gpost, wgu, wd, tm):
    bsz, seq, _ = h2.shape
    row = lambda b, i, e: (b, i, 0)
    return pl.pallas_call(
        _moe_kernel,
        grid=(bsz, seq // tm, N_EXPERTS),
        in_specs=[
            pl.BlockSpec((1, tm, D_MODEL), row),
            pl.BlockSpec((1, tm, LANES), row),
            pl.BlockSpec((1, tm, D_MODEL), row),
            pl.BlockSpec((1, 1, D_MODEL), lambda b, i, e: (b, 0, 0)),
            pl.BlockSpec((1, D_MODEL), lambda b, i, e: (0, 0)),
            pl.BlockSpec((1, D_MODEL, 2 * D_EXPERT), lambda b, i, e: (e, 0, 0)),
            pl.BlockSpec((1, D_EXPERT, D_MODEL), lambda b, i, e: (e, 0, 0)),
        ],
        out_specs=pl.BlockSpec((1, tm, D_MODEL), row),
        out_shape=jax.ShapeDtypeStruct((bsz, seq, D_MODEL), F32),
        scratch_shapes=[pltpu.VMEM((tm, D_MODEL), F32)],
        compiler_params=_params("parallel", "parallel", "arbitrary"),
        name="moe",
    )(h2, comb, h1, gate2, gpost, wgu, wd)


def _layer(x, positions, mod, g_pre_mix, w_in, w_out, g_post_mix, g_pre_ffn,
           w_group, b_group, w_expert, b_expert, w_gate_exp, w_up_exp, w_down_exp, g_post_ffn):
    bsz, seq, _ = x.shape
    shift1, scale1, gate1, shift2, scale2, gate2 = [
        m.reshape(bsz, 1, D_MODEL) for m in jnp.split(mod, N_MOD, axis=-1)]
    vec = lambda g: g.reshape(1, D_MODEL)

    a, i8, r = ATT_WIDTH, IDX_HEADS, RET_WIDTH
    offs = np.cumsum([0, a, a, a, a, IDX_HEAD_DIM, i8, r, r, r, r])
    seg = lambda j: w_in[:, offs[j]:offs[j + 1]].astype(BF16)
    wki = jnp.pad(w_in[:, offs[4]:offs[6]], ((0, 0), (0, LANES - IDX_HEAD_DIM - i8)))
    weights = [seg(0), seg(1), seg(2), seg(3), wki.astype(BF16),
               seg(6), seg(7), seg(8), seg(9)]

    tables = _rope_tables(positions, 512)
    qa, ka, vt, qi, ki, wit, qr, kr, vr, gr = _inproj(
        x, shift1, scale1, vec(g_pre_mix), weights, tables, 512)
    att = _dsa(qi, ki, wit, qa, ka, vt)
    ret = _retention(qr, kr, vr, gr)

    w_router = jnp.pad(jnp.concatenate([w_group, w_expert], axis=1),
                       ((0, 0), (0, LANES - N_GROUPS - N_EXPERTS))).astype(BF16)
    b_router = jnp.pad(jnp.concatenate([b_group, b_expert]),
                       (0, LANES - N_GROUPS - N_EXPERTS)).reshape(1, LANES)
    h1, h2, comb = _mid(att, ret, x, w_out[:ATT_WIDTH].astype(BF16), w_out[ATT_WIDTH:].astype(BF16),
                        vec(g_post_mix), gate1, vec(g_pre_ffn), scale2, shift2, w_router, b_router, 512)

    wgu = jnp.concatenate([w_gate_exp, w_up_exp], axis=2).astype(BF16)
    return _moe(h2, comb, h1, gate2, vec(g_post_ffn), wgu, w_down_exp.astype(BF16), 1024)


def kernel(x, c, positions, g_pre_mix, w_ada, b_ada, w_in, w_out, g_post_mix, g_pre_ffn,
           w_group, b_group, w_expert, b_expert, w_gate_exp, w_up_exp, w_down_exp, g_post_ffn):
    h = x
    for l in range(w_in.shape[0]):
        mod = _ada(c, w_ada[l], b_ada[l])
        h = _layer(h, positions, mod, g_pre_mix[l], w_in[l], w_out[l], g_post_mix[l], g_pre_ffn[l],
                   w_group[l], b_group[l], w_expert[l], b_expert[l],
                   w_gate_exp[l], w_up_exp[l], w_down_exp[l], g_post_ffn[l])
    return h
```

```python
import functools
import math

import numpy as np
import jax
import jax.numpy as jnp
from jax import lax
from jax.experimental import pallas as pl
from jax.experimental.pallas import tpu as pltpu

D_MODEL = 1024
ATT_HEADS = 8
ATT_HEAD_DIM = 64
IDX_HEADS = 8
IDX_HEAD_DIM = 64
TOPK_MAX = 256
RET_HEADS = 4
RET_HEAD_DIM = 128
ROPE_THETA = 10000.0
ATT_WIDTH = ATT_HEADS * ATT_HEAD_DIM
RET_WIDTH = RET_HEADS * RET_HEAD_DIM
N_GROUPS = 4
EXPERTS_PER_GROUP = 8
N_EXPERTS = N_GROUPS * EXPERTS_PER_GROUP
D_EXPERT = 256
N_MOD = 6
EPS = 1e-6

LANES = 128
SUB = 8
VMEM_LIMIT = 56 * 1024 * 1024

DSA_TQ = 256
DSA_TK = 256
V_ROWS = 80

F32 = jnp.float32
BF16 = jnp.bfloat16
NEG_INF = float("-inf")
F32_LOWEST = float(np.finfo(np.float32).min)
INT_MIN = -2 ** 31
DENOM_FLOOR = 1e-30

_NT = (((1,), (1,)), ((), ()))
_TN = (((0,), (0,)), ((), ()))


def _params(*sem):
    return pltpu.CompilerParams(dimension_semantics=sem, vmem_limit_bytes=VMEM_LIMIT)


def _rms(x):
    return x * lax.rsqrt(jnp.mean(x * x, axis=-1, keepdims=True) + EPS)


def _silu(x):
    return x * (1.0 / (1.0 + jnp.exp(-x)))


def _ada_kernel(c_ref, w_ref, b_ref, o_ref):
    a = _silu(c_ref[...]).astype(BF16)
    o_ref[...] = jnp.dot(a, w_ref[...].astype(BF16), preferred_element_type=F32) + b_ref[...]


def _ada(c, w, b):
    bsz = c.shape[0]
    n = w.shape[1]
    tn = D_MODEL
    return pl.pallas_call(
        _ada_kernel,
        grid=(n // tn,),
        in_specs=[
            pl.BlockSpec((bsz, D_MODEL), lambda j: (0, 0)),
            pl.BlockSpec((D_MODEL, tn), lambda j: (0, j)),
            pl.BlockSpec((1, tn), lambda j: (0, j)),
        ],
        out_specs=pl.BlockSpec((bsz, tn), lambda j: (0, j)),
        out_shape=jax.ShapeDtypeStruct((bsz, n), F32),
        compiler_params=_params("arbitrary"),
        name="ada",
    )(c, w, b.reshape(1, n))


def _tables_kernel(pos_ref, f_ref, cs64_ref, sn64_ref, cs128_ref, sn128_ref):
    ang = pos_ref[0].astype(F32) * f_ref[...]
    c = jnp.cos(ang)
    s = jnp.sin(ang)
    c_sw = pltpu.roll(c, 64, 1)
    s_sw = pltpu.roll(s, 64, 1)
    lane = lax.broadcasted_iota(jnp.int32, c.shape, 1)
    lo = lane < 64
    cs128_ref[0] = jnp.where(lo, c, c_sw)
    sn128_ref[0] = jnp.where(lo, -s, s_sw)
    cs64_ref[0] = jnp.where(lo, c_sw, c)
    s64 = jnp.where(lo, s_sw, s)
    sn64_ref[0] = jnp.where((lane % 64) < 32, -s64, s64)


def _rope_tables(positions, ts):
    bsz, seq = positions.shape
    f64 = ROPE_THETA ** (-jnp.arange(32, dtype=F32) / 32)
    f128 = ROPE_THETA ** (-jnp.arange(64, dtype=F32) / 64)
    frow = jnp.concatenate([f128, f64, f64]).reshape(1, LANES)
    tab = jax.ShapeDtypeStruct((bsz, seq, LANES), F32)
    tspec = pl.BlockSpec((1, ts, LANES), lambda b, i: (b, i, 0))
    return pl.pallas_call(
        _tables_kernel,
        grid=(bsz, seq // ts),
        in_specs=[
            pl.BlockSpec((1, ts, 1), lambda b, i: (b, i, 0)),
            pl.BlockSpec((1, LANES), lambda b, i: (0, 0)),
        ],
        out_specs=[tspec] * 4,
        out_shape=[tab] * 4,
        compiler_params=_params("parallel", "parallel"),
        name="tables",
    )(positions.reshape(bsz, seq, 1), frow)


def _rope64(y, cs, sn):
    lane = lax.broadcasted_iota(jnp.int32, y.shape, 1)
    rot = jnp.where((lane % 64) < 32, pltpu.roll(y, 96, 1), pltpu.roll(y, 32, 1))
    return y * cs + rot * sn


def _rope128(y, cs, sn):
    return y * cs + pltpu.roll(y, 64, 1) * sn


def _inproj_kernel(x_ref, shift_ref, scale_ref, g_ref,
                   wqa_ref, wka_ref, wv_ref, wqi_ref, wki_ref,
                   wqr_ref, wkr_ref, wvr_ref, wgr_ref,
                   cs64_ref, sn64_ref, cs128_ref, sn128_ref,
                   qa_ref, ka_ref, vt_ref, qi_ref, ki_ref, wit_ref,
                   qr_ref, kr_ref, vr_ref, gr_ref):
    x = x_ref[0]
    h = _rms(x) * g_ref[...] * (1.0 + scale_ref[0]) + shift_ref[0]
    hb = h.astype(BF16)
    cs64, sn64 = cs64_ref[0], sn64_ref[0]
    cs128, sn128 = cs128_ref[0], sn128_ref[0]

    def heads64(w_ref, o_ref, mult):
        for c in range(ATT_WIDTH // LANES):
            y = jnp.dot(hb, w_ref[:, c * LANES:(c + 1) * LANES], preferred_element_type=F32)
            y = (_rope64(y, cs64, sn64) * mult).astype(BF16)
            o_ref[0, 2 * c] = y[:, :64]
            o_ref[0, 2 * c + 1] = y[:, 64:]

    heads64(wqa_ref, qa_ref, ATT_HEAD_DIM ** -0.5)
    heads64(wka_ref, ka_ref, 1.0)
    heads64(wqi_ref, qi_ref, 1.0)

    ts = hb.shape[0]
    vt = jnp.dot(hb, wv_ref[...], preferred_element_type=F32).T
    for h in range(ATT_HEADS):
        for j in range(ts // DSA_TK):
            blk = vt[h * 64:(h + 1) * 64, j * DSA_TK:(j + 1) * DSA_TK]
            vt_ref[0, h, j, :64, :] = blk.astype(BF16)
            vt_ref[0, h, j, 64:, :] = jnp.ones((V_ROWS - 64, DSA_TK), BF16)

    yk = jnp.dot(hb, wki_ref[...], preferred_element_type=F32)
    ki_ref[0] = _rope64(yk, cs64, sn64)[:, :64].astype(BF16)
    wit_ref[0] = yk.T[64:64 + IDX_HEADS] * ((IDX_HEADS ** -0.5) * (IDX_HEAD_DIM ** -0.5))

    for c in range(RET_HEADS):
        sl = slice(c * LANES, (c + 1) * LANES)
        y = jnp.dot(hb, wqr_ref[:, sl], preferred_element_type=F32)
        qr_ref[0, :, sl] = _rope128(y, cs128, sn128).astype(BF16)
        y = jnp.dot(hb, wkr_ref[:, sl], preferred_element_type=F32)
        kr_ref[0, :, sl] = (_rope128(y, cs128, sn128) * (RET_HEAD_DIM ** -0.5)).astype(BF16)
    vr_ref[0] = jnp.dot(hb, wvr_ref[...], preferred_element_type=F32).astype(BF16)
    gr_ref[0] = jnp.dot(hb, wgr_ref[...], preferred_element_type=F32).astype(BF16)


def _inproj(x, shift, scale, g, weights, tables, ts):
    bsz, seq, _ = x.shape
    row = lambda b, i: (b, i, 0)
    per_b = lambda b, i: (b, 0, 0)
    const2 = lambda b, i: (0, 0)
    head_major = lambda b, i: (b, 0, i, 0)

    def wspec(w):
        return pl.BlockSpec(w.shape, const2)

    hm_shape = jax.ShapeDtypeStruct((bsz, ATT_HEADS, seq, 64), BF16)
    hm_spec = pl.BlockSpec((1, ATT_HEADS, ts, 64), head_major)
    wide = lambda n, dt=BF16: jax.ShapeDtypeStruct((bsz, seq, n), dt)
    wspec_out = lambda n: pl.BlockSpec((1, ts, n), row)
    return pl.pallas_call(
        _inproj_kernel,
        grid=(bsz, seq // ts),
        in_specs=[
            pl.BlockSpec((1, ts, D_MODEL), row),
            pl.BlockSpec((1, 1, D_MODEL), per_b),
            pl.BlockSpec((1, 1, D_MODEL), per_b),
            pl.BlockSpec((1, D_MODEL), const2),
            *[wspec(w) for w in weights],
            *[pl.BlockSpec((1, ts, LANES), row)] * 4,
        ],
        out_specs=[hm_spec, hm_spec,
                   pl.BlockSpec((1, ATT_HEADS, ts // DSA_TK, V_ROWS, DSA_TK), lambda b, i: (b, 0, i, 0, 0)),
                   hm_spec, wspec_out(64),
                   pl.BlockSpec((1, IDX_HEADS, ts), lambda b, i: (b, 0, i)),
                   wspec_out(RET_WIDTH), wspec_out(RET_WIDTH), wspec_out(RET_WIDTH), wspec_out(RET_WIDTH)],
        out_shape=[hm_shape, hm_shape,
                   jax.ShapeDtypeStruct((bsz, ATT_HEADS, seq // DSA_TK, V_ROWS, DSA_TK), BF16),
                   hm_shape, wide(64),
                   jax.ShapeDtypeStruct((bsz, IDX_HEADS, seq), F32),
                   wide(RET_WIDTH), wide(RET_WIDTH), wide(RET_WIDTH), wide(RET_WIDTH)],
        compiler_params=_params("parallel", "parallel"),
        name="inproj",
    )(x, shift, scale, g, *weights, *tables)


def _key_to_float(u):
    key = u ^ INT_MIN
    bits = jnp.where(key >= 0, key, key ^ 0x7FFFFFFF)
    return lax.bitcast_convert_type(bits, F32)


def _dsa_kernel(qi_ref, ki_ref, wit_ref, qa_ref, ka_ref, vt_ref, o_ref,
                sc_ref, m_ref, acc_ref, kmax_ref, *, topk):
    tq, tk = DSA_TQ, DSA_TK
    qb = pl.program_id(1)
    nk = (qb + 1) * (tq // tk)
    grp = (tk // SUB, SUB, tq)
    q_pos = qb * tq + lax.broadcasted_iota(jnp.int32, (SUB, tq), 1)

    def key_index(kc):
        return (kc * tk + lax.broadcasted_iota(jnp.int32, grp, 0) * SUB
                + lax.broadcasted_iota(jnp.int32, grp, 1))

    wt = wit_ref[0]

    def score_chunk(kc, carry):
        k0 = pl.multiple_of(kc * tk, tk)
        kch = ki_ref[0, pl.ds(k0, tk), :]
        acc = jnp.zeros((tk, tq), F32)
        for h in range(IDX_HEADS):
            logit = lax.dot_general(kch, qi_ref[0, h], _NT, preferred_element_type=F32)
            acc = acc + jnp.maximum(logit, 0.0) * wt[h:h + 1, :]
        sc_ref[kc] = jnp.where(key_index(kc) <= q_pos[None], acc.reshape(grp), NEG_INF)
        return carry

    lax.fori_loop(0, nk, score_chunk, 0)

    def count(pred):
        def body(kc, acc):
            return acc + jnp.sum(jnp.where(pred(sc_ref[kc], kc), 1.0, 0.0), axis=0)
        acc = lax.fori_loop(0, nk, body, jnp.zeros((SUB, tq), F32))
        return jnp.sum(acc, axis=0, keepdims=True)

    def rows(v):
        return jnp.broadcast_to(v, (SUB, tq))[None]

    def bit_body(i, u):
        trial = u | lax.shift_left(jnp.int32(1), 31 - i)
        cand = rows(_key_to_float(trial))
        cnt = count(lambda x, kc: x >= cand)
        return jnp.where(cnt >= topk, trial, u)

    u = lax.fori_loop(0, 32, bit_body, jnp.zeros((1, tq), jnp.int32))
    thr1 = jnp.where(q_pos[:1] < topk - 1, F32_LOWEST, _key_to_float(u))
    thr = rows(thr1)
    cnt_ge = count(lambda x, kc: x >= thr)
    has_excess_ties = jnp.max(cnt_ge) > topk

    def tie_cut():
        need = topk - count(lambda x, kc: x > thr)

        def cut_body(i, c):
            trial = c | lax.shift_left(jnp.int32(1), 12 - i)
            tb = rows(trial)
            f = count(lambda x, kc: (x == thr) & (key_index(kc) < tb))
            return jnp.where(f <= need, trial, c)

        return lax.fori_loop(0, 13, cut_body, jnp.zeros((1, tq), jnp.int32))

    cut = rows(lax.cond(has_excess_ties, tie_cut,
                        lambda: jnp.full((1, tq), 2 ** 13 - 1, jnp.int32)))

    def mask_body(kc, c):
        x = sc_ref[kc]
        keep = (x > thr) | ((x == thr) & (key_index(kc) < cut))
        sc_ref[kc] = jnp.where(keep, 0.0, NEG_INF)
        return c

    lax.fori_loop(0, nk, mask_body, 0)

    def logits(kc, h):
        k0 = pl.multiple_of(kc * tk, tk)
        kch = ka_ref[0, h, pl.ds(k0, tk), :]
        s = lax.dot_general(kch, qa_ref[0, h], _NT, preferred_element_type=F32)
        return s.reshape(grp) + sc_ref[kc]

    def max_chunk(kc, m):
        return jnp.stack([jnp.maximum(m[h], jnp.max(logits(kc, h), axis=0))
                          for h in range(ATT_HEADS)])

    def pv_chunk(kc, carry):
        ss = [logits(kc, h) for h in range(ATT_HEADS)]
        for h in range(ATT_HEADS):
            p = jnp.exp(ss[h] - m_ref[h][None]).reshape(tk, tq).astype(BF16)
            acc_ref[h] += jnp.dot(vt_ref[0, h, kc], p, preferred_element_type=F32)
        return carry

    def exp_pv_pass():
        acc_ref[...] = jnp.zeros(acc_ref.shape, F32)
        lax.fori_loop(0, nk, pv_chunk, 0)

    ones_r = jnp.ones((SUB, ATT_HEAD_DIM), BF16)

    @pl.when(qb == 0)
    def _():
        for h in range(ATT_HEADS):
            def norm_chunk(c, m):
                k = ka_ref[0, h, pl.ds(pl.multiple_of(c * tk, tk), tk), :]
                return jnp.maximum(m, lax.dot_general(ones_r, k * k, _NT, preferred_element_type=F32))
            m = lax.fori_loop(0, ka_ref.shape[2] // tk, norm_chunk, jnp.zeros((SUB, tk), F32))
            kmax_ref[h] = jnp.broadcast_to(jnp.max(m, axis=1, keepdims=True), (SUB, tq))

    for h in range(ATT_HEADS):
        q = qa_ref[0, h]
        q_norm2 = lax.dot_general(ones_r, q * q, _NT, preferred_element_type=F32)
        m_ref[h] = jnp.sqrt(q_norm2 * kmax_ref[h])
    exp_pv_pass()
    denom_min = jnp.min(jnp.stack([acc_ref[h][64:65] for h in range(ATT_HEADS)]))

    @pl.when(jnp.logical_not(denom_min > DENOM_FLOOR))
    def _():
        m8 = lax.fori_loop(0, nk, max_chunk, jnp.full((ATT_HEADS, SUB, tq), NEG_INF, F32))
        m_ref[...] = jnp.broadcast_to(jnp.max(m8, axis=1, keepdims=True), m8.shape)
        exp_pv_pass()

    outs = []
    for h in range(ATT_HEADS):
        acc = acc_ref[h]
        outs.append(acc[:64] / acc[64:65])
    o_ref[0] = jnp.concatenate(outs, axis=0).T.astype(BF16)


def _dsa(qi, ki, wit, qa, ka, vt):
    bsz, _, seq, _ = qi.shape
    tq, tk = DSA_TQ, DSA_TK
    topk = min(TOPK_MAX, seq // 4)
    assert seq % tq == 0 and tq % tk == 0 and tk >= topk
    blk_q = lambda b, i: (b, 0, i, 0)
    resident = dict(pipeline_mode=pl.Buffered(1))
    return pl.pallas_call(
        functools.partial(_dsa_kernel, topk=topk),
        grid=(bsz, seq // tq),
        in_specs=[
            pl.BlockSpec((1, IDX_HEADS, tq, 64), blk_q),
            pl.BlockSpec((1, seq, 64), lambda b, i: (b, 0, 0), **resident),
            pl.BlockSpec((1, IDX_HEADS, tq), lambda b, i: (b, 0, i)),
            pl.BlockSpec((1, ATT_HEADS, tq, 64), blk_q),
            pl.BlockSpec((1, ATT_HEADS, seq, 64), lambda b, i: (b, 0, 0, 0), **resident),
            pl.BlockSpec((1, ATT_HEADS, seq // tk, V_ROWS, tk), lambda b, i: (b, 0, 0, 0, 0), **resident),
        ],
        out_specs=pl.BlockSpec((1, tq, ATT_WIDTH), lambda b, i: (b, i, 0)),
        out_shape=jax.ShapeDtypeStruct((bsz, seq, ATT_WIDTH), BF16),
        scratch_shapes=[
            pltpu.VMEM((seq // tk, tk // SUB, SUB, tq), F32),
            pltpu.VMEM((ATT_HEADS, SUB, tq), F32),
            pltpu.VMEM((ATT_HEADS, V_ROWS, tq), F32),
            pltpu.VMEM((ATT_HEADS, SUB, tq), F32),
        ],
        compiler_params=_params("parallel", "arbitrary"),
        name="dsa",
    )(qi, ki, wit, qa, ka, vt)


RET_CHUNK = 256


def _ret_kernel(q_ref, k_ref, v_ref, g_ref, o_ref, state_ref, decay_ref):
    c = RET_CHUNK
    n = pl.program_id(1)
    row = lax.broadcasted_iota(jnp.int32, (c, c), 0)
    col = lax.broadcasted_iota(jnp.int32, (c, c), 1)
    idx = lax.broadcasted_iota(jnp.int32, (c, 1), 0).astype(F32)

    @pl.when(n == 0)
    def _():
        state_ref[...] = jnp.zeros_like(state_ref)
        for h in range(RET_HEADS):
            lg = math.log(1.0 - 2.0 ** (-5.0 - h))
            diff = (row - col).astype(F32)
            decay_ref[h] = jnp.where(row >= col, jnp.exp(lg * jnp.maximum(diff, 0.0)), 0.0)

    for h in range(RET_HEADS):
        lg = math.log(1.0 - 2.0 ** (-5.0 - h))
        sl = slice(h * LANES, (h + 1) * LANES)
        q = q_ref[0, :, sl]
        k = k_ref[0, :, sl]
        v = v_ref[0, :, sl]
        inner = lax.dot_general(q, k, _NT, preferred_element_type=F32) * decay_ref[h]
        o = jnp.dot(inner.astype(BF16), v, preferred_element_type=F32)
        qd = (q.astype(F32) * jnp.exp(lg * (idx + 1.0))).astype(BF16)
        state = state_ref[h]
        o = o + jnp.dot(qd, state.astype(BF16), preferred_element_type=F32)
        kd = (k.astype(F32) * jnp.exp(lg * (c - 1.0 - idx))).astype(BF16)
        kv = lax.dot_general(kd, v, _TN, preferred_element_type=F32)
        state_ref[h] = state * math.exp(lg * c) + kv
        gate = _silu(g_ref[0, :, sl].astype(F32))
        o_ref[0, :, sl] = (gate * _rms(o)).astype(BF16)


def _retention(qr, kr, vr, gr):
    bsz, seq, _ = qr.shape
    c = RET_CHUNK
    spec = pl.BlockSpec((1, c, RET_WIDTH), lambda b, i: (b, i, 0))
    return pl.pallas_call(
        _ret_kernel,
        grid=(bsz, seq // c),
        in_specs=[spec] * 4,
        out_specs=spec,
        out_shape=jax.ShapeDtypeStruct((bsz, seq, RET_WIDTH), BF16),
        scratch_shapes=[
            pltpu.VMEM((RET_HEADS, RET_HEAD_DIM, RET_HEAD_DIM), F32),
            pltpu.VMEM((RET_HEADS, c, c), F32),
        ],
        compiler_params=_params("arbitrary", "arbitrary"),
        name="ret",
    )(qr, kr, vr, gr)


ROUTER_E0 = N_GROUPS


def _mid_kernel(att_ref, ret_ref, x_ref, woa_ref, wor_ref, gpost_ref, gate1_ref,
                gpre_ref, scale2_ref, shift2_ref, wr_ref, br_ref,
                h1_ref, h2_ref, comb_ref):
    mix = jnp.dot(att_ref[0], woa_ref[...], preferred_element_type=F32)
    mix = mix + jnp.dot(ret_ref[0], wor_ref[...], preferred_element_type=F32)
    h1 = x_ref[0] + gate1_ref[0] * (_rms(mix) * gpost_ref[...])
    h1_ref[0] = h1
    h2 = (_rms(h1) * gpre_ref[...] * (1.0 + scale2_ref[0]) + shift2_ref[0]).astype(BF16)
    h2_ref[0] = h2

    logits = jnp.dot(h2, wr_ref[...], preferred_element_type=F32) + br_ref[...]
    lane = lax.broadcasted_iota(jnp.int32, logits.shape, 1)
    big = jnp.int32(LANES)

    def first_argmax(vals, vmax):
        return jnp.min(jnp.where(vals == vmax, lane, big), axis=1, keepdims=True)

    gl = jnp.where(lane < N_GROUPS, logits, NEG_INF)
    gexp = jnp.exp(gl - jnp.max(gl, axis=1, keepdims=True))
    gprob = gexp / jnp.sum(gexp, axis=1, keepdims=True)
    g_top = jnp.max(gprob, axis=1, keepdims=True)
    g_sel = first_argmax(gprob, g_top)
    e_lo = ROUTER_E0 + g_sel * EXPERTS_PER_GROUP
    in_group = (lane >= e_lo) & (lane < e_lo + EXPERTS_PER_GROUP)
    el = jnp.where(in_group, logits, NEG_INF)
    eexp = jnp.exp(el - jnp.max(el, axis=1, keepdims=True))
    eprob = jnp.where(in_group, eexp / jnp.sum(eexp, axis=1, keepdims=True), -1.0)
    top1 = jnp.max(eprob, axis=1, keepdims=True)
    idx1 = first_argmax(eprob, top1)
    rest = jnp.where(lane == idx1, -1.0, eprob)
    top2 = jnp.max(rest, axis=1, keepdims=True)
    idx2 = first_argmax(rest, top2)
    norm = g_top / (top1 + top2)
    comb_ref[0] = jnp.where(lane == idx1, top1 * norm, 0.0) + jnp.where(lane == idx2, top2 * norm, 0.0)


def _mid(att, ret, x, woa, wor, gpost, gate1, gpre, scale2, shift2, wr, br, ts):
    bsz, seq, _ = x.shape
    row = lambda b, i: (b, i, 0)
    per_b = lambda b, i: (b, 0, 0)
    const2 = lambda b, i: (0, 0)
    vec = pl.BlockSpec((1, D_MODEL), const2)
    bvec = pl.BlockSpec((1, 1, D_MODEL), per_b)
    return pl.pallas_call(
        _mid_kernel,
        grid=(bsz, seq // ts),
        in_specs=[
            pl.BlockSpec((1, ts, ATT_WIDTH), row),
            pl.BlockSpec((1, ts, RET_WIDTH), row),
            pl.BlockSpec((1, ts, D_MODEL), row),
            pl.BlockSpec(woa.shape, const2),
            pl.BlockSpec(wor.shape, const2),
            vec, bvec, vec, bvec, bvec,
            pl.BlockSpec(wr.shape, const2),
            pl.BlockSpec((1, LANES), const2),
        ],
        out_specs=[pl.BlockSpec((1, ts, D_MODEL), row),
                   pl.BlockSpec((1, ts, D_MODEL), row),
                   pl.BlockSpec((1, ts, LANES), row)],
        out_shape=[jax.ShapeDtypeStruct((bsz, seq, D_MODEL), F32),
                   jax.ShapeDtypeStruct((bsz, seq, D_MODEL), BF16),
                   jax.ShapeDtypeStruct((bsz, seq, LANES), F32)],
        compiler_params=_params("parallel", "parallel"),
        name="mid",
    )(att, ret, x, woa, wor, gpost, gate1, gpre, scale2, shift2, wr, br)


def _moe_kernel(h2_ref, comb_ref, h1_ref, gate2_ref, gpost_ref, wgu_ref, wd_ref, o_ref, acc_ref):
    e = pl.program_id(2)

    @pl.when(e == 0)
    def _():
        acc_ref[...] = jnp.zeros_like(acc_ref)

    t = h2_ref[0]
    gu = jnp.dot(t, wgu_ref[0], preferred_element_type=F32)
    hid = _silu(gu[:, :D_EXPERT]) * gu[:, D_EXPERT:]
    comb = comb_ref[0]
    lane = lax.broadcasted_iota(jnp.int32, comb.shape, 1)
    cw = jnp.sum(jnp.where(lane == ROUTER_E0 + e, comb, 0.0), axis=1, keepdims=True)
    acc_ref[...] += jnp.dot((hid * cw).astype(BF16), wd_ref[0], preferred_element_type=F32)

    @pl.when(e == N_EXPERTS - 1)
    def _():
        o_ref[0] = h1_ref[0] + gate2_ref[0] * (_rms(acc_ref[...]) * gpost_ref[...])


def _moe(h2, comb, h1, gate2, gpost, wgu, wd, tm):
    bsz, seq, _ = h2.shape
    row = lambda b, i, e: (b, i, 0)
    return pl.pallas_call(
        _moe_kernel,
        grid=(bsz, seq // tm, N_EXPERTS),
        in_specs=[
            pl.BlockSpec((1, tm, D_MODEL), row),
            pl.BlockSpec((1, tm, LANES), row),
            pl.BlockSpec((1, tm, D_MODEL), row),
            pl.BlockSpec((1, 1, D_MODEL), lambda b, i, e: (b, 0, 0)),
            pl.BlockSpec((1, D_MODEL), lambda b, i, e: (0, 0)),
            pl.BlockSpec((1, D_MODEL, 2 * D_EXPERT), lambda b, i, e: (e, 0, 0)),
            pl.BlockSpec((1, D_EXPERT, D_MODEL), lambda b, i, e: (e, 0, 0)),
        ],
        out_specs=pl.BlockSpec((1, tm, D_MODEL), row),
        out_shape=jax.ShapeDtypeStruct((bsz, seq, D_MODEL), F32),
        scratch_shapes=[pltpu.VMEM((tm, D_MODEL), F32)],
        compiler_params=_params("parallel", "parallel", "arbitrary"),
        name="moe",
    )(h2, comb, h1, gate2, gpost, wgu, wd)


def _layer(x, positions, mod, g_pre_mix, w_in, w_out, g_post_mix, g_pre_ffn,
           w_group, b_group, w_expert, b_expert, w_gate_exp, w_up_exp, w_down_exp, g_post_ffn):
    bsz, seq, _ = x.shape
    shift1, scale1, gate1, shift2, scale2, gate2 = [
        m.reshape(bsz, 1, D_MODEL) for m in jnp.split(mod, N_MOD, axis=-1)]
    vec = lambda g: g.reshape(1, D_MODEL)

    a, i8, r = ATT_WIDTH, IDX_HEADS, RET_WIDTH
    offs = np.cumsum([0, a, a, a, a, IDX_HEAD_DIM, i8, r, r, r, r])
    seg = lambda j: w_in[:, offs[j]:offs[j + 1]].astype(BF16)
    wki = jnp.pad(w_in[:, offs[4]:offs[6]], ((0, 0), (0, LANES - IDX_HEAD_DIM - i8)))
    weights = [seg(0), seg(1), seg(2), seg(3), wki.astype(BF16),
               seg(6), seg(7), seg(8), seg(9)]

    tables = _rope_tables(positions, 512)
    qa, ka, vt, qi, ki, wit, qr, kr, vr, gr = _inproj(
        x, shift1, scale1, vec(g_pre_mix), weights, tables, 512)
    att = _dsa(qi, ki, wit, qa, ka, vt)
    ret = _retention(qr, kr, vr, gr)

    w_router = jnp.pad(jnp.concatenate([w_group, w_expert], axis=1),
                       ((0, 0), (0, LANES - N_GROUPS - N_EXPERTS))).astype(BF16)
    b_router = jnp.pad(jnp.concatenate([b_group, b_expert]),
                       (0, LANES - N_GROUPS - N_EXPERTS)).reshape(1, LANES)
    h1, h2, comb = _mid(att, ret, x, w_out[:ATT_WIDTH].astype(BF16), w_out[ATT_WIDTH:].astype(BF16),
                        vec(g_post_mix), gate1, vec(g_pre_ffn), scale2, shift2, w_router, b_router, 512)

    wgu = jnp.concatenate([w_gate_exp, w_up_exp], axis=2).astype(BF16)
    return _moe(h2, comb, h1, gate2, vec(g_post_ffn), wgu, w_down_exp.astype(BF16), 1024)


def kernel(x, c, positions, g_pre_mix, w_ada, b_ada, w_in, w_out, g_post_mix, g_pre_ffn,
           w_group, b_group, w_expert, b_expert, w_gate_exp, w_up_exp, w_down_exp, g_post_ffn):
    h = x
    for l in range(w_in.shape[0]):
        mod = _ada(c, w_ada[l], b_ada[l])
        h = _layer(h, positions, mod, g_pre_mix[l], w_in[l], w_out[l], g_post_mix[l], g_pre_ffn[l],
                   w_group[l], b_group[l], w_expert[l], b_expert[l],
                   w_gate_exp[l], w_up_exp[l], w_down_exp[l], g_post_ffn[l])
    return h
```

```python
import functools
import math

import numpy as np
import jax
import jax.numpy as jnp
from jax import lax
from jax.experimental import pallas as pl
from jax.experimental.pallas import tpu as pltpu

D_MODEL = 1024
ATT_HEADS = 8
ATT_HEAD_DIM = 64
IDX_HEADS = 8
IDX_HEAD_DIM = 64
TOPK_MAX = 256
RET_HEADS = 4
RET_HEAD_DIM = 128
ROPE_THETA = 10000.0
ATT_WIDTH = ATT_HEADS * ATT_HEAD_DIM
RET_WIDTH = RET_HEADS * RET_HEAD_DIM
N_GROUPS = 4
EXPERTS_PER_GROUP = 8
N_EXPERTS = N_GROUPS * EXPERTS_PER_GROUP
D_EXPERT = 256
N_MOD = 6
EPS = 1e-6

LANES = 128
SUB = 8
VMEM_LIMIT = 56 * 1024 * 1024

DSA_TQ = 256
DSA_TK = 256
V_ROWS = 80

F32 = jnp.float32
BF16 = jnp.bfloat16
NEG_INF = float("-inf")
F32_LOWEST = float(np.finfo(np.float32).min)
INT_MIN = -2 ** 31
DENOM_FLOOR = 1e-30

_NT = (((1,), (1,)), ((), ()))
_TN = (((0,), (0,)), ((), ()))


def _params(*sem):
    return pltpu.CompilerParams(dimension_semantics=sem, vmem_limit_bytes=VMEM_LIMIT)


def _rms(x):
    return x * lax.rsqrt(jnp.mean(x * x, axis=-1, keepdims=True) + EPS)


def _silu(x):
    return x * (1.0 / (1.0 + jnp.exp(-x)))


def _ada_kernel(c_ref, w_ref, b_ref, o_ref):
    a = _silu(c_ref[...]).astype(BF16)
    o_ref[...] = jnp.dot(a, w_ref[...].astype(BF16), preferred_element_type=F32) + b_ref[...]


def _ada(c, w, b):
    bsz = c.shape[0]
    n = w.shape[1]
    tn = D_MODEL
    return pl.pallas_call(
        _ada_kernel,
        grid=(n // tn,),
        in_specs=[
            pl.BlockSpec((bsz, D_MODEL), lambda j: (0, 0)),
            pl.BlockSpec((D_MODEL, tn), lambda j: (0, j)),
            pl.BlockSpec((1, tn), lambda j: (0, j)),
        ],
        out_specs=pl.BlockSpec((bsz, tn), lambda j: (0, j)),
        out_shape=jax.ShapeDtypeStruct((bsz, n), F32),
        compiler_params=_params("arbitrary"),
        name="ada",
    )(c, w, b.reshape(1, n))


def _tables_kernel(pos_ref, f_ref, cs64_ref, sn64_ref, cs128_ref, sn128_ref):
    ang = pos_ref[0].astype(F32) * f_ref[...]
    c = jnp.cos(ang)
    s = jnp.sin(ang)
    c_sw = pltpu.roll(c, 64, 1)
    s_sw = pltpu.roll(s, 64, 1)
    lane = lax.broadcasted_iota(jnp.int32, c.shape, 1)
    lo = lane < 64
    cs128_ref[0] = jnp.where(lo, c, c_sw)
    sn128_ref[0] = jnp.where(lo, -s, s_sw)
    cs64_ref[0] = jnp.where(lo, c_sw, c)
    s64 = jnp.where(lo, s_sw, s)
    sn64_ref[0] = jnp.where((lane % 64) < 32, -s64, s64)


def _rope_tables(positions, ts):
    bsz, seq = positions.shape
    f64 = ROPE_THETA ** (-jnp.arange(32, dtype=F32) / 32)
    f128 = ROPE_THETA ** (-jnp.arange(64, dtype=F32) / 64)
    frow = jnp.concatenate([f128, f64, f64]).reshape(1, LANES)
    tab = jax.ShapeDtypeStruct((bsz, seq, LANES), F32)
    tspec = pl.BlockSpec((1, ts, LANES), lambda b, i: (b, i, 0))
    return pl.pallas_call(
        _tables_kernel,
        grid=(bsz, seq // ts),
        in_specs=[
            pl.BlockSpec((1, ts, 1), lambda b, i: (b, i, 0)),
            pl.BlockSpec((1, LANES), lambda b, i: (0, 0)),
        ],
        out_specs=[tspec] * 4,
        out_shape=[tab] * 4,
        compiler_params=_params("parallel", "parallel"),
        name="tables",
    )(positions.reshape(bsz, seq, 1), frow)


def _rope64(y, cs, sn):
    lane = lax.broadcasted_iota(jnp.int32, y.shape, 1)
    rot = jnp.where((lane % 64) < 32, pltpu.roll(y, 96, 1), pltpu.roll(y, 32, 1))
    return y * cs + rot * sn


def _rope128(y, cs, sn):
    return y * cs + pltpu.roll(y, 64, 1) * sn


def _inproj_kernel(x_ref, shift_ref, scale_ref, g_ref,
                   wqa_ref, wka_ref, wv_ref, wqi_ref, wki_ref,
                   wqr_ref, wkr_ref, wvr_ref, wgr_ref,
                   cs64_ref, sn64_ref, cs128_ref, sn128_ref,
                   qa_ref, ka_ref, vt_ref, qi_ref, ki_ref, wit_ref,
                   qr_ref, kr_ref, vr_ref, gr_ref):
    x = x_ref[0]
    h = _rms(x) * g_ref[...] * (1.0 + scale_ref[0]) + shift_ref[0]
    hb = h.astype(BF16)
    cs64, sn64 = cs64_ref[0], sn64_ref[0]
    cs128, sn128 = cs128_ref[0], sn128_ref[0]

    def heads64(w_ref, o_ref, mult):
        for c in range(ATT_WIDTH // LANES):
            y = jnp.dot(hb, w_ref[:, c * LANES:(c + 1) * LANES], preferred_element_type=F32)
            y = (_rope64(y, cs64, sn64) * mult).astype(BF16)
            o_ref[0, 2 * c] = y[:, :64]
            o_ref[0, 2 * c + 1] = y[:, 64:]

    heads64(wqa_ref, qa_ref, ATT_HEAD_DIM ** -0.5)
    heads64(wka_ref, ka_ref, 1.0)
    heads64(wqi_ref, qi_ref, 1.0)

    ts = hb.shape[0]
    vt = jnp.dot(hb, wv_ref[...], preferred_element_type=F32).T
    for h in range(ATT_HEADS):
        for j in range(ts // DSA_TK):
            blk = vt[h * 64:(h + 1) * 64, j * DSA_TK:(j + 1) * DSA_TK]
            vt_ref[0, h, j, :64, :] = blk.astype(BF16)
            vt_ref[0, h, j, 64:, :] = jnp.ones((V_ROWS - 64, DSA_TK), BF16)

    yk = jnp.dot(hb, wki_ref[...], preferred_element_type=F32)
    ki_ref[0] = _rope64(yk, cs64, sn64)[:, :64].astype(BF16)
    wit_ref[0] = yk.T[64:64 + IDX_HEADS] * ((IDX_HEADS ** -0.5) * (IDX_HEAD_DIM ** -0.5))

    for c in range(RET_HEADS):
        sl = slice(c * LANES, (c + 1) * LANES)
        y = jnp.dot(hb, wqr_ref[:, sl], preferred_element_type=F32)
        qr_ref[0, :, sl] = _rope128(y, cs128, sn128).astype(BF16)
        y = jnp.dot(hb, wkr_ref[:, sl], preferred_element_type=F32)
        kr_ref[0, :, sl] = (_rope128(y, cs128, sn128) * (RET_HEAD_DIM ** -0.5)).astype(BF16)
    vr_ref[0] = jnp.dot(hb, wvr_ref[...], preferred_element_type=F32).astype(BF16)
    gr_ref[0] = jnp.dot(hb, wgr_ref[...], preferred_element_type=F32).astype(BF16)


def _inproj(x, shift, scale, g, weights, tables, ts):
    bsz, seq, _ = x.shape
    row = lambda b, i: (b, i, 0)
    per_b = lambda b, i: (b, 0, 0)
    const2 = lambda b, i: (0, 0)
    head_major = lambda b, i: (b, 0, i, 0)

    def wspec(w):
        return pl.BlockSpec(w.shape, const2)

    hm_shape = jax.ShapeDtypeStruct((bsz, ATT_HEADS, seq, 64), BF16)
    hm_spec = pl.BlockSpec((1, ATT_HEADS, ts, 64), head_major)
    wide = lambda n, dt=BF16: jax.ShapeDtypeStruct((bsz, seq, n), dt)
    wspec_out = lambda n: pl.BlockSpec((1, ts, n), row)
    return pl.pallas_call(
        _inproj_kernel,
        grid=(bsz, seq // ts),
        in_specs=[
            pl.BlockSpec((1, ts, D_MODEL), row),
            pl.BlockSpec((1, 1, D_MODEL), per_b),
            pl.BlockSpec((1, 1, D_MODEL), per_b),
            pl.BlockSpec((1, D_MODEL), const2),
            *[wspec(w) for w in weights],
            *[pl.BlockSpec((1, ts, LANES), row)] * 4,
        ],
        out_specs=[hm_spec, hm_spec,
                   pl.BlockSpec((1, ATT_HEADS, ts // DSA_TK, V_ROWS, DSA_TK), lambda b, i: (b, 0, i, 0, 0)),
                   hm_spec, wspec_out(64),
                   pl.BlockSpec((1, IDX_HEADS, ts), lambda b, i: (b, 0, i)),
                   wspec_out(RET_WIDTH), wspec_out(RET_WIDTH), wspec_out(RET_WIDTH), wspec_out(RET_WIDTH)],
        out_shape=[hm_shape, hm_shape,
                   jax.ShapeDtypeStruct((bsz, ATT_HEADS, seq // DSA_TK, V_ROWS, DSA_TK), BF16),
                   hm_shape, wide(64),
                   jax.ShapeDtypeStruct((bsz, IDX_HEADS, seq), F32),
                   wide(RET_WIDTH), wide(RET_WIDTH), wide(RET_WIDTH), wide(RET_WIDTH)],
        compiler_params=_params("parallel", "parallel"),
        name="inproj",
    )(x, shift, scale, g, *weights, *tables)


def _key_to_float(u):
    key = u ^ INT_MIN
    bits = jnp.where(key >= 0, key, key ^ 0x7FFFFFFF)
    return lax.bitcast_convert_type(bits, F32)


def _dsa_kernel(qi_ref, ki_ref, wit_ref, qa_ref, ka_ref, vt_ref, o_ref,
                sc_ref, m_ref, acc_ref, kmax_ref, *, topk):
    tq, tk = DSA_TQ, DSA_TK
    qb = pl.program_id(1)
    nk = (qb + 1) * (tq // tk)
    grp = (tk // SUB, SUB, tq)
    q_pos = qb * tq + lax.broadcasted_iota(jnp.int32, (SUB, tq), 1)

    def key_index(kc):
        return (kc * tk + lax.broadcasted_iota(jnp.int32, grp, 0) * SUB
                + lax.broadcasted_iota(jnp.int32, grp, 1))

    wt = wit_ref[0]

    def score_chunk(kc, carry):
        k0 = pl.multiple_of(kc * tk, tk)
        kch = ki_ref[0, pl.ds(k0, tk), :]
        acc = jnp.zeros((tk, tq), F32)
        for h in range(IDX_HEADS):
            logit = lax.dot_general(kch, qi_ref[0, h], _NT, preferred_element_type=F32)
            acc = acc + jnp.maximum(logit, 0.0) * wt[h:h + 1, :]
        sc_ref[kc] = jnp.where(key_index(kc) <= q_pos[None], acc.reshape(grp), NEG_INF)
        return carry

    lax.fori_loop(0, nk, score_chunk, 0)

    def count(pred):
        def body(kc, acc):
            hit = jnp.where(pred(sc_ref[kc], kc), 1.0, 0.0)
            parts = [hit[g] for g in range(grp[0])]
            while len(parts) > 1:
                parts = [parts[i] + parts[i + 1] for i in range(0, len(parts), 2)]
            return acc + parts[0]
        acc = lax.fori_loop(0, nk, body, jnp.zeros((SUB, tq), F32))
        return jnp.sum(acc, axis=0, keepdims=True)

    def rows(v):
        return jnp.broadcast_to(v, (SUB, tq))[None]

    def bit_body(i, u):
        trial = u | lax.shift_left(jnp.int32(1), 31 - i)
        cand = rows(_key_to_float(trial))
        cnt = count(lambda x, kc: x >= cand)
        return jnp.where(cnt >= topk, trial, u)

    u = lax.fori_loop(0, 32, bit_body, jnp.zeros((1, tq), jnp.int32))
    thr1 = jnp.where(q_pos[:1] < topk - 1, F32_LOWEST, _key_to_float(u))
    thr = rows(thr1)
    cnt_ge = count(lambda x, kc: x >= thr)
    has_excess_ties = jnp.max(cnt_ge) > topk

    def tie_cut():
        need = topk - count(lambda x, kc: x > thr)

        def cut_body(i, c):
            trial = c | lax.shift_left(jnp.int32(1), 12 - i)
            tb = rows(trial)
            f = count(lambda x, kc: (x == thr) & (key_index(kc) < tb))
            return jnp.where(f <= need, trial, c)

        return lax.fori_loop(0, 13, cut_body, jnp.zeros((1, tq), jnp.int32))

    cut = rows(lax.cond(has_excess_ties, tie_cut,
                        lambda: jnp.full((1, tq), 2 ** 13 - 1, jnp.int32)))

    def mask_body(kc, c):
        x = sc_ref[kc]
        keep = (x > thr) | ((x == thr) & (key_index(kc) < cut))
        sc_ref[kc] = jnp.where(keep, 0.0, NEG_INF)
        return c

    lax.fori_loop(0, nk, mask_body, 0)

    def logits(kc, h):
        k0 = pl.multiple_of(kc * tk, tk)
        kch = ka_ref[0, h, pl.ds(k0, tk), :]
        s = lax.dot_general(kch, qa_ref[0, h], _NT, preferred_element_type=F32)
        return s.reshape(grp) + sc_ref[kc]

    def max_chunk(kc, m):
        return jnp.stack([jnp.maximum(m[h], jnp.max(logits(kc, h), axis=0))
                          for h in range(ATT_HEADS)])

    def pv_chunk(kc, carry):
        ss = [logits(kc, h) for h in range(ATT_HEADS)]
        for h in range(ATT_HEADS):
            p = jnp.exp(ss[h] - m_ref[h][None]).reshape(tk, tq).astype(BF16)
            acc_ref[h] += jnp.dot(vt_ref[0, h, kc], p, preferred_element_type=F32)
        return carry

    def exp_pv_pass():
        acc_ref[...] = jnp.zeros(acc_ref.shape, F32)
        lax.fori_loop(0, nk, pv_chunk, 0)

    ones_r = jnp.ones((SUB, ATT_HEAD_DIM), BF16)

    @pl.when(qb == 0)
    def _():
        for h in range(ATT_HEADS):
            k = ka_ref[0, h]
            k_norm2 = lax.dot_general(ones_r, k * k, _NT, preferred_element_type=F32)
            kmax_ref[h] = jnp.broadcast_to(jnp.max(k_norm2, axis=1, keepdims=True), (SUB, tq))

    for h in range(ATT_HEADS):
        q = qa_ref[0, h]
        q_norm2 = lax.dot_general(ones_r, q * q, _NT, preferred_element_type=F32)
        m_ref[h] = jnp.sqrt(q_norm2 * kmax_ref[h])
    exp_pv_pass()
    denom_min = jnp.min(jnp.stack([acc_ref[h][64:65] for h in range(ATT_HEADS)]))

    @pl.when(jnp.logical_not(denom_min > DENOM_FLOOR))
    def _():
        m8 = lax.fori_loop(0, nk, max_chunk, jnp.full((ATT_HEADS, SUB, tq), NEG_INF, F32))
        m_ref[...] = jnp.broadcast_to(jnp.max(m8, axis=1, keepdims=True), m8.shape)
        exp_pv_pass()

    outs = []
    for h in range(ATT_HEADS):
        acc = acc_ref[h]
        outs.append(acc[:64] / acc[64:65])
    o_ref[0] = jnp.concatenate(outs, axis=0).T.astype(BF16)


def _dsa(qi, ki, wit, qa, ka, vt):
    bsz, _, seq, _ = qi.shape
    tq, tk = DSA_TQ, DSA_TK
    topk = min(TOPK_MAX, seq // 4)
    assert seq % tq == 0 and tq % tk == 0 and tk >= topk
    blk_q = lambda b, i: (b, 0, i, 0)
    resident = dict(pipeline_mode=pl.Buffered(1))
    return pl.pallas_call(
        functools.partial(_dsa_kernel, topk=topk),
        grid=(bsz, seq // tq),
        in_specs=[
            pl.BlockSpec((1, IDX_HEADS, tq, 64), blk_q),
            pl.BlockSpec((1, seq, 64), lambda b, i: (b, 0, 0), **resident),
            pl.BlockSpec((1, IDX_HEADS, tq), lambda b, i: (b, 0, i)),
            pl.BlockSpec((1, ATT_HEADS, tq, 64), blk_q),
            pl.BlockSpec((1, ATT_HEADS, seq, 64), lambda b, i: (b, 0, 0, 0), **resident),
            pl.BlockSpec((1, ATT_HEADS, seq // tk, V_ROWS, tk), lambda b, i: (b, 0, 0, 0, 0), **resident),
        ],
        out_specs=pl.BlockSpec((1, tq, ATT_WIDTH), lambda b, i: (b, i, 0)),
        out_shape=jax.ShapeDtypeStruct((bsz, seq, ATT_WIDTH), BF16),
        scratch_shapes=[
            pltpu.VMEM((seq // tk, tk // SUB, SUB, tq), F32),
            pltpu.VMEM((ATT_HEADS, SUB, tq), F32),
            pltpu.VMEM((ATT_HEADS, V_ROWS, tq), F32),
            pltpu.VMEM((ATT_HEADS, SUB, tq), F32),
        ],
        compiler_params=_params("parallel", "arbitrary"),
        name="dsa",
    )(qi, ki, wit, qa, ka, vt)


RET_CHUNK = 256


def _ret_kernel(q_ref, k_ref, v_ref, g_ref, o_ref, state_ref, decay_ref):
    c = RET_CHUNK
    n = pl.program_id(1)
    row = lax.broadcasted_iota(jnp.int32, (c, c), 0)
    col = lax.broadcasted_iota(jnp.int32, (c, c), 1)
    idx = lax.broadcasted_iota(jnp.int32, (c, 1), 0).astype(F32)

    @pl.when(n == 0)
    def _():
        state_ref[...] = jnp.zeros_like(state_ref)
        for h in range(RET_HEADS):
            lg = math.log(1.0 - 2.0 ** (-5.0 - h))
            diff = (row - col).astype(F32)
            decay_ref[h] = jnp.where(row >= col, jnp.exp(lg * jnp.maximum(diff, 0.0)), 0.0)

    for h in range(RET_HEADS):
        lg = math.log(1.0 - 2.0 ** (-5.0 - h))
        sl = slice(h * LANES, (h + 1) * LANES)
        q = q_ref[0, :, sl]
        k = k_ref[0, :, sl]
        v = v_ref[0, :, sl]
        inner = lax.dot_general(q, k, _NT, preferred_element_type=F32) * decay_ref[h]
        o = jnp.dot(inner.astype(BF16), v, preferred_element_type=F32)
        qd = (q.astype(F32) * jnp.exp(lg * (idx + 1.0))).astype(BF16)
        state = state_ref[h]
        o = o + jnp.dot(qd, state.astype(BF16), preferred_element_type=F32)
        kd = (k.astype(F32) * jnp.exp(lg * (c - 1.0 - idx))).astype(BF16)
        kv = lax.dot_general(kd, v, _TN, preferred_element_type=F32)
        state_ref[h] = state * math.exp(lg * c) + kv
        gate = _silu(g_ref[0, :, sl].astype(F32))
        o_ref[0, :, sl] = (gate * _rms(o)).astype(BF16)


def _retention(qr, kr, vr, gr):
    bsz, seq, _ = qr.shape
    c = RET_CHUNK
    spec = pl.BlockSpec((1, c, RET_WIDTH), lambda b, i: (b, i, 0))
    return pl.pallas_call(
        _ret_kernel,
        grid=(bsz, seq // c),
        in_specs=[spec] * 4,
        out_specs=spec,
        out_shape=jax.ShapeDtypeStruct((bsz, seq, RET_WIDTH), BF16),
        scratch_shapes=[
            pltpu.VMEM((RET_HEADS, RET_HEAD_DIM, RET_HEAD_DIM), F32),
            pltpu.VMEM((RET_HEADS, c, c), F32),
        ],
        compiler_params=_params("arbitrary", "arbitrary"),
        name="ret",
    )(qr, kr, vr, gr)


ROUTER_E0 = N_GROUPS


def _mid_kernel(att_ref, ret_ref, x_ref, woa_ref, wor_ref, gpost_ref, gate1_ref,
                gpre_ref, scale2_ref, shift2_ref, wr_ref, br_ref,
                h1_ref, h2_ref, comb_ref):
    mix = jnp.dot(att_ref[0], woa_ref[...], preferred_element_type=F32)
    mix = mix + jnp.dot(ret_ref[0], wor_ref[...], preferred_element_type=F32)
    h1 = x_ref[0] + gate1_ref[0] * (_rms(mix) * gpost_ref[...])
    h1_ref[0] = h1
    h2 = (_rms(h1) * gpre_ref[...] * (1.0 + scale2_ref[0]) + shift2_ref[0]).astype(BF16)
    h2_ref[0] = h2

    logits = jnp.dot(h2, wr_ref[...], preferred_element_type=F32) + br_ref[...]
    lane = lax.broadcasted_iota(jnp.int32, logits.shape, 1)
    big = jnp.int32(LANES)

    def first_argmax(vals, vmax):
        return jnp.min(jnp.where(vals == vmax, lane, big), axis=1, keepdims=True)

    gl = jnp.where(lane < N_GROUPS, logits, NEG_INF)
    gexp = jnp.exp(gl - jnp.max(gl, axis=1, keepdims=True))
    gprob = gexp / jnp.sum(gexp, axis=1, keepdims=True)
    g_top = jnp.max(gprob, axis=1, keepdims=True)
    g_sel = first_argmax(gprob, g_top)
    e_lo = ROUTER_E0 + g_sel * EXPERTS_PER_GROUP
    in_group = (lane >= e_lo) & (lane < e_lo + EXPERTS_PER_GROUP)
    el = jnp.where(in_group, logits, NEG_INF)
    eexp = jnp.exp(el - jnp.max(el, axis=1, keepdims=True))
    eprob = jnp.where(in_group, eexp / jnp.sum(eexp, axis=1, keepdims=True), -1.0)
    top1 = jnp.max(eprob, axis=1, keepdims=True)
    idx1 = first_argmax(eprob, top1)
    rest = jnp.where(lane == idx1, -1.0, eprob)
    top2 = jnp.max(rest, axis=1, keepdims=True)
    idx2 = first_argmax(rest, top2)
    norm = g_top / (top1 + top2)
    comb_ref[0] = jnp.where(lane == idx1, top1 * norm, 0.0) + jnp.where(lane == idx2, top2 * norm, 0.0)


def _mid(att, ret, x, woa, wor, gpost, gate1, gpre, scale2, shift2, wr, br, ts):
    bsz, seq, _ = x.shape
    row = lambda b, i: (b, i, 0)
    per_b = lambda b, i: (b, 0, 0)
    const2 = lambda b, i: (0, 0)
    vec = pl.BlockSpec((1, D_MODEL), const2)
    bvec = pl.BlockSpec((1, 1, D_MODEL), per_b)
    return pl.pallas_call(
        _mid_kernel,
        grid=(bsz, seq // ts),
        in_specs=[
            pl.BlockSpec((1, ts, ATT_WIDTH), row),
            pl.BlockSpec((1, ts, RET_WIDTH), row),
            pl.BlockSpec((1, ts, D_MODEL), row),
            pl.BlockSpec(woa.shape, const2),
            pl.BlockSpec(wor.shape, const2),
            vec, bvec, vec, bvec, bvec,
            pl.BlockSpec(wr.shape, const2),
            pl.BlockSpec((1, LANES), const2),
        ],
        out_specs=[pl.BlockSpec((1, ts, D_MODEL), row),
                   pl.BlockSpec((1, ts, D_MODEL), row),
                   pl.BlockSpec((1, ts, LANES), row)],
        out_shape=[jax.ShapeDtypeStruct((bsz, seq, D_MODEL), F32),
                   jax.ShapeDtypeStruct((bsz, seq, D_MODEL), BF16),
                   jax.ShapeDtypeStruct((bsz, seq, LANES), F32)],
        compiler_params=_params("parallel", "parallel"),
        name="mid",
    )(att, ret, x, woa, wor, gpost, gate1, gpre, scale2, shift2, wr, br)


MOE_EPS = 8


def _moe_kernel(h2_ref, comb_ref, h1_ref, gate2_ref, gpost_ref, wgu_ref, wd_ref, o_ref, acc_ref):
    step = pl.program_id(2)

    @pl.when(step == 0)
    def _():
        acc_ref[...] = jnp.zeros_like(acc_ref)

    t = h2_ref[0]
    comb = comb_ref[0]
    lane = lax.broadcasted_iota(jnp.int32, comb.shape, 1)
    hids = []
    for j in range(MOE_EPS):
        gu = jnp.dot(t, wgu_ref[j], preferred_element_type=F32)
        hid = _silu(gu[:, :D_EXPERT]) * gu[:, D_EXPERT:]
        e_lane = ROUTER_E0 + step * MOE_EPS + j
        cw = jnp.sum(jnp.where(lane == e_lane, comb, 0.0), axis=1, keepdims=True)
        hids.append((hid * cw).astype(BF16))
    wd = wd_ref[...].reshape(MOE_EPS * D_EXPERT, D_MODEL)
    acc_ref[...] += jnp.dot(jnp.concatenate(hids, axis=1), wd, preferred_element_type=F32)

    @pl.when(step == N_EXPERTS // MOE_EPS - 1)
    def _():
        o_ref[0] = h1_ref[0] + gate2_ref[0] * (_rms(acc_ref[...]) * gpost_ref[...])


def _moe(h2, comb, h1, gate2, gpost, wgu, wd, tm):
    bsz, seq, _ = h2.shape
    row = lambda b, i, e: (b, i, 0)
    return pl.pallas_call(
        _moe_kernel,
        grid=(bsz, seq // tm, N_EXPERTS // MOE_EPS),
        in_specs=[
            pl.BlockSpec((1, tm, D_MODEL), row),
            pl.BlockSpec((1, tm, LANES), row),
            pl.BlockSpec((1, tm, D_MODEL), row),
            pl.BlockSpec((1, 1, D_MODEL), lambda b, i, e: (b, 0, 0)),
            pl.BlockSpec((1, D_MODEL), lambda b, i, e: (0, 0)),
            pl.BlockSpec((MOE_EPS, D_MODEL, 2 * D_EXPERT), lambda b, i, e: (e, 0, 0)),
            pl.BlockSpec((MOE_EPS, D_EXPERT, D_MODEL), lambda b, i, e: (e, 0, 0)),
        ],
        out_specs=pl.BlockSpec((1, tm, D_MODEL), row),
        out_shape=jax.ShapeDtypeStruct((bsz, seq, D_MODEL), F32),
        scratch_shapes=[pltpu.VMEM((tm, D_MODEL), F32)],
        compiler_params=_params("parallel", "parallel", "arbitrary"),
        name="moe",
    )(h2, comb, h1, gate2, gpost, wgu, wd)


def _layer(x, positions, mod, g_pre_mix, w_in, w_out, g_post_mix, g_pre_ffn,
           w_group, b_group, w_expert, b_expert, w_gate_exp, w_up_exp, w_down_exp, g_post_ffn):
    bsz, seq, _ = x.shape
    shift1, scale1, gate1, shift2, scale2, gate2 = [
        m.reshape(bsz, 1, D_MODEL) for m in jnp.split(mod, N_MOD, axis=-1)]
    vec = lambda g: g.reshape(1, D_MODEL)

    a, i8, r = ATT_WIDTH, IDX_HEADS, RET_WIDTH
    offs = np.cumsum([0, a, a, a, a, IDX_HEAD_DIM, i8, r, r, r, r])
    seg = lambda j: w_in[:, offs[j]:offs[j + 1]].astype(BF16)
    wki = jnp.pad(w_in[:, offs[4]:offs[6]], ((0, 0), (0, LANES - IDX_HEAD_DIM - i8)))
    weights = [seg(0), seg(1), seg(2), seg(3), wki.astype(BF16),
               seg(6), seg(7), seg(8), seg(9)]

    tables = _rope_tables(positions, 512)
    qa, ka, vt, qi, ki, wit, qr, kr, vr, gr = _inproj(
        x, shift1, scale1, vec(g_pre_mix), weights, tables, 512)
    att = _dsa(qi, ki, wit, qa, ka, vt)
    ret = _retention(qr, kr, vr, gr)

    w_router = jnp.pad(jnp.concatenate([w_group, w_expert], axis=1),
                       ((0, 0), (0, LANES - N_GROUPS - N_EXPERTS))).astype(BF16)
    b_router = jnp.pad(jnp.concatenate([b_group, b_expert]),
                       (0, LANES - N_GROUPS - N_EXPERTS)).reshape(1, LANES)
    h1, h2, comb = _mid(att, ret, x, w_out[:ATT_WIDTH].astype(BF16), w_out[ATT_WIDTH:].astype(BF16),
                        vec(g_post_mix), gate1, vec(g_pre_ffn), scale2, shift2, w_router, b_router, 512)

    wgu = jnp.concatenate([w_gate_exp, w_up_exp], axis=2).astype(BF16)
    return _moe(h2, comb, h1, gate2, vec(g_post_ffn), wgu, w_down_exp.astype(BF16), 1024)


def kernel(x, c, positions, g_pre_mix, w_ada, b_ada, w_in, w_out, g_post_mix, g_pre_ffn,
           w_group, b_group, w_expert, b_expert, w_gate_exp, w_up_exp, w_down_exp, g_post_ffn):
    h = x
    for l in range(w_in.shape[0]):
        mod = _ada(c, w_ada[l], b_ada[l])
        h = _layer(h, positions, mod, g_pre_mix[l], w_in[l], w_out[l], g_post_mix[l], g_pre_ffn[l],
                   w_group[l], b_group[l], w_expert[l], b_expert[l],
                   w_gate_exp[l], w_up_exp[l], w_down_exp[l], g_post_ffn[l])
    return h
```

```python
import functools
import math

import numpy as np
import jax
import jax.numpy as jnp
from jax import lax
from jax.experimental import pallas as pl
from jax.experimental.pallas import tpu as pltpu

D_MODEL = 1024
ATT_HEADS = 8
ATT_HEAD_DIM = 64
IDX_HEADS = 8
IDX_HEAD_DIM = 64
TOPK_MAX = 256
RET_HEADS = 4
RET_HEAD_DIM = 128
ROPE_THETA = 10000.0
ATT_WIDTH = ATT_HEADS * ATT_HEAD_DIM
RET_WIDTH = RET_HEADS * RET_HEAD_DIM
N_GROUPS = 4
EXPERTS_PER_GROUP = 8
N_EXPERTS = N_GROUPS * EXPERTS_PER_GROUP
D_EXPERT = 256
N_MOD = 6
EPS = 1e-6

LANES = 128
SUB = 8
VMEM_LIMIT = 56 * 1024 * 1024

DSA_TQ = 256
DSA_TK = 256
V_ROWS = 80

F32 = jnp.float32
BF16 = jnp.bfloat16
NEG_INF = float("-inf")
F32_LOWEST = float(np.finfo(np.float32).min)
INT_MIN = -2 ** 31
DENOM_FLOOR = 1e-30

_NT = (((1,), (1,)), ((), ()))
_TN = (((0,), (0,)), ((), ()))


def _params(*sem):
    return pltpu.CompilerParams(dimension_semantics=sem, vmem_limit_bytes=VMEM_LIMIT)


def _rms(x):
    return x * lax.rsqrt(jnp.mean(x * x, axis=-1, keepdims=True) + EPS)


def _silu(x):
    return x * (1.0 / (1.0 + jnp.exp(-x)))


def _ada_kernel(c_ref, w_ref, b_ref, o_ref):
    a = _silu(c_ref[...]).astype(BF16)
    o_ref[...] = jnp.dot(a, w_ref[...].astype(BF16), preferred_element_type=F32) + b_ref[...]


def _ada(c, w, b):
    bsz = c.shape[0]
    n = w.shape[1]
    tn = D_MODEL
    return pl.pallas_call(
        _ada_kernel,
        grid=(n // tn,),
        in_specs=[
            pl.BlockSpec((bsz, D_MODEL), lambda j: (0, 0)),
            pl.BlockSpec((D_MODEL, tn), lambda j: (0, j)),
            pl.BlockSpec((1, tn), lambda j: (0, j)),
        ],
        out_specs=pl.BlockSpec((bsz, tn), lambda j: (0, j)),
        out_shape=jax.ShapeDtypeStruct((bsz, n), F32),
        compiler_params=_params("arbitrary"),
        name="ada",
    )(c, w, b.reshape(1, n))


def _tables_kernel(pos_ref, f_ref, cs64_ref, sn64_ref, cs128_ref, sn128_ref):
    ang = pos_ref[0].astype(F32) * f_ref[...]
    c = jnp.cos(ang)
    s = jnp.sin(ang)
    c_sw = pltpu.roll(c, 64, 1)
    s_sw = pltpu.roll(s, 64, 1)
    lane = lax.broadcasted_iota(jnp.int32, c.shape, 1)
    lo = lane < 64
    cs128_ref[0] = jnp.where(lo, c, c_sw)
    sn128_ref[0] = jnp.where(lo, -s, s_sw)
    cs64_ref[0] = jnp.where(lo, c_sw, c)
    s64 = jnp.where(lo, s_sw, s)
    sn64_ref[0] = jnp.where((lane % 64) < 32, -s64, s64)


def _rope_tables(positions, ts):
    bsz, seq = positions.shape
    f64 = ROPE_THETA ** (-jnp.arange(32, dtype=F32) / 32)
    f128 = ROPE_THETA ** (-jnp.arange(64, dtype=F32) / 64)
    frow = jnp.concatenate([f128, f64, f64]).reshape(1, LANES)
    tab = jax.ShapeDtypeStruct((bsz, seq, LANES), F32)
    tspec = pl.BlockSpec((1, ts, LANES), lambda b, i: (b, i, 0))
    return pl.pallas_call(
        _tables_kernel,
        grid=(bsz, seq // ts),
        in_specs=[
            pl.BlockSpec((1, ts, 1), lambda b, i: (b, i, 0)),
            pl.BlockSpec((1, LANES), lambda b, i: (0, 0)),
        ],
        out_specs=[tspec] * 4,
        out_shape=[tab] * 4,
        compiler_params=_params("parallel", "parallel"),
        name="tables",
    )(positions.reshape(bsz, seq, 1), frow)


def _rope64(y, cs, sn):
    lane = lax.broadcasted_iota(jnp.int32, y.shape, 1)
    rot = jnp.where((lane % 64) < 32, pltpu.roll(y, 96, 1), pltpu.roll(y, 32, 1))
    return y * cs + rot * sn


def _rope128(y, cs, sn):
    return y * cs + pltpu.roll(y, 64, 1) * sn


def _inproj_kernel(x_ref, shift_ref, scale_ref, g_ref,
                   wqa_ref, wka_ref, wv_ref, wqi_ref, wki_ref,
                   wqr_ref, wkr_ref, wvr_ref, wgr_ref,
                   cs64_ref, sn64_ref, cs128_ref, sn128_ref,
                   qa_ref, ka_ref, vt_ref, qi_ref, ki_ref, wit_ref,
                   qr_ref, kr_ref, vr_ref, gr_ref):
    x = x_ref[0]
    h = _rms(x) * g_ref[...] * (1.0 + scale_ref[0]) + shift_ref[0]
    hb = h.astype(BF16)
    cs64, sn64 = cs64_ref[0], sn64_ref[0]
    cs128, sn128 = cs128_ref[0], sn128_ref[0]

    lane = lax.broadcasted_iota(jnp.int32, (hb.shape[0], LANES), 1)
    wide = 2 * LANES

    def pairs(w_ref, rope, cs, sn):
        for c in range(w_ref.shape[1] // wide):
            y = jnp.dot(hb, w_ref[:, c * wide:(c + 1) * wide], preferred_element_type=F32)
            for j in range(wide // LANES):
                yield 2 * c + j, rope(y[:, j * LANES:(j + 1) * LANES], cs, sn)

    def query_heads(w_ref, o_ref, mult):
        for c, y in pairs(w_ref, _rope64, cs64, sn64):
            y = y * mult
            o_ref[0, 2 * c] = jnp.where(lane < 64, y, 0.0).astype(BF16)
            o_ref[0, 2 * c + 1] = jnp.where(lane >= 64, y, 0.0).astype(BF16)

    query_heads(wqa_ref, qa_ref, ATT_HEAD_DIM ** -0.5)
    query_heads(wqi_ref, qi_ref, 1.0)
    for c, y in pairs(wka_ref, _rope64, cs64, sn64):
        ka_ref[0, c] = y.astype(BF16)

    ts = hb.shape[0]
    vt = jnp.dot(hb, wv_ref[...], preferred_element_type=F32).T
    for h in range(ATT_HEADS):
        for j in range(ts // DSA_TK):
            blk = vt[h * 64:(h + 1) * 64, j * DSA_TK:(j + 1) * DSA_TK]
            vt_ref[0, h, j, :64, :] = blk.astype(BF16)
            vt_ref[0, h, j, 64:, :] = jnp.ones((V_ROWS - 64, DSA_TK), BF16)

    yk = jnp.dot(hb, wki_ref[...], preferred_element_type=F32)
    ki_ref[0] = _rope64(yk[:, :LANES], cs64, sn64).astype(BF16)
    wit_ref[0] = yk[:, LANES:].T[:IDX_HEADS] * ((IDX_HEADS ** -0.5) * (IDX_HEAD_DIM ** -0.5))

    for c, y in pairs(wqr_ref, _rope128, cs128, sn128):
        qr_ref[0, :, c * LANES:(c + 1) * LANES] = y.astype(BF16)
    for c, y in pairs(wkr_ref, _rope128, cs128, sn128):
        kr_ref[0, :, c * LANES:(c + 1) * LANES] = (y * (RET_HEAD_DIM ** -0.5)).astype(BF16)
    vr_ref[0] = jnp.dot(hb, wvr_ref[...], preferred_element_type=F32).astype(BF16)
    gr_ref[0] = jnp.dot(hb, wgr_ref[...], preferred_element_type=F32).astype(BF16)


def _inproj(x, shift, scale, g, weights, tables, ts):
    bsz, seq, _ = x.shape
    row = lambda b, i: (b, i, 0)
    per_b = lambda b, i: (b, 0, 0)
    const2 = lambda b, i: (0, 0)
    head_major = lambda b, i: (b, 0, i, 0)

    def wspec(w):
        return pl.BlockSpec(w.shape, const2)

    hm_shape = jax.ShapeDtypeStruct((bsz, ATT_HEADS, seq, LANES), BF16)
    hm_spec = pl.BlockSpec((1, ATT_HEADS, ts, LANES), head_major)
    pair_shape = jax.ShapeDtypeStruct((bsz, ATT_HEADS // 2, seq, LANES), BF16)
    pair_spec = pl.BlockSpec((1, ATT_HEADS // 2, ts, LANES), head_major)
    wide = lambda n, dt=BF16: jax.ShapeDtypeStruct((bsz, seq, n), dt)
    wspec_out = lambda n: pl.BlockSpec((1, ts, n), row)
    return pl.pallas_call(
        _inproj_kernel,
        grid=(bsz, seq // ts),
        in_specs=[
            pl.BlockSpec((1, ts, D_MODEL), row),
            pl.BlockSpec((1, 1, D_MODEL), per_b),
            pl.BlockSpec((1, 1, D_MODEL), per_b),
            pl.BlockSpec((1, D_MODEL), const2),
            *[wspec(w) for w in weights],
            *[pl.BlockSpec((1, ts, LANES), row)] * 4,
        ],
        out_specs=[hm_spec, pair_spec,
                   pl.BlockSpec((1, ATT_HEADS, ts // DSA_TK, V_ROWS, DSA_TK), lambda b, i: (b, 0, i, 0, 0)),
                   hm_spec, wspec_out(LANES),
                   pl.BlockSpec((1, IDX_HEADS, ts), lambda b, i: (b, 0, i)),
                   wspec_out(RET_WIDTH), wspec_out(RET_WIDTH), wspec_out(RET_WIDTH), wspec_out(RET_WIDTH)],
        out_shape=[hm_shape, pair_shape,
                   jax.ShapeDtypeStruct((bsz, ATT_HEADS, seq // DSA_TK, V_ROWS, DSA_TK), BF16),
                   hm_shape, wide(LANES),
                   jax.ShapeDtypeStruct((bsz, IDX_HEADS, seq), F32),
                   wide(RET_WIDTH), wide(RET_WIDTH), wide(RET_WIDTH), wide(RET_WIDTH)],
        compiler_params=_params("parallel", "parallel"),
        name="inproj",
    )(x, shift, scale, g, *weights, *tables)


def _key_to_float(u):
    key = u ^ INT_MIN
    bits = jnp.where(key >= 0, key, key ^ 0x7FFFFFFF)
    return lax.bitcast_convert_type(bits, F32)


def _dsa_kernel(qi_ref, ki_ref, wit_ref, qa_ref, ka_ref, vt_ref, o_ref,
                sc_ref, m_ref, acc_ref, kmax_ref, *, topk):
    tq, tk = DSA_TQ, DSA_TK
    qb = pl.program_id(1)
    nk = (qb + 1) * (tq // tk)
    grp = (tk // SUB, SUB, tq)
    q_pos = qb * tq + lax.broadcasted_iota(jnp.int32, (SUB, tq), 1)

    def key_index(kc):
        return (kc * tk + lax.broadcasted_iota(jnp.int32, grp, 0) * SUB
                + lax.broadcasted_iota(jnp.int32, grp, 1))

    wt = wit_ref[0]

    def score_chunk(kc, carry):
        k0 = pl.multiple_of(kc * tk, tk)
        kch = ki_ref[0, pl.ds(k0, tk), :]
        acc = jnp.zeros((tk, tq), F32)
        for h in range(IDX_HEADS):
            logit = lax.dot_general(kch, qi_ref[0, h], _NT, preferred_element_type=F32)
            acc = acc + jnp.maximum(logit, 0.0) * wt[h:h + 1, :]
        sc_ref[kc] = jnp.where(key_index(kc) <= q_pos[None], acc.reshape(grp), NEG_INF)
        return carry

    lax.fori_loop(0, nk, score_chunk, 0)

    def count(pred):
        def body(kc, acc):
            hit = jnp.where(pred(sc_ref[kc], kc), 1.0, 0.0)
            parts = [hit[g] for g in range(grp[0])]
            while len(parts) > 1:
                parts = [parts[i] + parts[i + 1] for i in range(0, len(parts), 2)]
            return acc + parts[0]
        acc = lax.fori_loop(0, nk, body, jnp.zeros((SUB, tq), F32))
        return jnp.sum(acc, axis=0, keepdims=True)

    def rows(v):
        return jnp.broadcast_to(v, (SUB, tq))[None]

    def bit_body(i, u):
        trial = u | lax.shift_left(jnp.int32(1), 31 - i)
        cand = rows(_key_to_float(trial))
        cnt = count(lambda x, kc: x >= cand)
        return jnp.where(cnt >= topk, trial, u)

    u = lax.fori_loop(0, 32, bit_body, jnp.zeros((1, tq), jnp.int32))
    thr1 = jnp.where(q_pos[:1] < topk - 1, F32_LOWEST, _key_to_float(u))
    thr = rows(thr1)
    cnt_ge = count(lambda x, kc: x >= thr)
    has_excess_ties = jnp.max(cnt_ge) > topk

    def tie_cut():
        need = topk - count(lambda x, kc: x > thr)

        def cut_body(i, c):
            trial = c | lax.shift_left(jnp.int32(1), 12 - i)
            tb = rows(trial)
            f = count(lambda x, kc: (x == thr) & (key_index(kc) < tb))
            return jnp.where(f <= need, trial, c)

        return lax.fori_loop(0, 13, cut_body, jnp.zeros((1, tq), jnp.int32))

    cut = rows(lax.cond(has_excess_ties, tie_cut,
                        lambda: jnp.full((1, tq), 2 ** 13 - 1, jnp.int32)))

    def mask_body(kc, c):
        x = sc_ref[kc]
        keep = (x > thr) | ((x == thr) & (key_index(kc) < cut))
        sc_ref[kc] = jnp.where(keep, 0.0, NEG_INF)
        return c

    lax.fori_loop(0, nk, mask_body, 0)

    def logits(kc, h):
        k0 = pl.multiple_of(kc * tk, tk)
        kch = ka_ref[0, h // 2, pl.ds(k0, tk), :]
        s = lax.dot_general(kch, qa_ref[0, h], _NT, preferred_element_type=F32)
        return s.reshape(grp) + sc_ref[kc]

    def max_chunk(kc, m):
        return jnp.stack([jnp.maximum(m[h], jnp.max(logits(kc, h), axis=0))
                          for h in range(ATT_HEADS)])

    def pv_chunk(kc, carry):
        ss = [logits(kc, h) for h in range(ATT_HEADS)]
        for h in range(ATT_HEADS):
            p = jnp.exp(ss[h] - m_ref[h][None]).reshape(tk, tq).astype(BF16)
            acc_ref[h] += jnp.dot(vt_ref[0, h, kc], p, preferred_element_type=F32)
        return carry

    def exp_pv_pass():
        acc_ref[...] = jnp.zeros(acc_ref.shape, F32)
        lax.fori_loop(0, nk, pv_chunk, 0)

    ones_r = jnp.ones((SUB, LANES), BF16)
    lane_r = lax.broadcasted_iota(jnp.int32, (SUB, LANES), 1)

    @pl.when(qb == 0)
    def _():
        for h in range(ATT_HEADS):
            k = ka_ref[0, h // 2]
            head_lanes = jnp.where((lane_r >= 64) == (h % 2 == 1), 1.0, 0.0).astype(BF16)
            k_norm2 = lax.dot_general(head_lanes, k * k, _NT, preferred_element_type=F32)
            kmax_ref[h] = jnp.broadcast_to(jnp.max(k_norm2, axis=1, keepdims=True), (SUB, tq))

    for h in range(ATT_HEADS):
        q = qa_ref[0, h]
        q_norm2 = lax.dot_general(ones_r, q * q, _NT, preferred_element_type=F32)
        m_ref[h] = jnp.sqrt(q_norm2 * kmax_ref[h])
    exp_pv_pass()
    denom_min = jnp.min(jnp.stack([acc_ref[h][64:65] for h in range(ATT_HEADS)]))

    @pl.when(jnp.logical_not(denom_min > DENOM_FLOOR))
    def _():
        m8 = lax.fori_loop(0, nk, max_chunk, jnp.full((ATT_HEADS, SUB, tq), NEG_INF, F32))
        m_ref[...] = jnp.broadcast_to(jnp.max(m8, axis=1, keepdims=True), m8.shape)
        exp_pv_pass()

    outs = []
    for h in range(ATT_HEADS):
        acc = acc_ref[h]
        outs.append(acc[:64] / acc[64:65])
    o_ref[0] = jnp.concatenate(outs, axis=0).T.astype(BF16)


def _dsa(qi, ki, wit, qa, ka, vt):
    bsz, _, seq, _ = qi.shape
    tq, tk = DSA_TQ, DSA_TK
    topk = min(TOPK_MAX, seq // 4)
    assert seq % tq == 0 and tq % tk == 0 and tk >= topk
    blk_q = lambda b, i: (b, 0, i, 0)
    return pl.pallas_call(
        functools.partial(_dsa_kernel, topk=topk),
        grid=(bsz, seq // tq),
        in_specs=[
            pl.BlockSpec((1, IDX_HEADS, tq, LANES), blk_q),
            pl.BlockSpec((1, seq, LANES), lambda b, i: (b, 0, 0)),
            pl.BlockSpec((1, IDX_HEADS, tq), lambda b, i: (b, 0, i)),
            pl.BlockSpec((1, ATT_HEADS, tq, LANES), blk_q),
            pl.BlockSpec((1, ATT_HEADS // 2, seq, LANES), lambda b, i: (b, 0, 0, 0)),
            pl.BlockSpec((1, ATT_HEADS, seq // tk, V_ROWS, tk), lambda b, i: (b, 0, 0, 0, 0)),
        ],
        out_specs=pl.BlockSpec((1, tq, ATT_WIDTH), lambda b, i: (b, i, 0)),
        out_shape=jax.ShapeDtypeStruct((bsz, seq, ATT_WIDTH), BF16),
        scratch_shapes=[
            pltpu.VMEM((seq // tk, tk // SUB, SUB, tq), F32),
            pltpu.VMEM((ATT_HEADS, SUB, tq), F32),
            pltpu.VMEM((ATT_HEADS, V_ROWS, tq), F32),
            pltpu.VMEM((ATT_HEADS, SUB, tq), F32),
        ],
        compiler_params=_params("parallel", "arbitrary"),
        name="dsa",
    )(qi, ki, wit, qa, ka, vt)


RET_CHUNK = 256


def _ret_kernel(q_ref, k_ref, v_ref, g_ref, o_ref, state_ref, decay_ref):
    c = RET_CHUNK
    n = pl.program_id(1)
    row = lax.broadcasted_iota(jnp.int32, (c, c), 0)
    col = lax.broadcasted_iota(jnp.int32, (c, c), 1)
    idx = lax.broadcasted_iota(jnp.int32, (c, 1), 0).astype(F32)

    @pl.when(n == 0)
    def _():
        state_ref[...] = jnp.zeros_like(state_ref)
        for h in range(RET_HEADS):
            lg = math.log(1.0 - 2.0 ** (-5.0 - h))
            diff = (row - col).astype(F32)
            decay_ref[h] = jnp.where(row >= col, jnp.exp(lg * jnp.maximum(diff, 0.0)), 0.0)

    for h in range(RET_HEADS):
        lg = math.log(1.0 - 2.0 ** (-5.0 - h))
        sl = slice(h * LANES, (h + 1) * LANES)
        q = q_ref[0, :, sl]
        k = k_ref[0, :, sl]
        v = v_ref[0, :, sl]
        inner = lax.dot_general(q, k, _NT, preferred_element_type=F32) * decay_ref[h]
        o = jnp.dot(inner.astype(BF16), v, preferred_element_type=F32)
        qd = (q.astype(F32) * jnp.exp(lg * (idx + 1.0))).astype(BF16)
        state = state_ref[h]
        o = o + jnp.dot(qd, state.astype(BF16), preferred_element_type=F32)
        kd = (k.astype(F32) * jnp.exp(lg * (c - 1.0 - idx))).astype(BF16)
        kv = lax.dot_general(kd, v, _TN, preferred_element_type=F32)
        state_ref[h] = state * math.exp(lg * c) + kv
        gate = _silu(g_ref[0, :, sl].astype(F32))
        o_ref[0, :, sl] = (gate * _rms(o)).astype(BF16)


def _retention(qr, kr, vr, gr):
    bsz, seq, _ = qr.shape
    c = RET_CHUNK
    spec = pl.BlockSpec((1, c, RET_WIDTH), lambda b, i: (b, i, 0))
    return pl.pallas_call(
        _ret_kernel,
        grid=(bsz, seq // c),
        in_specs=[spec] * 4,
        out_specs=spec,
        out_shape=jax.ShapeDtypeStruct((bsz, seq, RET_WIDTH), BF16),
        scratch_shapes=[
            pltpu.VMEM((RET_HEADS, RET_HEAD_DIM, RET_HEAD_DIM), F32),
            pltpu.VMEM((RET_HEADS, c, c), F32),
        ],
        compiler_params=_params("arbitrary", "arbitrary"),
        name="ret",
    )(qr, kr, vr, gr)


ROUTER_E0 = N_GROUPS


def _mid_kernel(att_ref, ret_ref, x_ref, woa_ref, wor_ref, gpost_ref, gate1_ref,
                gpre_ref, scale2_ref, shift2_ref, wr_ref, br_ref,
                h1_ref, h2_ref, comb_ref):
    mix = jnp.dot(att_ref[0], woa_ref[...], preferred_element_type=F32)
    mix = mix + jnp.dot(ret_ref[0], wor_ref[...], preferred_element_type=F32)
    h1 = x_ref[0] + gate1_ref[0] * (_rms(mix) * gpost_ref[...])
    h1_ref[0] = h1
    h2 = (_rms(h1) * gpre_ref[...] * (1.0 + scale2_ref[0]) + shift2_ref[0]).astype(BF16)
    h2_ref[0] = h2

    logits = jnp.dot(h2, wr_ref[...], preferred_element_type=F32) + br_ref[...]
    lane = lax.broadcasted_iota(jnp.int32, logits.shape, 1)
    big = jnp.int32(LANES)

    def first_argmax(vals, vmax):
        return jnp.min(jnp.where(vals == vmax, lane, big), axis=1, keepdims=True)

    gl = jnp.where(lane < N_GROUPS, logits, NEG_INF)
    gexp = jnp.exp(gl - jnp.max(gl, axis=1, keepdims=True))
    gprob = gexp / jnp.sum(gexp, axis=1, keepdims=True)
    g_top = jnp.max(gprob, axis=1, keepdims=True)
    g_sel = first_argmax(gprob, g_top)
    e_lo = ROUTER_E0 + g_sel * EXPERTS_PER_GROUP
    in_group = (lane >= e_lo) & (lane < e_lo + EXPERTS_PER_GROUP)
    el = jnp.where(in_group, logits, NEG_INF)
    eexp = jnp.exp(el - jnp.max(el, axis=1, keepdims=True))
    eprob = jnp.where(in_group, eexp / jnp.sum(eexp, axis=1, keepdims=True), -1.0)
    top1 = jnp.max(eprob, axis=1, keepdims=True)
    idx1 = first_argmax(eprob, top1)
    rest = jnp.where(lane == idx1, -1.0, eprob)
    top2 = jnp.max(rest, axis=1, keepdims=True)
    idx2 = first_argmax(rest, top2)
    norm = g_top / (top1 + top2)
    comb_ref[0] = jnp.where(lane == idx1, top1 * norm, 0.0) + jnp.where(lane == idx2, top2 * norm, 0.0)


def _mid(att, ret, x, woa, wor, gpost, gate1, gpre, scale2, shift2, wr, br, ts):
    bsz, seq, _ = x.shape
    row = lambda b, i: (b, i, 0)
    per_b = lambda b, i: (b, 0, 0)
    const2 = lambda b, i: (0, 0)
    vec = pl.BlockSpec((1, D_MODEL), const2)
    bvec = pl.BlockSpec((1, 1, D_MODEL), per_b)
    return pl.pallas_call(
        _mid_kernel,
        grid=(bsz, seq // ts),
        in_specs=[
            pl.BlockSpec((1, ts, ATT_WIDTH), row),
            pl.BlockSpec((1, ts, RET_WIDTH), row),
            pl.BlockSpec((1, ts, D_MODEL), row),
            pl.BlockSpec(woa.shape, const2),
            pl.BlockSpec(wor.shape, const2),
            vec, bvec, vec, bvec, bvec,
            pl.BlockSpec(wr.shape, const2),
            pl.BlockSpec((1, LANES), const2),
        ],
        out_specs=[pl.BlockSpec((1, ts, D_MODEL), row),
                   pl.BlockSpec((1, ts, D_MODEL), row),
                   pl.BlockSpec((1, ts, LANES), row)],
        out_shape=[jax.ShapeDtypeStruct((bsz, seq, D_MODEL), F32),
                   jax.ShapeDtypeStruct((bsz, seq, D_MODEL), BF16),
                   jax.ShapeDtypeStruct((bsz, seq, LANES), F32)],
        compiler_params=_params("parallel", "parallel"),
        name="mid",
    )(att, ret, x, woa, wor, gpost, gate1, gpre, scale2, shift2, wr, br)


MOE_EPS = 8


def _moe_kernel(h2_ref, comb_ref, h1_ref, gate2_ref, gpost_ref, wgu_ref, wd_ref, o_ref, acc_ref):
    step = pl.program_id(2)

    @pl.when(step == 0)
    def _():
        acc_ref[...] = jnp.zeros_like(acc_ref)

    t = h2_ref[0]
    comb = comb_ref[0]
    lane = lax.broadcasted_iota(jnp.int32, comb.shape, 1)
    hids = []
    for j in range(MOE_EPS):
        gu = jnp.dot(t, wgu_ref[j], preferred_element_type=F32)
        hid = _silu(gu[:, :D_EXPERT]) * gu[:, D_EXPERT:]
        e_lane = ROUTER_E0 + step * MOE_EPS + j
        cw = jnp.sum(jnp.where(lane == e_lane, comb, 0.0), axis=1, keepdims=True)
        hids.append((hid * cw).astype(BF16))
    wd = wd_ref[...].reshape(MOE_EPS * D_EXPERT, D_MODEL)
    acc_ref[...] += jnp.dot(jnp.concatenate(hids, axis=1), wd, preferred_element_type=F32)

    @pl.when(step == N_EXPERTS // MOE_EPS - 1)
    def _():
        o_ref[0] = h1_ref[0] + gate2_ref[0] * (_rms(acc_ref[...]) * gpost_ref[...])


def _moe(h2, comb, h1, gate2, gpost, wgu, wd, tm):
    bsz, seq, _ = h2.shape
    row = lambda b, i, e: (b, i, 0)
    return pl.pallas_call(
        _moe_kernel,
        grid=(bsz, seq // tm, N_EXPERTS // MOE_EPS),
        in_specs=[
            pl.BlockSpec((1, tm, D_MODEL), row),
            pl.BlockSpec((1, tm, LANES), row),
            pl.BlockSpec((1, tm, D_MODEL), row),
            pl.BlockSpec((1, 1, D_MODEL), lambda b, i, e: (b, 0, 0)),
            pl.BlockSpec((1, D_MODEL), lambda b, i, e: (0, 0)),
            pl.BlockSpec((MOE_EPS, D_MODEL, 2 * D_EXPERT), lambda b, i, e: (e, 0, 0)),
            pl.BlockSpec((MOE_EPS, D_EXPERT, D_MODEL), lambda b, i, e: (e, 0, 0)),
        ],
        out_specs=pl.BlockSpec((1, tm, D_MODEL), row),
        out_shape=jax.ShapeDtypeStruct((bsz, seq, D_MODEL), F32),
        scratch_shapes=[pltpu.VMEM((tm, D_MODEL), F32)],
        compiler_params=_params("parallel", "parallel", "arbitrary"),
        name="moe",
    )(h2, comb, h1, gate2, gpost, wgu, wd)


def _layer(x, positions, mod, g_pre_mix, w_in, w_out, g_post_mix, g_pre_ffn,
           w_group, b_group, w_expert, b_expert, w_gate_exp, w_up_exp, w_down_exp, g_post_ffn):
    bsz, seq, _ = x.shape
    shift1, scale1, gate1, shift2, scale2, gate2 = [
        m.reshape(bsz, 1, D_MODEL) for m in jnp.split(mod, N_MOD, axis=-1)]
    vec = lambda g: g.reshape(1, D_MODEL)

    a, i8, r = ATT_WIDTH, IDX_HEADS, RET_WIDTH
    offs = np.cumsum([0, a, a, a, a, IDX_HEAD_DIM, i8, r, r, r, r])
    seg = lambda j: w_in[:, offs[j]:offs[j + 1]].astype(BF16)
    wki = jnp.concatenate([seg(4), seg(4), jnp.pad(seg(5), ((0, 0), (0, LANES - i8)))], axis=1)
    weights = [seg(0), seg(1), seg(2), seg(3), wki,
               seg(6), seg(7), seg(8), seg(9)]

    tables = _rope_tables(positions, 512)
    qa, ka, vt, qi, ki, wit, qr, kr, vr, gr = _inproj(
        x, shift1, scale1, vec(g_pre_mix), weights, tables, 512)
    att = _dsa(qi, ki, wit, qa, ka, vt)
    ret = _retention(qr, kr, vr, gr)

    w_router = jnp.pad(jnp.concatenate([w_group, w_expert], axis=1),
                       ((0, 0), (0, LANES - N_GROUPS - N_EXPERTS))).astype(BF16)
    b_router = jnp.pad(jnp.concatenate([b_group, b_expert]),
                       (0, LANES - N_GROUPS - N_EXPERTS)).reshape(1, LANES)
    h1, h2, comb = _mid(att, ret, x, w_out[:ATT_WIDTH].astype(BF16), w_out[ATT_WIDTH:].astype(BF16),
                        vec(g_post_mix), gate1, vec(g_pre_ffn), scale2, shift2, w_router, b_router, 512)

    wgu = jnp.concatenate([w_gate_exp, w_up_exp], axis=2).astype(BF16)
    return _moe(h2, comb, h1, gate2, vec(g_post_ffn), wgu, w_down_exp.astype(BF16), 1024)


def kernel(x, c, positions, g_pre_mix, w_ada, b_ada, w_in, w_out, g_post_mix, g_pre_ffn,
           w_group, b_group, w_expert, b_expert, w_gate_exp, w_up_exp, w_down_exp, g_post_ffn):
    h = x
    for l in range(w_in.shape[0]):
        mod = _ada(c, w_ada[l], b_ada[l])
        h = _layer(h, positions, mod, g_pre_mix[l], w_in[l], w_out[l], g_post_mix[l], g_pre_ffn[l],
                   w_group[l], b_group[l], w_expert[l], b_expert[l],
                   w_gate_exp[l], w_up_exp[l], w_down_exp[l], g_post_ffn[l])
    return h
```

```python
import functools
import math

import numpy as np
import jax
import jax.numpy as jnp
from jax import lax
from jax.experimental import pallas as pl
from jax.experimental.pallas import tpu as pltpu

D_MODEL = 1024
ATT_HEADS = 8
ATT_HEAD_DIM = 64
IDX_HEADS = 8
IDX_HEAD_DIM = 64
TOPK_MAX = 256
RET_HEADS = 4
RET_HEAD_DIM = 128
ROPE_THETA = 10000.0
ATT_WIDTH = ATT_HEADS * ATT_HEAD_DIM
RET_WIDTH = RET_HEADS * RET_HEAD_DIM
N_GROUPS = 4
EXPERTS_PER_GROUP = 8
N_EXPERTS = N_GROUPS * EXPERTS_PER_GROUP
D_EXPERT = 256
N_MOD = 6
EPS = 1e-6

LANES = 128
SUB = 8
VMEM_LIMIT = 56 * 1024 * 1024

DSA_TQ = 256
DSA_TK = 256
V_ROWS = 80
F32 = jnp.float32
BF16 = jnp.bfloat16
NEG_INF = float("-inf")
F32_LOWEST = float(np.finfo(np.float32).min)
INT_MIN = -2 ** 31
DENOM_FLOOR = 1e-30

_NT = (((1,), (1,)), ((), ()))
_TN = (((0,), (0,)), ((), ()))


def _params(*sem):
    return pltpu.CompilerParams(dimension_semantics=sem, vmem_limit_bytes=VMEM_LIMIT)


def _rms(x):
    return x * lax.rsqrt(jnp.mean(x * x, axis=-1, keepdims=True) + EPS)


def _silu(x):
    return x * (1.0 / (1.0 + jnp.exp(-x)))


def _ada_kernel(c_ref, w_ref, b_ref, o_ref):
    a = _silu(c_ref[...]).astype(BF16)
    o_ref[...] = jnp.dot(a, w_ref[...].astype(BF16), preferred_element_type=F32) + b_ref[...]


def _ada(c, w, b):
    bsz = c.shape[0]
    n = w.shape[1]
    tn = D_MODEL
    return pl.pallas_call(
        _ada_kernel,
        grid=(n // tn,),
        in_specs=[
            pl.BlockSpec((bsz, D_MODEL), lambda j: (0, 0)),
            pl.BlockSpec((D_MODEL, tn), lambda j: (0, j)),
            pl.BlockSpec((1, tn), lambda j: (0, j)),
        ],
        out_specs=pl.BlockSpec((bsz, tn), lambda j: (0, j)),
        out_shape=jax.ShapeDtypeStruct((bsz, n), F32),
        compiler_params=_params("arbitrary"),
        name="ada",
    )(c, w, b.reshape(1, n))


def _tables_kernel(pos_ref, f_ref, cs64_ref, sn64_ref, cs128_ref, sn128_ref):
    ang = pos_ref[0].astype(F32) * f_ref[...]
    c = jnp.cos(ang)
    s = jnp.sin(ang)
    c_sw = pltpu.roll(c, 64, 1)
    s_sw = pltpu.roll(s, 64, 1)
    lane = lax.broadcasted_iota(jnp.int32, c.shape, 1)
    lo = lane < 64
    cs128_ref[0] = jnp.where(lo, c, c_sw)
    sn128_ref[0] = jnp.where(lo, -s, s_sw)
    cs64_ref[0] = jnp.where(lo, c_sw, c)
    s64 = jnp.where(lo, s_sw, s)
    sn64_ref[0] = jnp.where((lane % 64) < 32, -s64, s64)


def _rope_tables(positions, ts):
    bsz, seq = positions.shape
    f64 = ROPE_THETA ** (-jnp.arange(32, dtype=F32) / 32)
    f128 = ROPE_THETA ** (-jnp.arange(64, dtype=F32) / 64)
    frow = jnp.concatenate([f128, f64, f64]).reshape(1, LANES)
    tab = jax.ShapeDtypeStruct((bsz, seq, LANES), F32)
    tspec = pl.BlockSpec((1, ts, LANES), lambda b, i: (b, i, 0))
    return pl.pallas_call(
        _tables_kernel,
        grid=(bsz, seq // ts),
        in_specs=[
            pl.BlockSpec((1, ts, 1), lambda b, i: (b, i, 0)),
            pl.BlockSpec((1, LANES), lambda b, i: (0, 0)),
        ],
        out_specs=[tspec] * 4,
        out_shape=[tab] * 4,
        compiler_params=_params("parallel", "parallel"),
        name="tables",
    )(positions.reshape(bsz, seq, 1), frow)


def _rope64(y, cs, sn):
    lane = lax.broadcasted_iota(jnp.int32, y.shape, 1)
    rot = jnp.where((lane % 64) < 32, pltpu.roll(y, 96, 1), pltpu.roll(y, 32, 1))
    return y * cs + rot * sn


def _rope128(y, cs, sn):
    return y * cs + pltpu.roll(y, 64, 1) * sn


def _inproj_kernel(x_ref, shift_ref, scale_ref, g_ref,
                   wqa_ref, wka_ref, wv_ref, wqi_ref, wki_ref,
                   wqr_ref, wkr_ref, wvr_ref, wgr_ref,
                   cs64_ref, sn64_ref, cs128_ref, sn128_ref,
                   qa_ref, ka_ref, vt_ref, qi_ref, ki_ref, wit_ref,
                   qr_ref, kr_ref, vr_ref, gr_ref):
    x = x_ref[0]
    h = _rms(x) * g_ref[...] * (1.0 + scale_ref[0]) + shift_ref[0]
    hb = h.astype(BF16)
    cs64, sn64 = cs64_ref[0], sn64_ref[0]
    cs128, sn128 = cs128_ref[0], sn128_ref[0]

    lane = lax.broadcasted_iota(jnp.int32, (hb.shape[0], LANES), 1)
    wide = 2 * LANES

    def pairs(w_ref, rope, cs, sn):
        for c in range(w_ref.shape[1] // wide):
            y = jnp.dot(hb, w_ref[:, c * wide:(c + 1) * wide], preferred_element_type=F32)
            for j in range(wide // LANES):
                yield 2 * c + j, rope(y[:, j * LANES:(j + 1) * LANES], cs, sn)

    def query_heads(w_ref, o_ref, mult):
        for c, y in pairs(w_ref, _rope64, cs64, sn64):
            y = y * mult
            o_ref[0, 2 * c] = jnp.where(lane < 64, y, 0.0).astype(BF16)
            o_ref[0, 2 * c + 1] = jnp.where(lane >= 64, y, 0.0).astype(BF16)

    query_heads(wqa_ref, qa_ref, ATT_HEAD_DIM ** -0.5)
    query_heads(wqi_ref, qi_ref, 1.0)
    for c, y in pairs(wka_ref, _rope64, cs64, sn64):
        ka_ref[0, c] = y.astype(BF16)

    ts = hb.shape[0]
    vt = jnp.dot(hb, wv_ref[...], preferred_element_type=F32).T
    for h in range(ATT_HEADS):
        for j in range(ts // DSA_TK):
            blk = vt[h * 64:(h + 1) * 64, j * DSA_TK:(j + 1) * DSA_TK]
            vt_ref[0, h, j, :64, :] = blk.astype(BF16)
            vt_ref[0, h, j, 64:, :] = jnp.ones((V_ROWS - 64, DSA_TK), BF16)

    yk = jnp.dot(hb, wki_ref[...], preferred_element_type=F32)
    ki_ref[0] = _rope64(yk[:, :LANES], cs64, sn64).astype(BF16)
    wit_ref[0] = yk[:, LANES:].T[:IDX_HEADS] * ((IDX_HEADS ** -0.5) * (IDX_HEAD_DIM ** -0.5))

    for c, y in pairs(wqr_ref, _rope128, cs128, sn128):
        qr_ref[0, :, c * LANES:(c + 1) * LANES] = y.astype(BF16)
    for c, y in pairs(wkr_ref, _rope128, cs128, sn128):
        kr_ref[0, :, c * LANES:(c + 1) * LANES] = (y * (RET_HEAD_DIM ** -0.5)).astype(BF16)
    vr_ref[0] = jnp.dot(hb, wvr_ref[...], preferred_element_type=F32).astype(BF16)
    gr_ref[0] = jnp.dot(hb, wgr_ref[...], preferred_element_type=F32).astype(BF16)


def _inproj(x, shift, scale, g, weights, tables, ts):
    bsz, seq, _ = x.shape
    row = lambda b, i: (b, i, 0)
    per_b = lambda b, i: (b, 0, 0)
    const2 = lambda b, i: (0, 0)
    head_major = lambda b, i: (b, 0, i, 0)

    def wspec(w):
        return pl.BlockSpec(w.shape, const2)

    hm_shape = jax.ShapeDtypeStruct((bsz, ATT_HEADS, seq, LANES), BF16)
    hm_spec = pl.BlockSpec((1, ATT_HEADS, ts, LANES), head_major)
    pair_shape = jax.ShapeDtypeStruct((bsz, ATT_HEADS // 2, seq, LANES), BF16)
    pair_spec = pl.BlockSpec((1, ATT_HEADS // 2, ts, LANES), head_major)
    wide = lambda n, dt=BF16: jax.ShapeDtypeStruct((bsz, seq, n), dt)
    wspec_out = lambda n: pl.BlockSpec((1, ts, n), row)
    return pl.pallas_call(
        _inproj_kernel,
        grid=(bsz, seq // ts),
        in_specs=[
            pl.BlockSpec((1, ts, D_MODEL), row),
            pl.BlockSpec((1, 1, D_MODEL), per_b),
            pl.BlockSpec((1, 1, D_MODEL), per_b),
            pl.BlockSpec((1, D_MODEL), const2),
            *[wspec(w) for w in weights],
            *[pl.BlockSpec((1, ts, LANES), row)] * 4,
        ],
        out_specs=[hm_spec, pair_spec,
                   pl.BlockSpec((1, ATT_HEADS, ts // DSA_TK, V_ROWS, DSA_TK), lambda b, i: (b, 0, i, 0, 0)),
                   hm_spec, wspec_out(LANES),
                   pl.BlockSpec((1, IDX_HEADS, ts), lambda b, i: (b, 0, i)),
                   wspec_out(RET_WIDTH), wspec_out(RET_WIDTH), wspec_out(RET_WIDTH), wspec_out(RET_WIDTH)],
        out_shape=[hm_shape, pair_shape,
                   jax.ShapeDtypeStruct((bsz, ATT_HEADS, seq // DSA_TK, V_ROWS, DSA_TK), BF16),
                   hm_shape, wide(LANES),
                   jax.ShapeDtypeStruct((bsz, IDX_HEADS, seq), F32),
                   wide(RET_WIDTH), wide(RET_WIDTH), wide(RET_WIDTH), wide(RET_WIDTH)],
        compiler_params=_params("parallel", "parallel"),
        name="inproj",
    )(x, shift, scale, g, *weights, *tables)


def _key_to_float(u):
    key = u ^ INT_MIN
    bits = jnp.where(key >= 0, key, key ^ 0x7FFFFFFF)
    return lax.bitcast_convert_type(bits, F32)


def _dsa_kernel(qi_ref, ki_ref, wit_ref, qa_ref, ka_ref, vt_ref, o_ref,
                sc_ref, m_ref, acc_ref, kmax_ref, qit_ref, qat_ref, *, topk):
    tq, tk = DSA_TQ, DSA_TK
    qb = pl.program_id(1)
    nk = (qb + 1) * (tq // tk)
    grp = (tk // SUB, SUB, tq)
    q_pos = qb * tq + lax.broadcasted_iota(jnp.int32, (SUB, tq), 1)

    def key_index(kc):
        return (kc * tk + lax.broadcasted_iota(jnp.int32, grp, 0) * SUB
                + lax.broadcasted_iota(jnp.int32, grp, 1))

    for h in range(IDX_HEADS):
        qit_ref[h] = qi_ref[0, h].astype(F32).T.astype(BF16)
        qat_ref[h] = qa_ref[0, h].astype(F32).T.astype(BF16)

    wt = wit_ref[0]

    def score_chunk(kc, carry):
        k0 = pl.multiple_of(kc * tk, tk)
        kch = ki_ref[0, pl.ds(k0, tk), :]
        acc = jnp.zeros((tk, tq), F32)
        for h in range(IDX_HEADS):
            logit = jnp.dot(kch, qit_ref[h], preferred_element_type=F32)
            acc = acc + jnp.maximum(logit, 0.0) * wt[h:h + 1, :]
        sc_ref[kc] = jnp.where(key_index(kc) <= q_pos[None], acc.reshape(grp), NEG_INF)
        return carry

    lax.fori_loop(0, nk, score_chunk, 0)

    def count(pred):
        def body(kc, acc):
            hit = jnp.where(pred(sc_ref[kc], kc), 1.0, 0.0)
            parts = [hit[g] for g in range(grp[0])]
            while len(parts) > 1:
                parts = [parts[i] + parts[i + 1] for i in range(0, len(parts), 2)]
            return acc + parts[0]
        acc = lax.fori_loop(0, nk, body, jnp.zeros((SUB, tq), F32))
        return jnp.sum(acc, axis=0, keepdims=True)

    def rows(v):
        return jnp.broadcast_to(v, (SUB, tq))[None]

    few_keys = q_pos[:1] < topk - 1

    def bit_body(i, state):
        u, cnt_u = state
        trial = u | lax.shift_left(jnp.int32(1), 31 - i)
        cand = rows(_key_to_float(trial))
        cnt = count(lambda x, kc: x >= cand)
        accept = cnt >= topk
        return jnp.where(accept, trial, u), jnp.where(accept, cnt, cnt_u)

    u, cnt_u = lax.fori_loop(
        0, 32, bit_body, (jnp.zeros((1, tq), jnp.int32), jnp.zeros((1, tq), F32)))
    thr = rows(jnp.where(few_keys, F32_LOWEST, _key_to_float(u)))
    has_excess_ties = jnp.max(jnp.where(few_keys, 0.0, cnt_u)) > topk

    def tie_cut():
        need = topk - count(lambda x, kc: x > thr)

        def cut_body(i, c):
            trial = c | lax.shift_left(jnp.int32(1), 12 - i)
            tb = rows(trial)
            f = count(lambda x, kc: (x == thr) & (key_index(kc) < tb))
            return jnp.where(f <= need, trial, c)

        return lax.fori_loop(0, 13, cut_body, jnp.zeros((1, tq), jnp.int32))

    cut = rows(lax.cond(has_excess_ties, tie_cut,
                        lambda: jnp.full((1, tq), 2 ** 13 - 1, jnp.int32)))

    def mask_body(kc, c):
        x = sc_ref[kc]
        keep = (x > thr) | ((x == thr) & (key_index(kc) < cut))
        sc_ref[kc] = jnp.where(keep, 0.0, NEG_INF)
        return c

    lax.fori_loop(0, nk, mask_body, 0)

    def logits(kc, h):
        k0 = pl.multiple_of(kc * tk, tk)
        kch = ka_ref[0, h // 2, pl.ds(k0, tk), :]
        s = jnp.dot(kch, qat_ref[h], preferred_element_type=F32)
        return s.reshape(grp) + sc_ref[kc]

    def max_chunk(kc, m):
        return jnp.stack([jnp.maximum(m[h], jnp.max(logits(kc, h), axis=0))
                          for h in range(ATT_HEADS)])

    def pv_chunk(kc, carry):
        ss = [logits(kc, h) for h in range(ATT_HEADS)]
        for h in range(ATT_HEADS):
            p = jnp.exp(ss[h] - m_ref[h][None]).reshape(tk, tq).astype(BF16)
            acc_ref[h] += jnp.dot(vt_ref[0, h, kc], p, preferred_element_type=F32)
        return carry

    def exp_pv_pass():
        acc_ref[...] = jnp.zeros(acc_ref.shape, F32)
        lax.fori_loop(0, nk, pv_chunk, 0)

    ones_r = jnp.ones((SUB, LANES), BF16)
    lane_r = lax.broadcasted_iota(jnp.int32, (SUB, LANES), 1)

    @pl.when(qb == 0)
    def _():
        for h in range(ATT_HEADS):
            k = ka_ref[0, h // 2]
            head_lanes = jnp.where((lane_r >= 64) == (h % 2 == 1), 1.0, 0.0).astype(BF16)
            k_norm2 = lax.dot_general(head_lanes, k * k, _NT, preferred_element_type=F32)
            kmax_ref[h] = jnp.broadcast_to(jnp.max(k_norm2, axis=1, keepdims=True), (SUB, tq))

    for h in range(ATT_HEADS):
        q = qa_ref[0, h]
        q_norm2 = lax.dot_general(ones_r, q * q, _NT, preferred_element_type=F32)
        m_ref[h] = jnp.sqrt(q_norm2 * kmax_ref[h])
    exp_pv_pass()
    denom_min = jnp.min(jnp.stack([acc_ref[h][64:65] for h in range(ATT_HEADS)]))

    @pl.when(jnp.logical_not(denom_min > DENOM_FLOOR))
    def _():
        m8 = lax.fori_loop(0, nk, max_chunk, jnp.full((ATT_HEADS, SUB, tq), NEG_INF, F32))
        m_ref[...] = jnp.broadcast_to(jnp.max(m8, axis=1, keepdims=True), m8.shape)
        exp_pv_pass()

    outs = []
    for h in range(ATT_HEADS):
        acc = acc_ref[h]
        outs.append(acc[:64] / acc[64:65])
    o_ref[0] = jnp.concatenate(outs, axis=0).T.astype(BF16)


def _dsa(qi, ki, wit, qa, ka, vt):
    bsz, _, seq, _ = qi.shape
    tq, tk = DSA_TQ, DSA_TK
    topk = min(TOPK_MAX, seq // 4)
    assert seq % tq == 0 and tq % tk == 0 and tk >= topk
    blk_q = lambda b, i: (b, 0, i, 0)
    return pl.pallas_call(
        functools.partial(_dsa_kernel, topk=topk),
        grid=(bsz, seq // tq),
        in_specs=[
            pl.BlockSpec((1, IDX_HEADS, tq, LANES), blk_q),
            pl.BlockSpec((1, seq, LANES), lambda b, i: (b, 0, 0)),
            pl.BlockSpec((1, IDX_HEADS, tq), lambda b, i: (b, 0, i)),
            pl.BlockSpec((1, ATT_HEADS, tq, LANES), blk_q),
            pl.BlockSpec((1, ATT_HEADS // 2, seq, LANES), lambda b, i: (b, 0, 0, 0)),
            pl.BlockSpec((1, ATT_HEADS, seq // tk, V_ROWS, tk), lambda b, i: (b, 0, 0, 0, 0)),
        ],
        out_specs=pl.BlockSpec((1, tq, ATT_WIDTH), lambda b, i: (b, i, 0)),
        out_shape=jax.ShapeDtypeStruct((bsz, seq, ATT_WIDTH), BF16),
        scratch_shapes=[
            pltpu.VMEM((seq // tk, tk // SUB, SUB, tq), F32),
            pltpu.VMEM((ATT_HEADS, SUB, tq), F32),
            pltpu.VMEM((ATT_HEADS, V_ROWS, tq), F32),
            pltpu.VMEM((ATT_HEADS, SUB, tq), F32),
            pltpu.VMEM((IDX_HEADS, LANES, tq), BF16),
            pltpu.VMEM((ATT_HEADS, LANES, tq), BF16),
        ],
        compiler_params=_params("parallel", "arbitrary"),
        name="dsa",
    )(qi, ki, wit, qa, ka, vt)


RET_CHUNK = 256


def _ret_kernel(q_ref, k_ref, v_ref, g_ref, o_ref, state_ref, decay_ref):
    c = RET_CHUNK
    n = pl.program_id(1)
    row = lax.broadcasted_iota(jnp.int32, (c, c), 0)
    col = lax.broadcasted_iota(jnp.int32, (c, c), 1)
    idx = lax.broadcasted_iota(jnp.int32, (c, 1), 0).astype(F32)

    @pl.when(n == 0)
    def _():
        state_ref[...] = jnp.zeros_like(state_ref)
        for h in range(RET_HEADS):
            lg = math.log(1.0 - 2.0 ** (-5.0 - h))
            diff = (row - col).astype(F32)
            decay_ref[h] = jnp.where(row >= col, jnp.exp(lg * jnp.maximum(diff, 0.0)), 0.0)

    for h in range(RET_HEADS):
        lg = math.log(1.0 - 2.0 ** (-5.0 - h))
        sl = slice(h * LANES, (h + 1) * LANES)
        q = q_ref[0, :, sl]
        k = k_ref[0, :, sl]
        v = v_ref[0, :, sl]
        inner = lax.dot_general(q, k, _NT, preferred_element_type=F32) * decay_ref[h]
        o = jnp.dot(inner.astype(BF16), v, preferred_element_type=F32)
        qd = (q.astype(F32) * jnp.exp(lg * (idx + 1.0))).astype(BF16)
        state = state_ref[h]
        o = o + jnp.dot(qd, state.astype(BF16), preferred_element_type=F32)
        kd = (k.astype(F32) * jnp.exp(lg * (c - 1.0 - idx))).astype(BF16)
        kv = lax.dot_general(kd, v, _TN, preferred_element_type=F32)
        state_ref[h] = state * math.exp(lg * c) + kv
        gate = _silu(g_ref[0, :, sl].astype(F32))
        o_ref[0, :, sl] = (gate * _rms(o)).astype(BF16)


def _retention(qr, kr, vr, gr):
    bsz, seq, _ = qr.shape
    c = RET_CHUNK
    spec = pl.BlockSpec((1, c, RET_WIDTH), lambda b, i: (b, i, 0))
    return pl.pallas_call(
        _ret_kernel,
        grid=(bsz, seq // c),
        in_specs=[spec] * 4,
        out_specs=spec,
        out_shape=jax.ShapeDtypeStruct((bsz, seq, RET_WIDTH), BF16),
        scratch_shapes=[
            pltpu.VMEM((RET_HEADS, RET_HEAD_DIM, RET_HEAD_DIM), F32),
            pltpu.VMEM((RET_HEADS, c, c), F32),
        ],
        compiler_params=_params("arbitrary", "arbitrary"),
        name="ret",
    )(qr, kr, vr, gr)


ROUTER_E0 = N_GROUPS


def _mid_kernel(att_ref, ret_ref, x_ref, woa_ref, wor_ref, gpost_ref, gate1_ref,
                gpre_ref, scale2_ref, shift2_ref, wr_ref, br_ref,
                h1_ref, h2_ref, comb_ref):
    mix = jnp.dot(att_ref[0], woa_ref[...], preferred_element_type=F32)
    mix = mix + jnp.dot(ret_ref[0], wor_ref[...], preferred_element_type=F32)
    h1 = x_ref[0] + gate1_ref[0] * (_rms(mix) * gpost_ref[...])
    h1_ref[0] = h1
    h2 = (_rms(h1) * gpre_ref[...] * (1.0 + scale2_ref[0]) + shift2_ref[0]).astype(BF16)
    h2_ref[0] = h2

    logits = jnp.dot(h2, wr_ref[...], preferred_element_type=F32) + br_ref[...]
    lane = lax.broadcasted_iota(jnp.int32, logits.shape, 1)
    big = jnp.int32(LANES)

    def first_argmax(vals, vmax):
        return jnp.min(jnp.where(vals == vmax, lane, big), axis=1, keepdims=True)

    gl = jnp.where(lane < N_GROUPS, logits, NEG_INF)
    gexp = jnp.exp(gl - jnp.max(gl, axis=1, keepdims=True))
    gprob = gexp / jnp.sum(gexp, axis=1, keepdims=True)
    g_top = jnp.max(gprob, axis=1, keepdims=True)
    g_sel = first_argmax(gprob, g_top)
    e_lo = ROUTER_E0 + g_sel * EXPERTS_PER_GROUP
    in_group = (lane >= e_lo) & (lane < e_lo + EXPERTS_PER_GROUP)
    el = jnp.where(in_group, logits, NEG_INF)
    eexp = jnp.exp(el - jnp.max(el, axis=1, keepdims=True))
    eprob = jnp.where(in_group, eexp / jnp.sum(eexp, axis=1, keepdims=True), -1.0)
    top1 = jnp.max(eprob, axis=1, keepdims=True)
    idx1 = first_argmax(eprob, top1)
    rest = jnp.where(lane == idx1, -1.0, eprob)
    top2 = jnp.max(rest, axis=1, keepdims=True)
    idx2 = first_argmax(rest, top2)
    norm = g_top / (top1 + top2)
    comb_ref[0] = jnp.where(lane == idx1, top1 * norm, 0.0) + jnp.where(lane == idx2, top2 * norm, 0.0)


def _mid(att, ret, x, woa, wor, gpost, gate1, gpre, scale2, shift2, wr, br, ts):
    bsz, seq, _ = x.shape
    row = lambda b, i: (b, i, 0)
    per_b = lambda b, i: (b, 0, 0)
    const2 = lambda b, i: (0, 0)
    vec = pl.BlockSpec((1, D_MODEL), const2)
    bvec = pl.BlockSpec((1, 1, D_MODEL), per_b)
    return pl.pallas_call(
        _mid_kernel,
        grid=(bsz, seq // ts),
        in_specs=[
            pl.BlockSpec((1, ts, ATT_WIDTH), row),
            pl.BlockSpec((1, ts, RET_WIDTH), row),
            pl.BlockSpec((1, ts, D_MODEL), row),
            pl.BlockSpec(woa.shape, const2),
            pl.BlockSpec(wor.shape, const2),
            vec, bvec, vec, bvec, bvec,
            pl.BlockSpec(wr.shape, const2),
            pl.BlockSpec((1, LANES), const2),
        ],
        out_specs=[pl.BlockSpec((1, ts, D_MODEL), row),
                   pl.BlockSpec((1, ts, D_MODEL), row),
                   pl.BlockSpec((1, ts, LANES), row)],
        out_shape=[jax.ShapeDtypeStruct((bsz, seq, D_MODEL), F32),
                   jax.ShapeDtypeStruct((bsz, seq, D_MODEL), BF16),
                   jax.ShapeDtypeStruct((bsz, seq, LANES), F32)],
        compiler_params=_params("parallel", "parallel"),
        name="mid",
    )(att, ret, x, woa, wor, gpost, gate1, gpre, scale2, shift2, wr, br)


MOE_EPS = 8


def _moe_kernel(h2_ref, comb_ref, h1_ref, gate2_ref, gpost_ref, wgu_ref, wd_ref, o_ref, acc_ref):
    step = pl.program_id(2)

    @pl.when(step == 0)
    def _():
        acc_ref[...] = jnp.zeros_like(acc_ref)

    t = h2_ref[0]
    comb = comb_ref[0]
    lane = lax.broadcasted_iota(jnp.int32, comb.shape, 1)
    hids = []
    for j in range(MOE_EPS):
        gu = jnp.dot(t, wgu_ref[j], preferred_element_type=F32)
        hid = _silu(gu[:, :D_EXPERT]) * gu[:, D_EXPERT:]
        e_lane = ROUTER_E0 + step * MOE_EPS + j
        cw = jnp.sum(jnp.where(lane == e_lane, comb, 0.0), axis=1, keepdims=True)
        hids.append((hid * cw).astype(BF16))
    wd = wd_ref[...].reshape(MOE_EPS * D_EXPERT, D_MODEL)
    acc_ref[...] += jnp.dot(jnp.concatenate(hids, axis=1), wd, preferred_element_type=F32)

    @pl.when(step == N_EXPERTS // MOE_EPS - 1)
    def _():
        o_ref[0] = h1_ref[0] + gate2_ref[0] * (_rms(acc_ref[...]) * gpost_ref[...])


def _moe(h2, comb, h1, gate2, gpost, wgu, wd, tm):
    bsz, seq, _ = h2.shape
    row = lambda b, i, e: (b, i, 0)
    return pl.pallas_call(
        _moe_kernel,
        grid=(bsz, seq // tm, N_EXPERTS // MOE_EPS),
        in_specs=[
            pl.BlockSpec((1, tm, D_MODEL), row),
            pl.BlockSpec((1, tm, LANES), row),
            pl.BlockSpec((1, tm, D_MODEL), row),
            pl.BlockSpec((1, 1, D_MODEL), lambda b, i, e: (b, 0, 0)),
            pl.BlockSpec((1, D_MODEL), lambda b, i, e: (0, 0)),
            pl.BlockSpec((MOE_EPS, D_MODEL, 2 * D_EXPERT), lambda b, i, e: (e, 0, 0)),
            pl.BlockSpec((MOE_EPS, D_EXPERT, D_MODEL), lambda b, i, e: (e, 0, 0)),
        ],
        out_specs=pl.BlockSpec((1, tm, D_MODEL), row),
        out_shape=jax.ShapeDtypeStruct((bsz, seq, D_MODEL), F32),
        scratch_shapes=[pltpu.VMEM((tm, D_MODEL), F32)],
        compiler_params=_params("parallel", "parallel", "arbitrary"),
        name="moe",
    )(h2, comb, h1, gate2, gpost, wgu, wd)


def _layer(x, positions, mod, g_pre_mix, w_in, w_out, g_post_mix, g_pre_ffn,
           w_group, b_group, w_expert, b_expert, w_gate_exp, w_up_exp, w_down_exp, g_post_ffn):
    bsz, seq, _ = x.shape
    shift1, scale1, gate1, shift2, scale2, gate2 = [
        m.reshape(bsz, 1, D_MODEL) for m in jnp.split(mod, N_MOD, axis=-1)]
    vec = lambda g: g.reshape(1, D_MODEL)

    a, i8, r = ATT_WIDTH, IDX_HEADS, RET_WIDTH
    offs = np.cumsum([0, a, a, a, a, IDX_HEAD_DIM, i8, r, r, r, r])
    seg = lambda j: w_in[:, offs[j]:offs[j + 1]].astype(BF16)
    wki = jnp.concatenate([seg(4), seg(4), jnp.pad(seg(5), ((0, 0), (0, LANES - i8)))], axis=1)
    weights = [seg(0), seg(1), seg(2), seg(3), wki,
               seg(6), seg(7), seg(8), seg(9)]

    tables = _rope_tables(positions, 512)
    qa, ka, vt, qi, ki, wit, qr, kr, vr, gr = _inproj(
        x, shift1, scale1, vec(g_pre_mix), weights, tables, 512)
    att = _dsa(qi, ki, wit, qa, ka, vt)
    ret = _retention(qr, kr, vr, gr)

    w_router = jnp.pad(jnp.concatenate([w_group, w_expert], axis=1),
                       ((0, 0), (0, LANES - N_GROUPS - N_EXPERTS))).astype(BF16)
    b_router = jnp.pad(jnp.concatenate([b_group, b_expert]),
                       (0, LANES - N_GROUPS - N_EXPERTS)).reshape(1, LANES)
    h1, h2, comb = _mid(att, ret, x, w_out[:ATT_WIDTH].astype(BF16), w_out[ATT_WIDTH:].astype(BF16),
                        vec(g_post_mix), gate1, vec(g_pre_ffn), scale2, shift2, w_router, b_router, 512)

    wgu = jnp.concatenate([w_gate_exp, w_up_exp], axis=2).astype(BF16)
    return _moe(h2, comb, h1, gate2, vec(g_post_ffn), wgu, w_down_exp.astype(BF16), 1024)


def kernel(x, c, positions, g_pre_mix, w_ada, b_ada, w_in, w_out, g_post_mix, g_pre_ffn,
           w_group, b_group, w_expert, b_expert, w_gate_exp, w_up_exp, w_down_exp, g_post_ffn):
    h = x
    for l in range(w_in.shape[0]):
        mod = _ada(c, w_ada[l], b_ada[l])
        h = _layer(h, positions, mod, g_pre_mix[l], w_in[l], w_out[l], g_post_mix[l], g_pre_ffn[l],
                   w_group[l], b_group[l], w_expert[l], b_expert[l],
                   w_gate_exp[l], w_up_exp[l], w_down_exp[l], g_post_ffn[l])
    return h
```

```python
import functools
import math

import numpy as np
import jax
import jax.numpy as jnp
from jax import lax
from jax.experimental import pallas as pl
from jax.experimental.pallas import tpu as pltpu

D_MODEL = 1024
ATT_HEADS = 8
ATT_HEAD_DIM = 64
IDX_HEADS = 8
IDX_HEAD_DIM = 64
TOPK_MAX = 256
RET_HEADS = 4
RET_HEAD_DIM = 128
ROPE_THETA = 10000.0
ATT_WIDTH = ATT_HEADS * ATT_HEAD_DIM
RET_WIDTH = RET_HEADS * RET_HEAD_DIM
N_GROUPS = 4
EXPERTS_PER_GROUP = 8
N_EXPERTS = N_GROUPS * EXPERTS_PER_GROUP
D_EXPERT = 256
N_MOD = 6
EPS = 1e-6

LANES = 128
SUB = 8
VMEM_LIMIT = 56 * 1024 * 1024

DSA_TQ = 256
DSA_TK = 256
V_ROWS = 80
F32 = jnp.float32
BF16 = jnp.bfloat16
NEG_INF = float("-inf")
F32_LOWEST = float(np.finfo(np.float32).min)
INT_MIN = -2 ** 31
BF16_KEY_BITS = 16
DENOM_FLOOR = 1e-30

_NT = (((1,), (1,)), ((), ()))
_TN = (((0,), (0,)), ((), ()))


def _params(*sem):
    return pltpu.CompilerParams(dimension_semantics=sem, vmem_limit_bytes=VMEM_LIMIT)


def _rms(x):
    return x * lax.rsqrt(jnp.mean(x * x, axis=-1, keepdims=True) + EPS)


def _silu(x):
    return x * (1.0 / (1.0 + jnp.exp(-x)))


def _ada_kernel(c_ref, w_ref, b_ref, o_ref):
    a = _silu(c_ref[...]).astype(BF16)
    o_ref[...] = jnp.dot(a, w_ref[...].astype(BF16), preferred_element_type=F32) + b_ref[...]


def _ada(c, w, b):
    bsz = c.shape[0]
    n = w.shape[1]
    tn = D_MODEL
    return pl.pallas_call(
        _ada_kernel,
        grid=(n // tn,),
        in_specs=[
            pl.BlockSpec((bsz, D_MODEL), lambda j: (0, 0)),
            pl.BlockSpec((D_MODEL, tn), lambda j: (0, j)),
            pl.BlockSpec((1, tn), lambda j: (0, j)),
        ],
        out_specs=pl.BlockSpec((bsz, tn), lambda j: (0, j)),
        out_shape=jax.ShapeDtypeStruct((bsz, n), F32),
        compiler_params=_params("arbitrary"),
        name="ada",
    )(c, w, b.reshape(1, n))


def _tables_kernel(pos_ref, f_ref, cs64_ref, sn64_ref, cs128_ref, sn128_ref):
    ang = pos_ref[0].astype(F32) * f_ref[...]
    c = jnp.cos(ang)
    s = jnp.sin(ang)
    c_sw = pltpu.roll(c, 64, 1)
    s_sw = pltpu.roll(s, 64, 1)
    lane = lax.broadcasted_iota(jnp.int32, c.shape, 1)
    lo = lane < 64
    cs128_ref[0] = jnp.where(lo, c, c_sw)
    sn128_ref[0] = jnp.where(lo, -s, s_sw)
    cs64_ref[0] = jnp.where(lo, c_sw, c)
    s64 = jnp.where(lo, s_sw, s)
    sn64_ref[0] = jnp.where((lane % 64) < 32, -s64, s64)


def _rope_tables(positions, ts):
    bsz, seq = positions.shape
    f64 = ROPE_THETA ** (-jnp.arange(32, dtype=F32) / 32)
    f128 = ROPE_THETA ** (-jnp.arange(64, dtype=F32) / 64)
    frow = jnp.concatenate([f128, f64, f64]).reshape(1, LANES)
    tab = jax.ShapeDtypeStruct((bsz, seq, LANES), F32)
    tspec = pl.BlockSpec((1, ts, LANES), lambda b, i: (b, i, 0))
    return pl.pallas_call(
        _tables_kernel,
        grid=(bsz, seq // ts),
        in_specs=[
            pl.BlockSpec((1, ts, 1), lambda b, i: (b, i, 0)),
            pl.BlockSpec((1, LANES), lambda b, i: (0, 0)),
        ],
        out_specs=[tspec] * 4,
        out_shape=[tab] * 4,
        compiler_params=_params("parallel", "parallel"),
        name="tables",
    )(positions.reshape(bsz, seq, 1), frow)


def _rope64(y, cs, sn):
    lane = lax.broadcasted_iota(jnp.int32, y.shape, 1)
    rot = jnp.where((lane % 64) < 32, pltpu.roll(y, 96, 1), pltpu.roll(y, 32, 1))
    return y * cs + rot * sn


def _rope128(y, cs, sn):
    return y * cs + pltpu.roll(y, 64, 1) * sn


def _inproj_kernel(x_ref, shift_ref, scale_ref, g_ref,
                   wqa_ref, wka_ref, wv_ref, wqi_ref, wki_ref,
                   wqr_ref, wkr_ref, wvr_ref, wgr_ref,
                   cs64_ref, sn64_ref, cs128_ref, sn128_ref,
                   qa_ref, ka_ref, vt_ref, qi_ref, ki_ref, wit_ref,
                   qr_ref, kr_ref, vr_ref, gr_ref):
    x = x_ref[0]
    h = _rms(x) * g_ref[...] * (1.0 + scale_ref[0]) + shift_ref[0]
    hb = h.astype(BF16)
    cs64, sn64 = cs64_ref[0], sn64_ref[0]
    cs128, sn128 = cs128_ref[0], sn128_ref[0]

    lane = lax.broadcasted_iota(jnp.int32, (hb.shape[0], LANES), 1)
    wide = 2 * LANES

    def pairs(w_ref, rope, cs, sn):
        for c in range(w_ref.shape[1] // wide):
            y = jnp.dot(hb, w_ref[:, c * wide:(c + 1) * wide], preferred_element_type=F32)
            for j in range(wide // LANES):
                yield 2 * c + j, rope(y[:, j * LANES:(j + 1) * LANES], cs, sn)

    def query_heads(w_ref, o_ref, mult):
        for c, y in pairs(w_ref, _rope64, cs64, sn64):
            y = y * mult
            o_ref[0, 2 * c] = jnp.where(lane < 64, y, 0.0).astype(BF16)
            o_ref[0, 2 * c + 1] = jnp.where(lane >= 64, y, 0.0).astype(BF16)

    query_heads(wqa_ref, qa_ref, ATT_HEAD_DIM ** -0.5)
    query_heads(wqi_ref, qi_ref, 1.0)
    for c, y in pairs(wka_ref, _rope64, cs64, sn64):
        ka_ref[0, c] = y.astype(BF16)

    ts = hb.shape[0]
    vt = jnp.dot(hb, wv_ref[...], preferred_element_type=F32).T
    for h in range(ATT_HEADS):
        for j in range(ts // DSA_TK):
            blk = vt[h * 64:(h + 1) * 64, j * DSA_TK:(j + 1) * DSA_TK]
            vt_ref[0, h, j, :64, :] = blk.astype(BF16)
            vt_ref[0, h, j, 64:, :] = jnp.ones((V_ROWS - 64, DSA_TK), BF16)

    yk = jnp.dot(hb, wki_ref[...], preferred_element_type=F32)
    ki_ref[0] = _rope64(yk[:, :LANES], cs64, sn64).astype(BF16)
    wit_ref[0] = yk[:, LANES:].T[:IDX_HEADS] * ((IDX_HEADS ** -0.5) * (IDX_HEAD_DIM ** -0.5))

    for c, y in pairs(wqr_ref, _rope128, cs128, sn128):
        qr_ref[0, :, c * LANES:(c + 1) * LANES] = y.astype(BF16)
    for c, y in pairs(wkr_ref, _rope128, cs128, sn128):
        kr_ref[0, :, c * LANES:(c + 1) * LANES] = (y * (RET_HEAD_DIM ** -0.5)).astype(BF16)
    vr_ref[0] = jnp.dot(hb, wvr_ref[...], preferred_element_type=F32).astype(BF16)
    gr_ref[0] = jnp.dot(hb, wgr_ref[...], preferred_element_type=F32).astype(BF16)


def _inproj(x, shift, scale, g, weights, tables, ts):
    bsz, seq, _ = x.shape
    row = lambda b, i: (b, i, 0)
    per_b = lambda b, i: (b, 0, 0)
    const2 = lambda b, i: (0, 0)
    head_major = lambda b, i: (b, 0, i, 0)

    def wspec(w):
        return pl.BlockSpec(w.shape, const2)

    hm_shape = jax.ShapeDtypeStruct((bsz, ATT_HEADS, seq, LANES), BF16)
    hm_spec = pl.BlockSpec((1, ATT_HEADS, ts, LANES), head_major)
    pair_shape = jax.ShapeDtypeStruct((bsz, ATT_HEADS // 2, seq, LANES), BF16)
    pair_spec = pl.BlockSpec((1, ATT_HEADS // 2, ts, LANES), head_major)
    wide = lambda n, dt=BF16: jax.ShapeDtypeStruct((bsz, seq, n), dt)
    wspec_out = lambda n: pl.BlockSpec((1, ts, n), row)
    return pl.pallas_call(
        _inproj_kernel,
        grid=(bsz, seq // ts),
        in_specs=[
            pl.BlockSpec((1, ts, D_MODEL), row),
            pl.BlockSpec((1, 1, D_MODEL), per_b),
            pl.BlockSpec((1, 1, D_MODEL), per_b),
            pl.BlockSpec((1, D_MODEL), const2),
            *[wspec(w) for w in weights],
            *[pl.BlockSpec((1, ts, LANES), row)] * 4,
        ],
        out_specs=[hm_spec, pair_spec,
                   pl.BlockSpec((1, ATT_HEADS, ts // DSA_TK, V_ROWS, DSA_TK), lambda b, i: (b, 0, i, 0, 0)),
                   hm_spec, wspec_out(LANES),
                   pl.BlockSpec((1, IDX_HEADS, ts), lambda b, i: (b, 0, i)),
                   wspec_out(RET_WIDTH), wspec_out(RET_WIDTH), wspec_out(RET_WIDTH), wspec_out(RET_WIDTH)],
        out_shape=[hm_shape, pair_shape,
                   jax.ShapeDtypeStruct((bsz, ATT_HEADS, seq // DSA_TK, V_ROWS, DSA_TK), BF16),
                   hm_shape, wide(LANES),
                   jax.ShapeDtypeStruct((bsz, IDX_HEADS, seq), F32),
                   wide(RET_WIDTH), wide(RET_WIDTH), wide(RET_WIDTH), wide(RET_WIDTH)],
        compiler_params=_params("parallel", "parallel"),
        name="inproj",
    )(x, shift, scale, g, *weights, *tables)


def _key_to_float(u):
    key = u ^ INT_MIN
    bits = jnp.where(key >= 0, key, key ^ 0x7FFFFFFF)
    return lax.bitcast_convert_type(bits, F32)


def _dsa_kernel(qi_ref, ki_ref, wit_ref, qa_ref, ka_ref, vt_ref, o_ref,
                sc_ref, sb_ref, m_ref, acc_ref, kmax_ref, qit_ref, qat_ref, *, topk):
    tq, tk = DSA_TQ, DSA_TK
    qb = pl.program_id(1)
    nk = (qb + 1) * (tq // tk)
    grp = (tk // SUB, SUB, tq)
    grp16 = (tk // (2 * SUB), 2 * SUB, tq)
    q_pos = qb * tq + lax.broadcasted_iota(jnp.int32, (SUB, tq), 1)

    def key_index(kc):
        return (kc * tk + lax.broadcasted_iota(jnp.int32, grp, 0) * SUB
                + lax.broadcasted_iota(jnp.int32, grp, 1))

    for h in range(IDX_HEADS):
        qit_ref[h] = qi_ref[0, h].astype(F32).T.astype(BF16)
        qat_ref[h] = qa_ref[0, h].astype(F32).T.astype(BF16)

    wt = wit_ref[0]

    def score_chunk(kc, carry):
        k0 = pl.multiple_of(kc * tk, tk)
        kch = ki_ref[0, pl.ds(k0, tk), :]
        acc = jnp.zeros((tk, tq), F32)
        for h in range(IDX_HEADS):
            logit = jnp.dot(kch, qit_ref[h], preferred_element_type=F32)
            acc = acc + jnp.maximum(logit, 0.0) * wt[h:h + 1, :]
        score = jnp.where(key_index(kc) <= q_pos[None], acc.reshape(grp), NEG_INF)
        sc_ref[kc] = score
        sb_ref[kc] = score.reshape(tk, tq).astype(BF16).reshape(grp16)
        return carry

    lax.fori_loop(0, nk, score_chunk, 0)

    def count(pred):
        def body(kc, acc):
            hit = jnp.where(pred(sc_ref[kc], kc), 1.0, 0.0)
            parts = [hit[g] for g in range(grp[0])]
            while len(parts) > 1:
                parts = [parts[i] + parts[i + 1] for i in range(0, len(parts), 2)]
            return acc + parts[0]
        acc = lax.fori_loop(0, nk, body, jnp.zeros((SUB, tq), F32))
        return jnp.sum(acc, axis=0, keepdims=True)

    def rows(v):
        return jnp.broadcast_to(v, (SUB, tq))[None]

    few_keys = q_pos[:1] < topk - 1

    def count_bf16(cand):
        cb = jnp.broadcast_to(cand, (2 * SUB, tq)).astype(BF16)[None]

        def body(kc, acc):
            hit = jnp.where(sb_ref[kc] >= cb, jnp.ones((), BF16), jnp.zeros((), BF16))
            parts = [hit[g] for g in range(grp16[0])]
            while len(parts) > 1:
                parts = [parts[i] + parts[i + 1] for i in range(0, len(parts), 2)]
            return acc + parts[0].astype(F32)
        acc = lax.fori_loop(0, nk, body, jnp.zeros((2 * SUB, tq), F32))
        return jnp.sum(acc, axis=0, keepdims=True)

    def high_body(i, u):
        trial = u | lax.shift_left(jnp.int32(1), 31 - i)
        return jnp.where(count_bf16(_key_to_float(trial)) >= topk, trial, u)

    u_hi = lax.fori_loop(0, BF16_KEY_BITS, high_body, jnp.zeros((1, tq), jnp.int32))
    base = u_hi - 2 ** (32 - BF16_KEY_BITS)

    def low_body(i, state):
        d, cnt_u = state
        trial = d | lax.shift_left(jnp.int32(1), 32 - BF16_KEY_BITS - i)
        cand = rows(_key_to_float(base + trial))
        cnt = count(lambda x, kc: x >= cand)
        accept = cnt >= topk
        return jnp.where(accept, trial, d), jnp.where(accept, cnt, cnt_u)

    d, cnt_u = lax.fori_loop(
        0, 32 - BF16_KEY_BITS + 1, low_body,
        (jnp.zeros((1, tq), jnp.int32), jnp.full((1, tq), 2.0 * topk, F32)))
    u = base + d
    thr = rows(jnp.where(few_keys, F32_LOWEST, _key_to_float(u)))
    has_excess_ties = jnp.max(jnp.where(few_keys, 0.0, cnt_u)) > topk

    def tie_cut():
        need = topk - count(lambda x, kc: x > thr)

        def cut_body(i, c):
            trial = c | lax.shift_left(jnp.int32(1), 12 - i)
            tb = rows(trial)
            f = count(lambda x, kc: (x == thr) & (key_index(kc) < tb))
            return jnp.where(f <= need, trial, c)

        return lax.fori_loop(0, 13, cut_body, jnp.zeros((1, tq), jnp.int32))

    cut = rows(lax.cond(has_excess_ties, tie_cut,
                        lambda: jnp.full((1, tq), 2 ** 13 - 1, jnp.int32)))

    def mask_body(kc, c):
        x = sc_ref[kc]
        keep = (x > thr) | ((x == thr) & (key_index(kc) < cut))
        sc_ref[kc] = jnp.where(keep, 0.0, NEG_INF)
        return c

    lax.fori_loop(0, nk, mask_body, 0)

    def logits(kc, h):
        k0 = pl.multiple_of(kc * tk, tk)
        kch = ka_ref[0, h // 2, pl.ds(k0, tk), :]
        s = jnp.dot(kch, qat_ref[h], preferred_element_type=F32)
        return s.reshape(grp) + sc_ref[kc]

    def max_chunk(kc, m):
        return jnp.stack([jnp.maximum(m[h], jnp.max(logits(kc, h), axis=0))
                          for h in range(ATT_HEADS)])

    def pv_chunk(kc, carry):
        ss = [logits(kc, h) for h in range(ATT_HEADS)]
        for h in range(ATT_HEADS):
            p = jnp.exp(ss[h] - m_ref[h][None]).reshape(tk, tq).astype(BF16)
            acc_ref[h] += jnp.dot(vt_ref[0, h, kc], p, preferred_element_type=F32)
        return carry

    def exp_pv_pass():
        acc_ref[...] = jnp.zeros(acc_ref.shape, F32)
        lax.fori_loop(0, nk, pv_chunk, 0)

    ones_r = jnp.ones((SUB, LANES), BF16)
    lane_r = lax.broadcasted_iota(jnp.int32, (SUB, LANES), 1)

    @pl.when(qb == 0)
    def _():
        for h in range(ATT_HEADS):
            k = ka_ref[0, h // 2]
            head_lanes = jnp.where((lane_r >= 64) == (h % 2 == 1), 1.0, 0.0).astype(BF16)
            k_norm2 = lax.dot_general(head_lanes, k * k, _NT, preferred_element_type=F32)
            kmax_ref[h] = jnp.broadcast_to(jnp.max(k_norm2, axis=1, keepdims=True), (SUB, tq))

    for h in range(ATT_HEADS):
        q = qa_ref[0, h]
        q_norm2 = lax.dot_general(ones_r, q * q, _NT, preferred_element_type=F32)
        m_ref[h] = jnp.sqrt(q_norm2 * kmax_ref[h])
    exp_pv_pass()
    denom_min = jnp.min(jnp.stack([acc_ref[h][64:65] for h in range(ATT_HEADS)]))

    @pl.when(jnp.logical_not(denom_min > DENOM_FLOOR))
    def _():
        m8 = lax.fori_loop(0, nk, max_chunk, jnp.full((ATT_HEADS, SUB, tq), NEG_INF, F32))
        m_ref[...] = jnp.broadcast_to(jnp.max(m8, axis=1, keepdims=True), m8.shape)
        exp_pv_pass()

    outs = []
    for h in range(ATT_HEADS):
        acc = acc_ref[h]
        outs.append(acc[:64] / acc[64:65])
    o_ref[0] = jnp.concatenate(outs, axis=0).T.astype(BF16)


def _dsa(qi, ki, wit, qa, ka, vt):
    bsz, _, seq, _ = qi.shape
    tq, tk = DSA_TQ, DSA_TK
    topk = min(TOPK_MAX, seq // 4)
    assert seq % tq == 0 and tq % tk == 0 and tk >= topk
    blk_q = lambda b, i: (b, 0, i, 0)
    return pl.pallas_call(
        functools.partial(_dsa_kernel, topk=topk),
        grid=(bsz, seq // tq),
        in_specs=[
            pl.BlockSpec((1, IDX_HEADS, tq, LANES), blk_q),
            pl.BlockSpec((1, seq, LANES), lambda b, i: (b, 0, 0)),
            pl.BlockSpec((1, IDX_HEADS, tq), lambda b, i: (b, 0, i)),
            pl.BlockSpec((1, ATT_HEADS, tq, LANES), blk_q),
            pl.BlockSpec((1, ATT_HEADS // 2, seq, LANES), lambda b, i: (b, 0, 0, 0)),
            pl.BlockSpec((1, ATT_HEADS, seq // tk, V_ROWS, tk), lambda b, i: (b, 0, 0, 0, 0)),
        ],
        out_specs=pl.BlockSpec((1, tq, ATT_WIDTH), lambda b, i: (b, i, 0)),
        out_shape=jax.ShapeDtypeStruct((bsz, seq, ATT_WIDTH), BF16),
        scratch_shapes=[
            pltpu.VMEM((seq // tk, tk // SUB, SUB, tq), F32),
            pltpu.VMEM((seq // tk, tk // (2 * SUB), 2 * SUB, tq), BF16),
            pltpu.VMEM((ATT_HEADS, SUB, tq), F32),
            pltpu.VMEM((ATT_HEADS, V_ROWS, tq), F32),
            pltpu.VMEM((ATT_HEADS, SUB, tq), F32),
            pltpu.VMEM((IDX_HEADS, LANES, tq), BF16),
            pltpu.VMEM((ATT_HEADS, LANES, tq), BF16),
        ],
        compiler_params=_params("parallel", "arbitrary"),
        name="dsa",
    )(qi, ki, wit, qa, ka, vt)


RET_CHUNK = 256


def _ret_kernel(q_ref, k_ref, v_ref, g_ref, o_ref, state_ref, decay_ref):
    c = RET_CHUNK
    n = pl.program_id(1)
    row = lax.broadcasted_iota(jnp.int32, (c, c), 0)
    col = lax.broadcasted_iota(jnp.int32, (c, c), 1)
    idx = lax.broadcasted_iota(jnp.int32, (c, 1), 0).astype(F32)

    @pl.when(n == 0)
    def _():
        state_ref[...] = jnp.zeros_like(state_ref)
        for h in range(RET_HEADS):
            lg = math.log(1.0 - 2.0 ** (-5.0 - h))
            diff = (row - col).astype(F32)
            decay_ref[h] = jnp.where(row >= col, jnp.exp(lg * jnp.maximum(diff, 0.0)), 0.0)

    for h in range(RET_HEADS):
        lg = math.log(1.0 - 2.0 ** (-5.0 - h))
        sl = slice(h * LANES, (h + 1) * LANES)
        q = q_ref[0, :, sl]
        k = k_ref[0, :, sl]
        v = v_ref[0, :, sl]
        inner = lax.dot_general(q, k, _NT, preferred_element_type=F32) * decay_ref[h]
        o = jnp.dot(inner.astype(BF16), v, preferred_element_type=F32)
        qd = (q.astype(F32) * jnp.exp(lg * (idx + 1.0))).astype(BF16)
        state = state_ref[h]
        o = o + jnp.dot(qd, state.astype(BF16), preferred_element_type=F32)
        kd = (k.astype(F32) * jnp.exp(lg * (c - 1.0 - idx))).astype(BF16)
        kv = lax.dot_general(kd, v, _TN, preferred_element_type=F32)
        state_ref[h] = state * math.exp(lg * c) + kv
        gate = _silu(g_ref[0, :, sl].astype(F32))
        o_ref[0, :, sl] = (gate * _rms(o)).astype(BF16)


def _retention(qr, kr, vr, gr):
    bsz, seq, _ = qr.shape
    c = RET_CHUNK
    spec = pl.BlockSpec((1, c, RET_WIDTH), lambda b, i: (b, i, 0))
    return pl.pallas_call(
        _ret_kernel,
        grid=(bsz, seq // c),
        in_specs=[spec] * 4,
        out_specs=spec,
        out_shape=jax.ShapeDtypeStruct((bsz, seq, RET_WIDTH), BF16),
        scratch_shapes=[
            pltpu.VMEM((RET_HEADS, RET_HEAD_DIM, RET_HEAD_DIM), F32),
            pltpu.VMEM((RET_HEADS, c, c), F32),
        ],
        compiler_params=_params("arbitrary", "arbitrary"),
        name="ret",
    )(qr, kr, vr, gr)


ROUTER_E0 = N_GROUPS


def _mid_kernel(att_ref, ret_ref, x_ref, woa_ref, wor_ref, gpost_ref, gate1_ref,
                gpre_ref, scale2_ref, shift2_ref, wr_ref, br_ref,
                h1_ref, h2_ref, comb_ref):
    mix = jnp.dot(att_ref[0], woa_ref[...], preferred_element_type=F32)
    mix = mix + jnp.dot(ret_ref[0], wor_ref[...], preferred_element_type=F32)
    h1 = x_ref[0] + gate1_ref[0] * (_rms(mix) * gpost_ref[...])
    h1_ref[0] = h1
    h2 = (_rms(h1) * gpre_ref[...] * (1.0 + scale2_ref[0]) + shift2_ref[0]).astype(BF16)
    h2_ref[0] = h2

    logits = jnp.dot(h2, wr_ref[...], preferred_element_type=F32) + br_ref[...]
    lane = lax.broadcasted_iota(jnp.int32, logits.shape, 1)
    big = jnp.int32(LANES)

    def first_argmax(vals, vmax):
        return jnp.min(jnp.where(vals == vmax, lane, big), axis=1, keepdims=True)

    gl = jnp.where(lane < N_GROUPS, logits, NEG_INF)
    gexp = jnp.exp(gl - jnp.max(gl, axis=1, keepdims=True))
    gprob = gexp / jnp.sum(gexp, axis=1, keepdims=True)
    g_top = jnp.max(gprob, axis=1, keepdims=True)
    g_sel = first_argmax(gprob, g_top)
    e_lo = ROUTER_E0 + g_sel * EXPERTS_PER_GROUP
    in_group = (lane >= e_lo) & (lane < e_lo + EXPERTS_PER_GROUP)
    el = jnp.where(in_group, logits, NEG_INF)
    eexp = jnp.exp(el - jnp.max(el, axis=1, keepdims=True))
    eprob = jnp.where(in_group, eexp / jnp.sum(eexp, axis=1, keepdims=True), -1.0)
    top1 = jnp.max(eprob, axis=1, keepdims=True)
    idx1 = first_argmax(eprob, top1)
    rest = jnp.where(lane == idx1, -1.0, eprob)
    top2 = jnp.max(rest, axis=1, keepdims=True)
    idx2 = first_argmax(rest, top2)
    norm = g_top / (top1 + top2)
    comb_ref[0] = jnp.where(lane == idx1, top1 * norm, 0.0) + jnp.where(lane == idx2, top2 * norm, 0.0)


def _mid(att, ret, x, woa, wor, gpost, gate1, gpre, scale2, shift2, wr, br, ts):
    bsz, seq, _ = x.shape
    row = lambda b, i: (b, i, 0)
    per_b = lambda b, i: (b, 0, 0)
    const2 = lambda b, i: (0, 0)
    vec = pl.BlockSpec((1, D_MODEL), const2)
    bvec = pl.BlockSpec((1, 1, D_MODEL), per_b)
    return pl.pallas_call(
        _mid_kernel,
        grid=(bsz, seq // ts),
        in_specs=[
            pl.BlockSpec((1, ts, ATT_WIDTH), row),
            pl.BlockSpec((1, ts, RET_WIDTH), row),
            pl.BlockSpec((1, ts, D_MODEL), row),
            pl.BlockSpec(woa.shape, const2),
            pl.BlockSpec(wor.shape, const2),
            vec, bvec, vec, bvec, bvec,
            pl.BlockSpec(wr.shape, const2),
            pl.BlockSpec((1, LANES), const2),
        ],
        out_specs=[pl.BlockSpec((1, ts, D_MODEL), row),
                   pl.BlockSpec((1, ts, D_MODEL), row),
                   pl.BlockSpec((1, ts, LANES), row)],
        out_shape=[jax.ShapeDtypeStruct((bsz, seq, D_MODEL), F32),
                   jax.ShapeDtypeStruct((bsz, seq, D_MODEL), BF16),
                   jax.ShapeDtypeStruct((bsz, seq, LANES), F32)],
        compiler_params=_params("parallel", "parallel"),
        name="mid",
    )(att, ret, x, woa, wor, gpost, gate1, gpre, scale2, shift2, wr, br)


MOE_EPS = 8


def _moe_kernel(h2_ref, comb_ref, h1_ref, gate2_ref, gpost_ref, wgu_ref, wd_ref, o_ref, acc_ref):
    step = pl.program_id(2)

    @pl.when(step == 0)
    def _():
        acc_ref[...] = jnp.zeros_like(acc_ref)

    t = h2_ref[0]
    comb = comb_ref[0]
    lane = lax.broadcasted_iota(jnp.int32, comb.shape, 1)
    hids = []
    for j in range(MOE_EPS):
        gu = jnp.dot(t, wgu_ref[j], preferred_element_type=F32)
        hid = _silu(gu[:, :D_EXPERT]) * gu[:, D_EXPERT:]
        e_lane = ROUTER_E0 + step * MOE_EPS + j
        cw = jnp.sum(jnp.where(lane == e_lane, comb, 0.0), axis=1, keepdims=True)
        hids.append((hid * cw).astype(BF16))
    wd = wd_ref[...].reshape(MOE_EPS * D_EXPERT, D_MODEL)
    acc_ref[...] += jnp.dot(jnp.concatenate(hids, axis=1), wd, preferred_element_type=F32)

    @pl.when(step == N_EXPERTS // MOE_EPS - 1)
    def _():
        o_ref[0] = h1_ref[0] + gate2_ref[0] * (_rms(acc_ref[...]) * gpost_ref[...])


def _moe(h2, comb, h1, gate2, gpost, wgu, wd, tm):
    bsz, seq, _ = h2.shape
    row = lambda b, i, e: (b, i, 0)
    return pl.pallas_call(
        _moe_kernel,
        grid=(bsz, seq // tm, N_EXPERTS // MOE_EPS),
        in_specs=[
            pl.BlockSpec((1, tm, D_MODEL), row),
            pl.BlockSpec((1, tm, LANES), row),
            pl.BlockSpec((1, tm, D_MODEL), row),
            pl.BlockSpec((1, 1, D_MODEL), lambda b, i, e: (b, 0, 0)),
            pl.BlockSpec((1, D_MODEL), lambda b, i, e: (0, 0)),
            pl.BlockSpec((MOE_EPS, D_MODEL, 2 * D_EXPERT), lambda b, i, e: (e, 0, 0)),
            pl.BlockSpec((MOE_EPS, D_EXPERT, D_MODEL), lambda b, i, e: (e, 0, 0)),
        ],
        out_specs=pl.BlockSpec((1, tm, D_MODEL), row),
        out_shape=jax.ShapeDtypeStruct((bsz, seq, D_MODEL), F32),
        scratch_shapes=[pltpu.VMEM((tm, D_MODEL), F32)],
        compiler_params=_params("parallel", "parallel", "arbitrary"),
        name="moe",
    )(h2, comb, h1, gate2, gpost, wgu, wd)


def _layer(x, positions, mod, g_pre_mix, w_in, w_out, g_post_mix, g_pre_ffn,
           w_group, b_group, w_expert, b_expert, w_gate_exp, w_up_exp, w_down_exp, g_post_ffn):
    bsz, seq, _ = x.shape
    shift1, scale1, gate1, shift2, scale2, gate2 = [
        m.reshape(bsz, 1, D_MODEL) for m in jnp.split(mod, N_MOD, axis=-1)]
    vec = lambda g: g.reshape(1, D_MODEL)

    a, i8, r = ATT_WIDTH, IDX_HEADS, RET_WIDTH
    offs = np.cumsum([0, a, a, a, a, IDX_HEAD_DIM, i8, r, r, r, r])
    seg = lambda j: w_in[:, offs[j]:offs[j + 1]].astype(BF16)
    wki = jnp.concatenate([seg(4), seg(4), jnp.pad(seg(5), ((0, 0), (0, LANES - i8)))], axis=1)
    weights = [seg(0), seg(1), seg(2), seg(3), wki,
               seg(6), seg(7), seg(8), seg(9)]

    tables = _rope_tables(positions, 512)
    qa, ka, vt, qi, ki, wit, qr, kr, vr, gr = _inproj(
        x, shift1, scale1, vec(g_pre_mix), weights, tables, 512)
    att = _dsa(qi, ki, wit, qa, ka, vt)
    ret = _retention(qr, kr, vr, gr)

    w_router = jnp.pad(jnp.concatenate([w_group, w_expert], axis=1),
                       ((0, 0), (0, LANES - N_GROUPS - N_EXPERTS))).astype(BF16)
    b_router = jnp.pad(jnp.concatenate([b_group, b_expert]),
                       (0, LANES - N_GROUPS - N_EXPERTS)).reshape(1, LANES)
    h1, h2, comb = _mid(att, ret, x, w_out[:ATT_WIDTH].astype(BF16), w_out[ATT_WIDTH:].astype(BF16),
                        vec(g_post_mix), gate1, vec(g_pre_ffn), scale2, shift2, w_router, b_router, 512)

    wgu = jnp.concatenate([w_gate_exp, w_up_exp], axis=2).astype(BF16)
    return _moe(h2, comb, h1, gate2, vec(g_post_ffn), wgu, w_down_exp.astype(BF16), 1024)


def kernel(x, c, positions, g_pre_mix, w_ada, b_ada, w_in, w_out, g_post_mix, g_pre_ffn,
           w_group, b_group, w_expert, b_expert, w_gate_exp, w_up_exp, w_down_exp, g_post_ffn):
    h = x
    for l in range(w_in.shape[0]):
        mod = _ada(c, w_ada[l], b_ada[l])
        h = _layer(h, positions, mod, g_pre_mix[l], w_in[l], w_out[l], g_post_mix[l], g_pre_ffn[l],
                   w_group[l], b_group[l], w_expert[l], b_expert[l],
                   w_gate_exp[l], w_up_exp[l], w_down_exp[l], g_post_ffn[l])
    return h
```

```python
import functools
import math

import numpy as np
import jax
import jax.numpy as jnp
from jax import lax
from jax.experimental import pallas as pl
from jax.experimental.pallas import tpu as pltpu

D_MODEL = 1024
ATT_HEADS = 8
ATT_HEAD_DIM = 64
IDX_HEADS = 8
IDX_HEAD_DIM = 64
TOPK_MAX = 256
RET_HEADS = 4
RET_HEAD_DIM = 128
ROPE_THETA = 10000.0
ATT_WIDTH = ATT_HEADS * ATT_HEAD_DIM
RET_WIDTH = RET_HEADS * RET_HEAD_DIM
N_GROUPS = 4
EXPERTS_PER_GROUP = 8
N_EXPERTS = N_GROUPS * EXPERTS_PER_GROUP
D_EXPERT = 256
N_MOD = 6
EPS = 1e-6

LANES = 128
SUB = 8
VMEM_LIMIT = 56 * 1024 * 1024

DSA_TQ = 256
DSA_TK = 256
V_ROWS = 80
F32 = jnp.float32
BF16 = jnp.bfloat16
NEG_INF = float("-inf")
F32_LOWEST = float(np.finfo(np.float32).min)
INT_MIN = -2 ** 31
BF16_KEY_BITS = 16
NO_CHUNK = 2 ** 30
DENOM_FLOOR = 1e-30

_NT = (((1,), (1,)), ((), ()))
_TN = (((0,), (0,)), ((), ()))


def _params(*sem):
    return pltpu.CompilerParams(dimension_semantics=sem, vmem_limit_bytes=VMEM_LIMIT)


def _rms(x):
    return x * lax.rsqrt(jnp.mean(x * x, axis=-1, keepdims=True) + EPS)


def _silu(x):
    return x * (1.0 / (1.0 + jnp.exp(-x)))


def _ada_kernel(c_ref, w_ref, b_ref, o_ref):
    a = _silu(c_ref[...]).astype(BF16)
    o_ref[...] = jnp.dot(a, w_ref[...].astype(BF16), preferred_element_type=F32) + b_ref[...]


def _ada(c, w, b):
    bsz = c.shape[0]
    n = w.shape[1]
    tn = D_MODEL
    return pl.pallas_call(
        _ada_kernel,
        grid=(n // tn,),
        in_specs=[
            pl.BlockSpec((bsz, D_MODEL), lambda j: (0, 0)),
            pl.BlockSpec((D_MODEL, tn), lambda j: (0, j)),
            pl.BlockSpec((1, tn), lambda j: (0, j)),
        ],
        out_specs=pl.BlockSpec((bsz, tn), lambda j: (0, j)),
        out_shape=jax.ShapeDtypeStruct((bsz, n), F32),
        compiler_params=_params("arbitrary"),
        name="ada",
    )(c, w, b.reshape(1, n))


def _tables_kernel(pos_ref, f_ref, cs64_ref, sn64_ref, cs128_ref, sn128_ref):
    ang = pos_ref[0].astype(F32) * f_ref[...]
    c = jnp.cos(ang)
    s = jnp.sin(ang)
    c_sw = pltpu.roll(c, 64, 1)
    s_sw = pltpu.roll(s, 64, 1)
    lane = lax.broadcasted_iota(jnp.int32, c.shape, 1)
    lo = lane < 64
    cs128_ref[0] = jnp.where(lo, c, c_sw)
    sn128_ref[0] = jnp.where(lo, -s, s_sw)
    cs64_ref[0] = jnp.where(lo, c_sw, c)
    s64 = jnp.where(lo, s_sw, s)
    sn64_ref[0] = jnp.where((lane % 64) < 32, -s64, s64)


def _rope_tables(positions, ts):
    bsz, seq = positions.shape
    f64 = ROPE_THETA ** (-jnp.arange(32, dtype=F32) / 32)
    f128 = ROPE_THETA ** (-jnp.arange(64, dtype=F32) / 64)
    frow = jnp.concatenate([f128, f64, f64]).reshape(1, LANES)
    tab = jax.ShapeDtypeStruct((bsz, seq, LANES), F32)
    tspec = pl.BlockSpec((1, ts, LANES), lambda b, i: (b, i, 0))
    return pl.pallas_call(
        _tables_kernel,
        grid=(bsz, seq // ts),
        in_specs=[
            pl.BlockSpec((1, ts, 1), lambda b, i: (b, i, 0)),
            pl.BlockSpec((1, LANES), lambda b, i: (0, 0)),
        ],
        out_specs=[tspec] * 4,
        out_shape=[tab] * 4,
        compiler_params=_params("parallel", "parallel"),
        name="tables",
    )(positions.reshape(bsz, seq, 1), frow)


def _rope64(y, cs, sn):
    lane = lax.broadcasted_iota(jnp.int32, y.shape, 1)
    rot = jnp.where((lane % 64) < 32, pltpu.roll(y, 96, 1), pltpu.roll(y, 32, 1))
    return y * cs + rot * sn


def _rope128(y, cs, sn):
    return y * cs + pltpu.roll(y, 64, 1) * sn


def _inproj_kernel(x_ref, shift_ref, scale_ref, g_ref,
                   wqa_ref, wka_ref, wv_ref, wqi_ref, wki_ref,
                   wqr_ref, wkr_ref, wvr_ref, wgr_ref,
                   cs64_ref, sn64_ref, cs128_ref, sn128_ref,
                   qa_ref, ka_ref, vt_ref, qi_ref, ki_ref, wit_ref,
                   qr_ref, kr_ref, vr_ref, gr_ref):
    x = x_ref[0]
    h = _rms(x) * g_ref[...] * (1.0 + scale_ref[0]) + shift_ref[0]
    hb = h.astype(BF16)
    cs64, sn64 = cs64_ref[0], sn64_ref[0]
    cs128, sn128 = cs128_ref[0], sn128_ref[0]

    lane = lax.broadcasted_iota(jnp.int32, (hb.shape[0], LANES), 1)
    wide = 2 * LANES

    def pairs(w_ref, rope, cs, sn):
        for c in range(w_ref.shape[1] // wide):
            y = jnp.dot(hb, w_ref[:, c * wide:(c + 1) * wide], preferred_element_type=F32)
            for j in range(wide // LANES):
                yield 2 * c + j, rope(y[:, j * LANES:(j + 1) * LANES], cs, sn)

    def query_heads(w_ref, o_ref, mult):
        for c, y in pairs(w_ref, _rope64, cs64, sn64):
            y = y * mult
            o_ref[0, 2 * c] = jnp.where(lane < 64, y, 0.0).astype(BF16)
            o_ref[0, 2 * c + 1] = jnp.where(lane >= 64, y, 0.0).astype(BF16)

    query_heads(wqa_ref, qa_ref, ATT_HEAD_DIM ** -0.5)
    query_heads(wqi_ref, qi_ref, 1.0)
    for c, y in pairs(wka_ref, _rope64, cs64, sn64):
        ka_ref[0, c] = y.astype(BF16)

    ts = hb.shape[0]
    vt = jnp.dot(hb, wv_ref[...], preferred_element_type=F32).T
    for h in range(ATT_HEADS):
        for j in range(ts // DSA_TK):
            blk = vt[h * 64:(h + 1) * 64, j * DSA_TK:(j + 1) * DSA_TK]
            vt_ref[0, h, j, :64, :] = blk.astype(BF16)
            vt_ref[0, h, j, 64:, :] = jnp.ones((V_ROWS - 64, DSA_TK), BF16)

    yk = jnp.dot(hb, wki_ref[...], preferred_element_type=F32)
    ki_ref[0] = _rope64(yk[:, :LANES], cs64, sn64).astype(BF16)
    wit_ref[0] = yk[:, LANES:].T[:IDX_HEADS] * ((IDX_HEADS ** -0.5) * (IDX_HEAD_DIM ** -0.5))

    for c, y in pairs(wqr_ref, _rope128, cs128, sn128):
        qr_ref[0, :, c * LANES:(c + 1) * LANES] = y.astype(BF16)
    for c, y in pairs(wkr_ref, _rope128, cs128, sn128):
        kr_ref[0, :, c * LANES:(c + 1) * LANES] = (y * (RET_HEAD_DIM ** -0.5)).astype(BF16)
    vr_ref[0] = jnp.dot(hb, wvr_ref[...], preferred_element_type=F32).astype(BF16)
    gr_ref[0] = jnp.dot(hb, wgr_ref[...], preferred_element_type=F32).astype(BF16)


def _inproj(x, shift, scale, g, weights, tables, ts):
    bsz, seq, _ = x.shape
    row = lambda b, i: (b, i, 0)
    per_b = lambda b, i: (b, 0, 0)
    const2 = lambda b, i: (0, 0)
    head_major = lambda b, i: (b, 0, i, 0)

    def wspec(w):
        return pl.BlockSpec(w.shape, const2)

    hm_shape = jax.ShapeDtypeStruct((bsz, ATT_HEADS, seq, LANES), BF16)
    hm_spec = pl.BlockSpec((1, ATT_HEADS, ts, LANES), head_major)
    pair_shape = jax.ShapeDtypeStruct((bsz, ATT_HEADS // 2, seq, LANES), BF16)
    pair_spec = pl.BlockSpec((1, ATT_HEADS // 2, ts, LANES), head_major)
    wide = lambda n, dt=BF16: jax.ShapeDtypeStruct((bsz, seq, n), dt)
    wspec_out = lambda n: pl.BlockSpec((1, ts, n), row)
    return pl.pallas_call(
        _inproj_kernel,
        grid=(bsz, seq // ts),
        in_specs=[
            pl.BlockSpec((1, ts, D_MODEL), row),
            pl.BlockSpec((1, 1, D_MODEL), per_b),
            pl.BlockSpec((1, 1, D_MODEL), per_b),
            pl.BlockSpec((1, D_MODEL), const2),
            *[wspec(w) for w in weights],
            *[pl.BlockSpec((1, ts, LANES), row)] * 4,
        ],
        out_specs=[hm_spec, pair_spec,
                   pl.BlockSpec((1, ATT_HEADS, ts // DSA_TK, V_ROWS, DSA_TK), lambda b, i: (b, 0, i, 0, 0)),
                   hm_spec, wspec_out(LANES),
                   pl.BlockSpec((1, IDX_HEADS, ts), lambda b, i: (b, 0, i)),
                   wspec_out(RET_WIDTH), wspec_out(RET_WIDTH), wspec_out(RET_WIDTH), wspec_out(RET_WIDTH)],
        out_shape=[hm_shape, pair_shape,
                   jax.ShapeDtypeStruct((bsz, ATT_HEADS, seq // DSA_TK, V_ROWS, DSA_TK), BF16),
                   hm_shape, wide(LANES),
                   jax.ShapeDtypeStruct((bsz, IDX_HEADS, seq), F32),
                   wide(RET_WIDTH), wide(RET_WIDTH), wide(RET_WIDTH), wide(RET_WIDTH)],
        compiler_params=_params("parallel", "parallel"),
        name="inproj",
    )(x, shift, scale, g, *weights, *tables)


def _key_to_float(u):
    key = u ^ INT_MIN
    bits = jnp.where(key >= 0, key, key ^ 0x7FFFFFFF)
    return lax.bitcast_convert_type(bits, F32)


def _dsa_kernel(qi_ref, ki_ref, wit_ref, qa_ref, ka_ref, vt_ref, o_ref,
                sc_ref, sb_ref, crit_ref, m_ref, acc_ref, kmax_ref, qit_ref, qat_ref, *, topk):
    tq, tk = DSA_TQ, DSA_TK
    qb = pl.program_id(1)
    nk = (qb + 1) * (tq // tk)
    grp = (tk // SUB, SUB, tq)
    grp16 = (tk // (2 * SUB), 2 * SUB, tq)
    q_pos = qb * tq + lax.broadcasted_iota(jnp.int32, (SUB, tq), 1)

    def key_index(kc):
        return (kc * tk + lax.broadcasted_iota(jnp.int32, grp, 0) * SUB
                + lax.broadcasted_iota(jnp.int32, grp, 1))

    for h in range(IDX_HEADS):
        qit_ref[h] = qi_ref[0, h].astype(F32).T.astype(BF16)
        qat_ref[h] = qa_ref[0, h].astype(F32).T.astype(BF16)

    wt = wit_ref[0]

    def score_chunk(kc, carry):
        k0 = pl.multiple_of(kc * tk, tk)
        kch = ki_ref[0, pl.ds(k0, tk), :]
        acc = jnp.zeros((tk, tq), F32)
        for h in range(IDX_HEADS):
            logit = jnp.dot(kch, qit_ref[h], preferred_element_type=F32)
            acc = acc + jnp.maximum(logit, 0.0) * wt[h:h + 1, :]
        score = jnp.where(key_index(kc) <= q_pos[None], acc.reshape(grp), NEG_INF)
        sc_ref[kc] = score
        sb_ref[kc] = score.reshape(tk, tq).astype(BF16).reshape(grp16)
        return carry

    lax.fori_loop(0, nk, score_chunk, 0)

    def count(pred):
        def body(kc, acc):
            hit = jnp.where(pred(sc_ref[kc], kc), 1.0, 0.0)
            parts = [hit[g] for g in range(grp[0])]
            while len(parts) > 1:
                parts = [parts[i] + parts[i + 1] for i in range(0, len(parts), 2)]
            return acc + parts[0]
        acc = lax.fori_loop(0, nk, body, jnp.zeros((SUB, tq), F32))
        return jnp.sum(acc, axis=0, keepdims=True)

    def rows(v):
        return jnp.broadcast_to(v, (SUB, tq))[None]

    few_keys = q_pos[:1] < topk - 1

    def count_bf16(cand):
        cb = jnp.broadcast_to(cand, (2 * SUB, tq)).astype(BF16)[None]

        def body(kc, acc):
            hit = jnp.where(sb_ref[kc] >= cb, jnp.ones((), BF16), jnp.zeros((), BF16))
            parts = [hit[g] for g in range(grp16[0])]
            while len(parts) > 1:
                parts = [parts[i] + parts[i + 1] for i in range(0, len(parts), 2)]
            return acc + parts[0].astype(F32)
        acc = lax.fori_loop(0, nk, body, jnp.zeros((2 * SUB, tq), F32))
        return jnp.sum(acc, axis=0, keepdims=True)

    def high_body(i, u):
        trial = u | lax.shift_left(jnp.int32(1), 31 - i)
        return jnp.where(count_bf16(_key_to_float(trial)) >= topk, trial, u)

    u_hi = lax.fori_loop(0, BF16_KEY_BITS, high_body, jnp.zeros((1, tq), jnp.int32))
    base = u_hi - 2 ** (32 - BF16_KEY_BITS)

    def low_body(i, state):
        d, cnt_u = state
        trial = d | lax.shift_left(jnp.int32(1), 32 - BF16_KEY_BITS - i)
        cand = rows(_key_to_float(base + trial))
        cnt = count(lambda x, kc: x >= cand)
        accept = cnt >= topk
        return jnp.where(accept, trial, d), jnp.where(accept, cnt, cnt_u)

    d, cnt_u = lax.fori_loop(
        0, 32 - BF16_KEY_BITS + 1, low_body,
        (jnp.zeros((1, tq), jnp.int32), jnp.full((1, tq), 2.0 * topk, F32)))
    u = base + d
    thr = rows(jnp.where(few_keys, F32_LOWEST, _key_to_float(u)))
    has_excess_ties = jnp.max(jnp.where(few_keys, 0.0, cnt_u)) > topk

    def tie_plan():
        need = topk - count(lambda x, kc: x > thr)

        def chunk_body(kc, carry):
            seen, crit_chunk, seen_before = carry
            tie = jnp.where(sc_ref[kc] == thr, 1.0, 0.0)
            parts = [tie[g] for g in range(grp[0])]
            while len(parts) > 1:
                parts = [parts[i] + parts[i + 1] for i in range(0, len(parts), 2)]
            seen_now = seen + jnp.sum(parts[0], axis=0, keepdims=True)
            reached = jnp.where((crit_chunk == NO_CHUNK) & (seen_now >= need), 1.0, 0.0)
            crit_ref[...] = crit_ref[...] + rows(reached) * (tie - crit_ref[...])
            return (seen_now, jnp.where(reached > 0.5, kc, crit_chunk),
                    jnp.where(reached > 0.5, seen, seen_before))

        crit_ref[...] = jnp.zeros(crit_ref.shape, F32)
        zero = jnp.zeros((1, tq), F32)
        _, crit_chunk, seen_before = lax.fori_loop(
            0, nk, chunk_body, (zero, jnp.full((1, tq), NO_CHUNK, jnp.int32), zero))
        tie = crit_ref[...].reshape(tk, tq)
        lower = (lax.broadcasted_iota(jnp.int32, (tk, tk), 0)
                 >= lax.broadcasted_iota(jnp.int32, (tk, tk), 1))
        rank = jnp.dot(jnp.where(lower, 1.0, 0.0).astype(BF16), tie.astype(BF16),
                       preferred_element_type=F32)
        crit_ref[...] = jnp.where(rank <= need - seen_before, tie, 0.0).reshape(grp)
        return crit_chunk

    def no_excess_ties():
        crit_ref[...] = jnp.zeros(crit_ref.shape, F32)
        return jnp.full((1, tq), NO_CHUNK, jnp.int32)

    crit_chunk = lax.cond(has_excess_ties, tie_plan, no_excess_ties)

    def mask_body(kc, c):
        x = sc_ref[kc]
        tie_kept = (rows(jnp.where(kc < crit_chunk, 1.0, 0.0))
                    + rows(jnp.where(kc == crit_chunk, 1.0, 0.0)) * crit_ref[...])
        keep = (x > thr) | ((x == thr) & (tie_kept > 0.5))
        sc_ref[kc] = jnp.where(keep, 0.0, NEG_INF)
        return c

    lax.fori_loop(0, nk, mask_body, 0)

    def logits(kc, h):
        k0 = pl.multiple_of(kc * tk, tk)
        kch = ka_ref[0, h // 2, pl.ds(k0, tk), :]
        s = jnp.dot(kch, qat_ref[h], preferred_element_type=F32)
        return s.reshape(grp) + sc_ref[kc]

    def max_chunk(kc, m):
        return jnp.stack([jnp.maximum(m[h], jnp.max(logits(kc, h), axis=0))
                          for h in range(ATT_HEADS)])

    def pv_chunk(kc, carry):
        ss = [logits(kc, h) for h in range(ATT_HEADS)]
        for h in range(ATT_HEADS):
            p = jnp.exp(ss[h] - m_ref[h][None]).reshape(tk, tq).astype(BF16)
            acc_ref[h] += jnp.dot(vt_ref[0, h, kc], p, preferred_element_type=F32)
        return carry

    def exp_pv_pass():
        acc_ref[...] = jnp.zeros(acc_ref.shape, F32)
        lax.fori_loop(0, nk, pv_chunk, 0)

    ones_r = jnp.ones((SUB, LANES), BF16)
    lane_r = lax.broadcasted_iota(jnp.int32, (SUB, LANES), 1)

    @pl.when(qb == 0)
    def _():
        for h in range(ATT_HEADS):
            k = ka_ref[0, h // 2]
            head_lanes = jnp.where((lane_r >= 64) == (h % 2 == 1), 1.0, 0.0).astype(BF16)
            k_norm2 = lax.dot_general(head_lanes, k * k, _NT, preferred_element_type=F32)
            kmax_ref[h] = jnp.broadcast_to(jnp.max(k_norm2, axis=1, keepdims=True), (SUB, tq))

    for h in range(ATT_HEADS):
        q = qa_ref[0, h]
        q_norm2 = lax.dot_general(ones_r, q * q, _NT, preferred_element_type=F32)
        m_ref[h] = jnp.sqrt(q_norm2 * kmax_ref[h])
    exp_pv_pass()
    denom_min = jnp.min(jnp.stack([acc_ref[h][64:65] for h in range(ATT_HEADS)]))

    @pl.when(jnp.logical_not(denom_min > DENOM_FLOOR))
    def _():
        m8 = lax.fori_loop(0, nk, max_chunk, jnp.full((ATT_HEADS, SUB, tq), NEG_INF, F32))
        m_ref[...] = jnp.broadcast_to(jnp.max(m8, axis=1, keepdims=True), m8.shape)
        exp_pv_pass()

    outs = []
    for h in range(ATT_HEADS):
        acc = acc_ref[h]
        outs.append(acc[:64] / acc[64:65])
    o_ref[0] = jnp.concatenate(outs, axis=0).T.astype(BF16)


def _dsa(qi, ki, wit, qa, ka, vt):
    bsz, _, seq, _ = qi.shape
    tq, tk = DSA_TQ, DSA_TK
    topk = min(TOPK_MAX, seq // 4)
    assert seq % tq == 0 and tq % tk == 0 and tk >= topk
    blk_q = lambda b, i: (b, 0, i, 0)
    return pl.pallas_call(
        functools.partial(_dsa_kernel, topk=topk),
        grid=(bsz, seq // tq),
        in_specs=[
            pl.BlockSpec((1, IDX_HEADS, tq, LANES), blk_q),
            pl.BlockSpec((1, seq, LANES), lambda b, i: (b, 0, 0)),
            pl.BlockSpec((1, IDX_HEADS, tq), lambda b, i: (b, 0, i)),
            pl.BlockSpec((1, ATT_HEADS, tq, LANES), blk_q),
            pl.BlockSpec((1, ATT_HEADS // 2, seq, LANES), lambda b, i: (b, 0, 0, 0)),
            pl.BlockSpec((1, ATT_HEADS, seq // tk, V_ROWS, tk), lambda b, i: (b, 0, 0, 0, 0)),
        ],
        out_specs=pl.BlockSpec((1, tq, ATT_WIDTH), lambda b, i: (b, i, 0)),
        out_shape=jax.ShapeDtypeStruct((bsz, seq, ATT_WIDTH), BF16),
        scratch_shapes=[
            pltpu.VMEM((seq // tk, tk // SUB, SUB, tq), F32),
            pltpu.VMEM((seq // tk, tk // (2 * SUB), 2 * SUB, tq), BF16),
            pltpu.VMEM((tk // SUB, SUB, tq), F32),
            pltpu.VMEM((ATT_HEADS, SUB, tq), F32),
            pltpu.VMEM((ATT_HEADS, V_ROWS, tq), F32),
            pltpu.VMEM((ATT_HEADS, SUB, tq), F32),
            pltpu.VMEM((IDX_HEADS, LANES, tq), BF16),
            pltpu.VMEM((ATT_HEADS, LANES, tq), BF16),
        ],
        compiler_params=_params("parallel", "arbitrary"),
        name="dsa",
    )(qi, ki, wit, qa, ka, vt)


RET_CHUNK = 256


def _ret_kernel(q_ref, k_ref, v_ref, g_ref, o_ref, state_ref, decay_ref):
    c = RET_CHUNK
    n = pl.program_id(1)
    row = lax.broadcasted_iota(jnp.int32, (c, c), 0)
    col = lax.broadcasted_iota(jnp.int32, (c, c), 1)
    idx = lax.broadcasted_iota(jnp.int32, (c, 1), 0).astype(F32)

    @pl.when(n == 0)
    def _():
        state_ref[...] = jnp.zeros_like(state_ref)
        for h in range(RET_HEADS):
            lg = math.log(1.0 - 2.0 ** (-5.0 - h))
            diff = (row - col).astype(F32)
            decay_ref[h] = jnp.where(row >= col, jnp.exp(lg * jnp.maximum(diff, 0.0)), 0.0)

    for h in range(RET_HEADS):
        lg = math.log(1.0 - 2.0 ** (-5.0 - h))
        sl = slice(h * LANES, (h + 1) * LANES)
        q = q_ref[0, :, sl]
        k = k_ref[0, :, sl]
        v = v_ref[0, :, sl]
        inner = lax.dot_general(q, k, _NT, preferred_element_type=F32) * decay_ref[h]
        o = jnp.dot(inner.astype(BF16), v, preferred_element_type=F32)
        qd = (q.astype(F32) * jnp.exp(lg * (idx + 1.0))).astype(BF16)
        state = state_ref[h]
        o = o + jnp.dot(qd, state.astype(BF16), preferred_element_type=F32)
        kd = (k.astype(F32) * jnp.exp(lg * (c - 1.0 - idx))).astype(BF16)
        kv = lax.dot_general(kd, v, _TN, preferred_element_type=F32)
        state_ref[h] = state * math.exp(lg * c) + kv
        gate = _silu(g_ref[0, :, sl].astype(F32))
        o_ref[0, :, sl] = (gate * _rms(o)).astype(BF16)


def _retention(qr, kr, vr, gr):
    bsz, seq, _ = qr.shape
    c = RET_CHUNK
    spec = pl.BlockSpec((1, c, RET_WIDTH), lambda b, i: (b, i, 0))
    return pl.pallas_call(
        _ret_kernel,
        grid=(bsz, seq // c),
        in_specs=[spec] * 4,
        out_specs=spec,
        out_shape=jax.ShapeDtypeStruct((bsz, seq, RET_WIDTH), BF16),
        scratch_shapes=[
            pltpu.VMEM((RET_HEADS, RET_HEAD_DIM, RET_HEAD_DIM), F32),
            pltpu.VMEM((RET_HEADS, c, c), F32),
        ],
        compiler_params=_params("arbitrary", "arbitrary"),
        name="ret",
    )(qr, kr, vr, gr)


ROUTER_E0 = N_GROUPS


def _mid_kernel(att_ref, ret_ref, x_ref, woa_ref, wor_ref, gpost_ref, gate1_ref,
                gpre_ref, scale2_ref, shift2_ref, wr_ref, br_ref,
                h1_ref, h2_ref, comb_ref):
    mix = jnp.dot(att_ref[0], woa_ref[...], preferred_element_type=F32)
    mix = mix + jnp.dot(ret_ref[0], wor_ref[...], preferred_element_type=F32)
    h1 = x_ref[0] + gate1_ref[0] * (_rms(mix) * gpost_ref[...])
    h1_ref[0] = h1
    h2 = (_rms(h1) * gpre_ref[...] * (1.0 + scale2_ref[0]) + shift2_ref[0]).astype(BF16)
    h2_ref[0] = h2

    logits = jnp.dot(h2, wr_ref[...], preferred_element_type=F32) + br_ref[...]
    lane = lax.broadcasted_iota(jnp.int32, logits.shape, 1)
    big = jnp.int32(LANES)

    def first_argmax(vals, vmax):
        return jnp.min(jnp.where(vals == vmax, lane, big), axis=1, keepdims=True)

    gl = jnp.where(lane < N_GROUPS, logits, NEG_INF)
    gexp = jnp.exp(gl - jnp.max(gl, axis=1, keepdims=True))
    gprob = gexp / jnp.sum(gexp, axis=1, keepdims=True)
    g_top = jnp.max(gprob, axis=1, keepdims=True)
    g_sel = first_argmax(gprob, g_top)
    e_lo = ROUTER_E0 + g_sel * EXPERTS_PER_GROUP
    in_group = (lane >= e_lo) & (lane < e_lo + EXPERTS_PER_GROUP)
    el = jnp.where(in_group, logits, NEG_INF)
    eexp = jnp.exp(el - jnp.max(el, axis=1, keepdims=True))
    eprob = jnp.where(in_group, eexp / jnp.sum(eexp, axis=1, keepdims=True), -1.0)
    top1 = jnp.max(eprob, axis=1, keepdims=True)
    idx1 = first_argmax(eprob, top1)
    rest = jnp.where(lane == idx1, -1.0, eprob)
    top2 = jnp.max(rest, axis=1, keepdims=True)
    idx2 = first_argmax(rest, top2)
    norm = g_top / (top1 + top2)
    comb_ref[0] = jnp.where(lane == idx1, top1 * norm, 0.0) + jnp.where(lane == idx2, top2 * norm, 0.0)


def _mid(att, ret, x, woa, wor, gpost, gate1, gpre, scale2, shift2, wr, br, ts):
    bsz, seq, _ = x.shape
    row = lambda b, i: (b, i, 0)
    per_b = lambda b, i: (b, 0, 0)
    const2 = lambda b, i: (0, 0)
    vec = pl.BlockSpec((1, D_MODEL), const2)
    bvec = pl.BlockSpec((1, 1, D_MODEL), per_b)
    return pl.pallas_call(
        _mid_kernel,
        grid=(bsz, seq // ts),
        in_specs=[
            pl.BlockSpec((1, ts, ATT_WIDTH), row),
            pl.BlockSpec((1, ts, RET_WIDTH), row),
            pl.BlockSpec((1, ts, D_MODEL), row),
            pl.BlockSpec(woa.shape, const2),
            pl.BlockSpec(wor.shape, const2),
            vec, bvec, vec, bvec, bvec,
            pl.BlockSpec(wr.shape, const2),
            pl.BlockSpec((1, LANES), const2),
        ],
        out_specs=[pl.BlockSpec((1, ts, D_MODEL), row),
                   pl.BlockSpec((1, ts, D_MODEL), row),
                   pl.BlockSpec((1, ts, LANES), row)],
        out_shape=[jax.ShapeDtypeStruct((bsz, seq, D_MODEL), F32),
                   jax.ShapeDtypeStruct((bsz, seq, D_MODEL), BF16),
                   jax.ShapeDtypeStruct((bsz, seq, LANES), F32)],
        compiler_params=_params("parallel", "parallel"),
        name="mid",
    )(att, ret, x, woa, wor, gpost, gate1, gpre, scale2, shift2, wr, br)


MOE_EPS = 8


def _moe_kernel(h2_ref, comb_ref, h1_ref, gate2_ref, gpost_ref, wgu_ref, wd_ref, o_ref, acc_ref):
    step = pl.program_id(2)

    @pl.when(step == 0)
    def _():
        acc_ref[...] = jnp.zeros_like(acc_ref)

    t = h2_ref[0]
    comb = comb_ref[0]
    lane = lax.broadcasted_iota(jnp.int32, comb.shape, 1)
    hids = []
    for j in range(MOE_EPS):
        gu = jnp.dot(t, wgu_ref[j], preferred_element_type=F32)
        hid = _silu(gu[:, :D_EXPERT]) * gu[:, D_EXPERT:]
        e_lane = ROUTER_E0 + step * MOE_EPS + j
        cw = jnp.sum(jnp.where(lane == e_lane, comb, 0.0), axis=1, keepdims=True)
        hids.append((hid * cw).astype(BF16))
    wd = wd_ref[...].reshape(MOE_EPS * D_EXPERT, D_MODEL)
    acc_ref[...] += jnp.dot(jnp.concatenate(hids, axis=1), wd, preferred_element_type=F32)

    @pl.when(step == N_EXPERTS // MOE_EPS - 1)
    def _():
        o_ref[0] = h1_ref[0] + gate2_ref[0] * (_rms(acc_ref[...]) * gpost_ref[...])


def _moe(h2, comb, h1, gate2, gpost, wgu, wd, tm):
    bsz, seq, _ = h2.shape
    row = lambda b, i, e: (b, i, 0)
    return pl.pallas_call(
        _moe_kernel,
        grid=(bsz, seq // tm, N_EXPERTS // MOE_EPS),
        in_specs=[
            pl.BlockSpec((1, tm, D_MODEL), row),
            pl.BlockSpec((1, tm, LANES), row),
            pl.BlockSpec((1, tm, D_MODEL), row),
            pl.BlockSpec((1, 1, D_MODEL), lambda b, i, e: (b, 0, 0)),
            pl.BlockSpec((1, D_MODEL), lambda b, i, e: (0, 0)),
            pl.BlockSpec((MOE_EPS, D_MODEL, 2 * D_EXPERT), lambda b, i, e: (e, 0, 0)),
            pl.BlockSpec((MOE_EPS, D_EXPERT, D_MODEL), lambda b, i, e: (e, 0, 0)),
        ],
        out_specs=pl.BlockSpec((1, tm, D_MODEL), row),
        out_shape=jax.ShapeDtypeStruct((bsz, seq, D_MODEL), F32),
        scratch_shapes=[pltpu.VMEM((tm, D_MODEL), F32)],
        compiler_params=_params("parallel", "parallel", "arbitrary"),
        name="moe",
    )(h2, comb, h1, gate2, gpost, wgu, wd)


def _layer(x, positions, mod, g_pre_mix, w_in, w_out, g_post_mix, g_pre_ffn,
           w_group, b_group, w_expert, b_expert, w_gate_exp, w_up_exp, w_down_exp, g_post_ffn):
    bsz, seq, _ = x.shape
    shift1, scale1, gate1, shift2, scale2, gate2 = [
        m.reshape(bsz, 1, D_MODEL) for m in jnp.split(mod, N_MOD, axis=-1)]
    vec = lambda g: g.reshape(1, D_MODEL)

    a, i8, r = ATT_WIDTH, IDX_HEADS, RET_WIDTH
    offs = np.cumsum([0, a, a, a, a, IDX_HEAD_DIM, i8, r, r, r, r])
    seg = lambda j: w_in[:, offs[j]:offs[j + 1]].astype(BF16)
    wki = jnp.concatenate([seg(4), seg(4), jnp.pad(seg(5), ((0, 0), (0, LANES - i8)))], axis=1)
    weights = [seg(0), seg(1), seg(2), seg(3), wki,
               seg(6), seg(7), seg(8), seg(9)]

    tables = _rope_tables(positions, 512)
    qa, ka, vt, qi, ki, wit, qr, kr, vr, gr = _inproj(
        x, shift1, scale1, vec(g_pre_mix), weights, tables, 512)
    att = _dsa(qi, ki, wit, qa, ka, vt)
    ret = _retention(qr, kr, vr, gr)

    w_router = jnp.pad(jnp.concatenate([w_group, w_expert], axis=1),
                       ((0, 0), (0, LANES - N_GROUPS - N_EXPERTS))).astype(BF16)
    b_router = jnp.pad(jnp.concatenate([b_group, b_expert]),
                       (0, LANES - N_GROUPS - N_EXPERTS)).reshape(1, LANES)
    h1, h2, comb = _mid(att, ret, x, w_out[:ATT_WIDTH].astype(BF16), w_out[ATT_WIDTH:].astype(BF16),
                        vec(g_post_mix), gate1, vec(g_pre_ffn), scale2, shift2, w_router, b_router, 512)

    wgu = jnp.concatenate([w_gate_exp, w_up_exp], axis=2).astype(BF16)
    return _moe(h2, comb, h1, gate2, vec(g_post_ffn), wgu, w_down_exp.astype(BF16), 1024)


def kernel(x, c, positions, g_pre_mix, w_ada, b_ada, w_in, w_out, g_post_mix, g_pre_ffn,
           w_group, b_group, w_expert, b_expert, w_gate_exp, w_up_exp, w_down_exp, g_post_ffn):
    h = x
    for l in range(w_in.shape[0]):
        mod = _ada(c, w_ada[l], b_ada[l])
        h = _layer(h, positions, mod, g_pre_mix[l], w_in[l], w_out[l], g_post_mix[l], g_pre_ffn[l],
                   w_group[l], b_group[l], w_expert[l], b_expert[l],
                   w_gate_exp[l], w_up_exp[l], w_down_exp[l], g_post_ffn[l])
    return h
```

```python
import functools
import math

import numpy as np
import jax
import jax.numpy as jnp
from jax import lax
from jax.experimental import pallas as pl
from jax.experimental.pallas import tpu as pltpu

D_MODEL = 1024
ATT_HEADS = 8
ATT_HEAD_DIM = 64
IDX_HEADS = 8
IDX_HEAD_DIM = 64
TOPK_MAX = 256
RET_HEADS = 4
RET_HEAD_DIM = 128
ROPE_THETA = 10000.0
ATT_WIDTH = ATT_HEADS * ATT_HEAD_DIM
RET_WIDTH = RET_HEADS * RET_HEAD_DIM
N_GROUPS = 4
EXPERTS_PER_GROUP = 8
N_EXPERTS = N_GROUPS * EXPERTS_PER_GROUP
D_EXPERT = 256
N_MOD = 6
EPS = 1e-6

LANES = 128
SUB = 8
VMEM_LIMIT = 56 * 1024 * 1024

DSA_TQ = 256
DSA_TK = 256
V_ROWS = 80
F32 = jnp.float32
BF16 = jnp.bfloat16
NEG_INF = float("-inf")
F32_LOWEST = float(np.finfo(np.float32).min)
INT_MIN = -2 ** 31
BF16_KEY_BITS = 16
NO_CHUNK = 2 ** 30
DENOM_FLOOR = 1e-30

_NT = (((1,), (1,)), ((), ()))
_TN = (((0,), (0,)), ((), ()))


def _params(*sem):
    return pltpu.CompilerParams(dimension_semantics=sem, vmem_limit_bytes=VMEM_LIMIT)


def _rms(x):
    return x * lax.rsqrt(jnp.mean(x * x, axis=-1, keepdims=True) + EPS)


def _silu(x):
    return x * (1.0 / (1.0 + jnp.exp(-x)))


def _ada_kernel(c_ref, w_ref, b_ref, o_ref):
    a = _silu(c_ref[...]).astype(BF16)
    o_ref[...] = jnp.dot(a, w_ref[...].astype(BF16), preferred_element_type=F32) + b_ref[...]


def _ada(c, w, b):
    bsz = c.shape[0]
    n = w.shape[1]
    tn = D_MODEL
    return pl.pallas_call(
        _ada_kernel,
        grid=(n // tn,),
        in_specs=[
            pl.BlockSpec((bsz, D_MODEL), lambda j: (0, 0)),
            pl.BlockSpec((D_MODEL, tn), lambda j: (0, j)),
            pl.BlockSpec((1, tn), lambda j: (0, j)),
        ],
        out_specs=pl.BlockSpec((bsz, tn), lambda j: (0, j)),
        out_shape=jax.ShapeDtypeStruct((bsz, n), F32),
        compiler_params=_params("arbitrary"),
        name="ada",
    )(c, w, b.reshape(1, n))


def _tables_kernel(pos_ref, f_ref, cs64_ref, sn64_ref, cs128_ref, sn128_ref):
    ang = pos_ref[0].astype(F32) * f_ref[...]
    c = jnp.cos(ang)
    s = jnp.sin(ang)
    c_sw = pltpu.roll(c, 64, 1)
    s_sw = pltpu.roll(s, 64, 1)
    lane = lax.broadcasted_iota(jnp.int32, c.shape, 1)
    lo = lane < 64
    cs128_ref[0] = jnp.where(lo, c, c_sw)
    sn128_ref[0] = jnp.where(lo, -s, s_sw)
    cs64_ref[0] = jnp.where(lo, c_sw, c)
    s64 = jnp.where(lo, s_sw, s)
    sn64_ref[0] = jnp.where((lane % 64) < 32, -s64, s64)


def _rope_tables(positions, ts):
    bsz, seq = positions.shape
    f64 = ROPE_THETA ** (-jnp.arange(32, dtype=F32) / 32)
    f128 = ROPE_THETA ** (-jnp.arange(64, dtype=F32) / 64)
    frow = jnp.concatenate([f128, f64, f64]).reshape(1, LANES)
    tab = jax.ShapeDtypeStruct((bsz, seq, LANES), F32)
    tspec = pl.BlockSpec((1, ts, LANES), lambda b, i: (b, i, 0))
    return pl.pallas_call(
        _tables_kernel,
        grid=(bsz, seq // ts),
        in_specs=[
            pl.BlockSpec((1, ts, 1), lambda b, i: (b, i, 0)),
            pl.BlockSpec((1, LANES), lambda b, i: (0, 0)),
        ],
        out_specs=[tspec] * 4,
        out_shape=[tab] * 4,
        compiler_params=_params("parallel", "parallel"),
        name="tables",
    )(positions.reshape(bsz, seq, 1), frow)


def _rope64(y, cs, sn):
    lane = lax.broadcasted_iota(jnp.int32, y.shape, 1)
    rot = jnp.where((lane % 64) < 32, pltpu.roll(y, 96, 1), pltpu.roll(y, 32, 1))
    return y * cs + rot * sn


def _rope128(y, cs, sn):
    return y * cs + pltpu.roll(y, 64, 1) * sn


def _inproj_kernel(x_ref, shift_ref, scale_ref, g_ref,
                   wqa_ref, wka_ref, wv_ref, wqi_ref, wki_ref,
                   wqr_ref, wkr_ref, wvr_ref, wgr_ref,
                   cs64_ref, sn64_ref, cs128_ref, sn128_ref,
                   qa_ref, ka_ref, vt_ref, qi_ref, ki_ref, wit_ref,
                   qr_ref, kr_ref, vr_ref, gr_ref):
    x = x_ref[0]
    h = _rms(x) * g_ref[...] * (1.0 + scale_ref[0]) + shift_ref[0]
    hb = h.astype(BF16)
    cs64, sn64 = cs64_ref[0], sn64_ref[0]
    cs128, sn128 = cs128_ref[0], sn128_ref[0]

    lane = lax.broadcasted_iota(jnp.int32, (hb.shape[0], LANES), 1)
    wide = 2 * LANES

    def pairs(w_ref, rope, cs, sn):
        for c in range(w_ref.shape[1] // wide):
            y = jnp.dot(hb, w_ref[:, c * wide:(c + 1) * wide], preferred_element_type=F32)
            for j in range(wide // LANES):
                yield 2 * c + j, rope(y[:, j * LANES:(j + 1) * LANES], cs, sn)

    def query_heads(w_ref, o_ref, mult):
        for c, y in pairs(w_ref, _rope64, cs64, sn64):
            y = y * mult
            o_ref[0, 2 * c] = jnp.where(lane < 64, y, 0.0).astype(BF16)
            o_ref[0, 2 * c + 1] = jnp.where(lane >= 64, y, 0.0).astype(BF16)

    query_heads(wqa_ref, qa_ref, ATT_HEAD_DIM ** -0.5)
    query_heads(wqi_ref, qi_ref, 1.0)
    for c, y in pairs(wka_ref, _rope64, cs64, sn64):
        ka_ref[0, c] = y.astype(BF16)

    ts = hb.shape[0]
    vt = jnp.dot(hb, wv_ref[...], preferred_element_type=F32).T
    for h in range(ATT_HEADS):
        for j in range(ts // DSA_TK):
            blk = vt[h * 64:(h + 1) * 64, j * DSA_TK:(j + 1) * DSA_TK]
            vt_ref[0, h, j, :64, :] = blk.astype(BF16)
            vt_ref[0, h, j, 64:, :] = jnp.ones((V_ROWS - 64, DSA_TK), BF16)

    yk = jnp.dot(hb, wki_ref[...], preferred_element_type=F32)
    ki_ref[0] = _rope64(yk[:, :LANES], cs64, sn64).astype(BF16)
    wit_ref[0] = yk[:, LANES:].T[:IDX_HEADS] * ((IDX_HEADS ** -0.5) * (IDX_HEAD_DIM ** -0.5))

    for c, y in pairs(wqr_ref, _rope128, cs128, sn128):
        qr_ref[0, :, c * LANES:(c + 1) * LANES] = y.astype(BF16)
    for c, y in pairs(wkr_ref, _rope128, cs128, sn128):
        kr_ref[0, :, c * LANES:(c + 1) * LANES] = (y * (RET_HEAD_DIM ** -0.5)).astype(BF16)
    vr_ref[0] = jnp.dot(hb, wvr_ref[...], preferred_element_type=F32).astype(BF16)
    gr_ref[0] = jnp.dot(hb, wgr_ref[...], preferred_element_type=F32).astype(BF16)


def _inproj(x, shift, scale, g, weights, tables, ts):
    bsz, seq, _ = x.shape
    row = lambda b, i: (b, i, 0)
    per_b = lambda b, i: (b, 0, 0)
    const2 = lambda b, i: (0, 0)
    head_major = lambda b, i: (b, 0, i, 0)

    def wspec(w):
        return pl.BlockSpec(w.shape, const2)

    hm_shape = jax.ShapeDtypeStruct((bsz, ATT_HEADS, seq, LANES), BF16)
    hm_spec = pl.BlockSpec((1, ATT_HEADS, ts, LANES), head_major)
    pair_shape = jax.ShapeDtypeStruct((bsz, ATT_HEADS // 2, seq, LANES), BF16)
    pair_spec = pl.BlockSpec((1, ATT_HEADS // 2, ts, LANES), head_major)
    wide = lambda n, dt=BF16: jax.ShapeDtypeStruct((bsz, seq, n), dt)
    wspec_out = lambda n: pl.BlockSpec((1, ts, n), row)
    return pl.pallas_call(
        _inproj_kernel,
        grid=(bsz, seq // ts),
        in_specs=[
            pl.BlockSpec((1, ts, D_MODEL), row),
            pl.BlockSpec((1, 1, D_MODEL), per_b),
            pl.BlockSpec((1, 1, D_MODEL), per_b),
            pl.BlockSpec((1, D_MODEL), const2),
            *[wspec(w) for w in weights],
            *[pl.BlockSpec((1, ts, LANES), row)] * 4,
        ],
        out_specs=[hm_spec, pair_spec,
                   pl.BlockSpec((1, ATT_HEADS, ts // DSA_TK, V_ROWS, DSA_TK), lambda b, i: (b, 0, i, 0, 0)),
                   hm_spec, wspec_out(LANES),
                   pl.BlockSpec((1, IDX_HEADS, ts), lambda b, i: (b, 0, i)),
                   wspec_out(RET_WIDTH), wspec_out(RET_WIDTH), wspec_out(RET_WIDTH), wspec_out(RET_WIDTH)],
        out_shape=[hm_shape, pair_shape,
                   jax.ShapeDtypeStruct((bsz, ATT_HEADS, seq // DSA_TK, V_ROWS, DSA_TK), BF16),
                   hm_shape, wide(LANES),
                   jax.ShapeDtypeStruct((bsz, IDX_HEADS, seq), F32),
                   wide(RET_WIDTH), wide(RET_WIDTH), wide(RET_WIDTH), wide(RET_WIDTH)],
        compiler_params=_params("parallel", "parallel"),
        name="inproj",
    )(x, shift, scale, g, *weights, *tables)


def _key_to_float(u):
    key = u ^ INT_MIN
    bits = jnp.where(key >= 0, key, key ^ 0x7FFFFFFF)
    return lax.bitcast_convert_type(bits, F32)


def _dsa_kernel(qi_ref, ki_ref, wit_ref, qa_ref, ka_ref, vt_ref, o_ref,
                sc_ref, sb_ref, crit_ref, m_ref, acc_ref, kmax_ref, qit_ref, qat_ref, *, topk):
    tq, tk = DSA_TQ, DSA_TK
    qb = pl.program_id(1)
    nk = (qb + 1) * (tq // tk)
    grp = (tk // SUB, SUB, tq)
    grp16 = (tk // (2 * SUB), 2 * SUB, tq)
    q_pos = qb * tq + lax.broadcasted_iota(jnp.int32, (SUB, tq), 1)

    def key_index(kc):
        return (kc * tk + lax.broadcasted_iota(jnp.int32, grp, 0) * SUB
                + lax.broadcasted_iota(jnp.int32, grp, 1))

    for h in range(IDX_HEADS):
        qit_ref[h] = qi_ref[0, h].astype(F32).T.astype(BF16)
        qat_ref[h] = qa_ref[0, h].astype(F32).T.astype(BF16)

    wt = wit_ref[0]

    def score_chunk(kc, carry):
        k0 = pl.multiple_of(kc * tk, tk)
        kch = ki_ref[0, pl.ds(k0, tk), :]
        acc = jnp.zeros((tk, tq), F32)
        for h in range(IDX_HEADS):
            logit = jnp.dot(kch, qit_ref[h], preferred_element_type=F32)
            acc = acc + jnp.maximum(logit, 0.0) * wt[h:h + 1, :]
        score = jnp.where(key_index(kc) <= q_pos[None], acc.reshape(grp), NEG_INF)
        sc_ref[kc] = score
        sb_ref[kc] = score.reshape(tk, tq).astype(BF16).reshape(grp16)
        return carry

    lax.fori_loop(0, nk, score_chunk, 0)

    def count(pred):
        def body(kc, acc):
            hit = jnp.where(pred(sc_ref[kc], kc), 1.0, 0.0)
            parts = [hit[g] for g in range(grp[0])]
            while len(parts) > 1:
                parts = [parts[i] + parts[i + 1] for i in range(0, len(parts), 2)]
            return acc + parts[0]
        acc = lax.fori_loop(0, nk, body, jnp.zeros((SUB, tq), F32))
        return jnp.sum(acc, axis=0, keepdims=True)

    def rows(v):
        return jnp.broadcast_to(v, (SUB, tq))[None]

    few_keys = q_pos[:1] < topk - 1

    def count_bf16(cand):
        cb = jnp.broadcast_to(cand, (2 * SUB, tq)).astype(BF16)[None]

        def body(kc, acc):
            hit = jnp.where(sb_ref[kc] >= cb, jnp.ones((), BF16), jnp.zeros((), BF16))
            parts = [hit[g] for g in range(grp16[0])]
            while len(parts) > 1:
                parts = [parts[i] + parts[i + 1] for i in range(0, len(parts), 2)]
            return acc + parts[0].astype(F32)
        acc = lax.fori_loop(0, nk, body, jnp.zeros((2 * SUB, tq), F32))
        return jnp.sum(acc, axis=0, keepdims=True)

    def high_body(i, u):
        trial = u | lax.shift_left(jnp.int32(1), 31 - i)
        return jnp.where(count_bf16(_key_to_float(trial)) >= topk, trial, u)

    u_hi = lax.fori_loop(0, BF16_KEY_BITS, high_body, jnp.zeros((1, tq), jnp.int32))
    base = u_hi - 2 ** (32 - BF16_KEY_BITS)

    def low_body(i, state):
        d, cnt_u = state
        trial = d | lax.shift_left(jnp.int32(1), 32 - BF16_KEY_BITS - i)
        cand = rows(_key_to_float(base + trial))
        cnt = count(lambda x, kc: x >= cand)
        accept = cnt >= topk
        return jnp.where(accept, trial, d), jnp.where(accept, cnt, cnt_u)

    d, cnt_u = lax.fori_loop(
        0, 32 - BF16_KEY_BITS + 1, low_body,
        (jnp.zeros((1, tq), jnp.int32), jnp.full((1, tq), 2.0 * topk, F32)))
    u = base + d
    thr = rows(jnp.where(few_keys, F32_LOWEST, _key_to_float(u)))
    has_excess_ties = jnp.max(jnp.where(few_keys, 0.0, cnt_u)) > topk

    def tie_plan():
        need = topk - count(lambda x, kc: x > thr)

        def chunk_body(kc, carry):
            seen, crit_chunk, seen_before = carry
            tie = jnp.where(sc_ref[kc] == thr, 1.0, 0.0)
            parts = [tie[g] for g in range(grp[0])]
            while len(parts) > 1:
                parts = [parts[i] + parts[i + 1] for i in range(0, len(parts), 2)]
            seen_now = seen + jnp.sum(parts[0], axis=0, keepdims=True)
            reached = jnp.where((crit_chunk == NO_CHUNK) & (seen_now >= need), 1.0, 0.0)
            crit_ref[...] = crit_ref[...] + rows(reached) * (tie - crit_ref[...])
            return (seen_now, jnp.where(reached > 0.5, kc, crit_chunk),
                    jnp.where(reached > 0.5, seen, seen_before))

        crit_ref[...] = jnp.zeros(crit_ref.shape, F32)
        zero = jnp.zeros((1, tq), F32)
        _, crit_chunk, seen_before = lax.fori_loop(
            0, nk, chunk_body, (zero, jnp.full((1, tq), NO_CHUNK, jnp.int32), zero))
        tie = crit_ref[...].reshape(tk, tq)
        lower = (lax.broadcasted_iota(jnp.int32, (tk, tk), 0)
                 >= lax.broadcasted_iota(jnp.int32, (tk, tk), 1))
        rank = jnp.dot(jnp.where(lower, 1.0, 0.0).astype(BF16), tie.astype(BF16),
                       preferred_element_type=F32)
        crit_ref[...] = jnp.where(rank <= need - seen_before, tie, 0.0).reshape(grp)
        return crit_chunk

    def no_excess_ties():
        crit_ref[...] = jnp.zeros(crit_ref.shape, F32)
        return jnp.full((1, tq), NO_CHUNK, jnp.int32)

    crit_chunk = lax.cond(has_excess_ties, tie_plan, no_excess_ties)

    def mask_body(kc, c):
        x = sc_ref[kc]
        tie_kept = (rows(jnp.where(kc < crit_chunk, 1.0, 0.0))
                    + rows(jnp.where(kc == crit_chunk, 1.0, 0.0)) * crit_ref[...])
        keep = (x > thr) | ((x == thr) & (tie_kept > 0.5))
        sc_ref[kc] = jnp.where(keep, 0.0, NEG_INF)
        return c

    lax.fori_loop(0, nk, mask_body, 0)

    def logits(kc, h):
        k0 = pl.multiple_of(kc * tk, tk)
        kch = ka_ref[0, h // 2, pl.ds(k0, tk), :]
        s = jnp.dot(kch, qat_ref[h], preferred_element_type=F32)
        return s.reshape(grp) + sc_ref[kc]

    def max_chunk(kc, m):
        return jnp.stack([jnp.maximum(m[h], jnp.max(logits(kc, h), axis=0))
                          for h in range(ATT_HEADS)])

    def pv_chunk(kc, carry):
        ss = [logits(kc, h) for h in range(ATT_HEADS)]
        for h in range(ATT_HEADS):
            p = jnp.exp(ss[h] - m_ref[h][None]).reshape(tk, tq).astype(BF16)
            acc_ref[h] += jnp.dot(vt_ref[0, h, kc], p, preferred_element_type=F32)
        return carry

    def exp_pv_pass():
        acc_ref[...] = jnp.zeros(acc_ref.shape, F32)
        lax.fori_loop(0, nk, pv_chunk, 0)

    ones_r = jnp.ones((SUB, LANES), BF16)
    lane_r = lax.broadcasted_iota(jnp.int32, (SUB, LANES), 1)

    @pl.when(qb == 0)
    def _():
        for h in range(ATT_HEADS):
            k = ka_ref[0, h // 2]
            head_lanes = jnp.where((lane_r >= 64) == (h % 2 == 1), 1.0, 0.0).astype(BF16)
            k_norm2 = lax.dot_general(head_lanes, k * k, _NT, preferred_element_type=F32)
            kmax_ref[h] = jnp.broadcast_to(jnp.max(k_norm2, axis=1, keepdims=True), (SUB, tq))

    for h in range(ATT_HEADS):
        q = qa_ref[0, h]
        q_norm2 = lax.dot_general(ones_r, q * q, _NT, preferred_element_type=F32)
        m_ref[h] = jnp.sqrt(q_norm2 * kmax_ref[h])
    exp_pv_pass()
    denom_min = jnp.min(jnp.stack([acc_ref[h][64:65] for h in range(ATT_HEADS)]))

    @pl.when(jnp.logical_not(denom_min > DENOM_FLOOR))
    def _():
        m8 = lax.fori_loop(0, nk, max_chunk, jnp.full((ATT_HEADS, SUB, tq), NEG_INF, F32))
        m_ref[...] = jnp.broadcast_to(jnp.max(m8, axis=1, keepdims=True), m8.shape)
        exp_pv_pass()

    outs = []
    for h in range(ATT_HEADS):
        acc = acc_ref[h]
        outs.append(acc[:64] / acc[64:65])
    o_ref[0] = jnp.concatenate(outs, axis=0).T.astype(BF16)


def _dsa(qi, ki, wit, qa, ka, vt):
    bsz, _, seq, _ = qi.shape
    tq, tk = DSA_TQ, DSA_TK
    topk = min(TOPK_MAX, seq // 4)
    assert seq % tq == 0 and tq % tk == 0 and tk >= topk
    blk_q = lambda b, i: (b, 0, i, 0)
    return pl.pallas_call(
        functools.partial(_dsa_kernel, topk=topk),
        grid=(bsz, seq // tq),
        in_specs=[
            pl.BlockSpec((1, IDX_HEADS, tq, LANES), blk_q),
            pl.BlockSpec((1, seq, LANES), lambda b, i: (b, 0, 0)),
            pl.BlockSpec((1, IDX_HEADS, tq), lambda b, i: (b, 0, i)),
            pl.BlockSpec((1, ATT_HEADS, tq, LANES), blk_q),
            pl.BlockSpec((1, ATT_HEADS // 2, seq, LANES), lambda b, i: (b, 0, 0, 0)),
            pl.BlockSpec((1, ATT_HEADS, seq // tk, V_ROWS, tk), lambda b, i: (b, 0, 0, 0, 0)),
        ],
        out_specs=pl.BlockSpec((1, tq, ATT_WIDTH), lambda b, i: (b, i, 0)),
        out_shape=jax.ShapeDtypeStruct((bsz, seq, ATT_WIDTH), BF16),
        scratch_shapes=[
            pltpu.VMEM((seq // tk, tk // SUB, SUB, tq), F32),
            pltpu.VMEM((seq // tk, tk // (2 * SUB), 2 * SUB, tq), BF16),
            pltpu.VMEM((tk // SUB, SUB, tq), F32),
            pltpu.VMEM((ATT_HEADS, SUB, tq), F32),
            pltpu.VMEM((ATT_HEADS, V_ROWS, tq), F32),
            pltpu.VMEM((ATT_HEADS, SUB, tq), F32),
            pltpu.VMEM((IDX_HEADS, LANES, tq), BF16),
            pltpu.VMEM((ATT_HEADS, LANES, tq), BF16),
        ],
        compiler_params=_params("parallel", "arbitrary"),
        name="dsa",
    )(qi, ki, wit, qa, ka, vt)


RET_CHUNK = 256


def _ret_kernel(q_ref, k_ref, v_ref, g_ref, o_ref, state_ref, decay_ref):
    c = RET_CHUNK
    n = pl.program_id(1)
    row = lax.broadcasted_iota(jnp.int32, (c, c), 0)
    col = lax.broadcasted_iota(jnp.int32, (c, c), 1)
    idx = lax.broadcasted_iota(jnp.int32, (c, 1), 0).astype(F32)

    @pl.when(n == 0)
    def _():
        state_ref[...] = jnp.zeros_like(state_ref)
        for h in range(RET_HEADS):
            lg = math.log(1.0 - 2.0 ** (-5.0 - h))
            diff = (row - col).astype(F32)
            decay_ref[h] = jnp.where(row >= col, jnp.exp(lg * jnp.maximum(diff, 0.0)), 0.0)

    for h in range(RET_HEADS):
        lg = math.log(1.0 - 2.0 ** (-5.0 - h))
        sl = slice(h * LANES, (h + 1) * LANES)
        q = q_ref[0, :, sl]
        k = k_ref[0, :, sl]
        v = v_ref[0, :, sl]
        inner = lax.dot_general(q, k, _NT, preferred_element_type=F32) * decay_ref[h]
        o = jnp.dot(inner.astype(BF16), v, preferred_element_type=F32)
        qd = (q.astype(F32) * jnp.exp(lg * (idx + 1.0))).astype(BF16)
        state = state_ref[h]
        o = o + jnp.dot(qd, state.astype(BF16), preferred_element_type=F32)
        kd = (k.astype(F32) * jnp.exp(lg * (c - 1.0 - idx))).astype(BF16)
        kv = lax.dot_general(kd, v, _TN, preferred_element_type=F32)
        state_ref[h] = state * math.exp(lg * c) + kv
        gate = _silu(g_ref[0, :, sl].astype(F32))
        o_ref[0, :, sl] = (gate * _rms(o)).astype(BF16)


def _retention(qr, kr, vr, gr):
    bsz, seq, _ = qr.shape
    c = RET_CHUNK
    spec = pl.BlockSpec((1, c, RET_WIDTH), lambda b, i: (b, i, 0))
    return pl.pallas_call(
        _ret_kernel,
        grid=(bsz, seq // c),
        in_specs=[spec] * 4,
        out_specs=spec,
        out_shape=jax.ShapeDtypeStruct((bsz, seq, RET_WIDTH), BF16),
        scratch_shapes=[
            pltpu.VMEM((RET_HEADS, RET_HEAD_DIM, RET_HEAD_DIM), F32),
            pltpu.VMEM((RET_HEADS, c, c), F32),
        ],
        compiler_params=_params("arbitrary", "arbitrary"),
        name="ret",
    )(qr, kr, vr, gr)


ROUTER_E0 = N_GROUPS


def _mid_kernel(att_ref, ret_ref, x_ref, woa_ref, wor_ref, gpost_ref, gate1_ref,
                gpre_ref, scale2_ref, shift2_ref, wr_ref, br_ref,
                h1_ref, h2_ref, comb_ref):
    mix = jnp.dot(att_ref[0], woa_ref[...], preferred_element_type=F32)
    mix = mix + jnp.dot(ret_ref[0], wor_ref[...], preferred_element_type=F32)
    h1 = x_ref[0] + gate1_ref[0] * (_rms(mix) * gpost_ref[...])
    h1_ref[0] = h1
    h2 = (_rms(h1) * gpre_ref[...] * (1.0 + scale2_ref[0]) + shift2_ref[0]).astype(BF16)
    h2_ref[0] = h2

    logits = jnp.dot(h2, wr_ref[...], preferred_element_type=F32) + br_ref[...]
    lane = lax.broadcasted_iota(jnp.int32, logits.shape, 1)
    big = jnp.int32(LANES)

    def first_argmax(vals, vmax):
        return jnp.min(jnp.where(vals == vmax, lane, big), axis=1, keepdims=True)

    gl = jnp.where(lane < N_GROUPS, logits, NEG_INF)
    gexp = jnp.exp(gl - jnp.max(gl, axis=1, keepdims=True))
    gprob = gexp / jnp.sum(gexp, axis=1, keepdims=True)
    g_top = jnp.max(gprob, axis=1, keepdims=True)
    g_sel = first_argmax(gprob, g_top)
    e_lo = ROUTER_E0 + g_sel * EXPERTS_PER_GROUP
    in_group = (lane >= e_lo) & (lane < e_lo + EXPERTS_PER_GROUP)
    el = jnp.where(in_group, logits, NEG_INF)
    eexp = jnp.exp(el - jnp.max(el, axis=1, keepdims=True))
    eprob = jnp.where(in_group, eexp / jnp.sum(eexp, axis=1, keepdims=True), -1.0)
    top1 = jnp.max(eprob, axis=1, keepdims=True)
    idx1 = first_argmax(eprob, top1)
    rest = jnp.where(lane == idx1, -1.0, eprob)
    top2 = jnp.max(rest, axis=1, keepdims=True)
    idx2 = first_argmax(rest, top2)
    norm = g_top / (top1 + top2)
    comb_ref[0] = jnp.where(lane == idx1, top1 * norm, 0.0) + jnp.where(lane == idx2, top2 * norm, 0.0)


def _mid(att, ret, x, woa, wor, gpost, gate1, gpre, scale2, shift2, wr, br, ts):
    bsz, seq, _ = x.shape
    row = lambda b, i: (b, i, 0)
    per_b = lambda b, i: (b, 0, 0)
    const2 = lambda b, i: (0, 0)
    vec = pl.BlockSpec((1, D_MODEL), const2)
    bvec = pl.BlockSpec((1, 1, D_MODEL), per_b)
    return pl.pallas_call(
        _mid_kernel,
        grid=(bsz, seq // ts),
        in_specs=[
            pl.BlockSpec((1, ts, ATT_WIDTH), row),
            pl.BlockSpec((1, ts, RET_WIDTH), row),
            pl.BlockSpec((1, ts, D_MODEL), row),
            pl.BlockSpec(woa.shape, const2),
            pl.BlockSpec(wor.shape, const2),
            vec, bvec, vec, bvec, bvec,
            pl.BlockSpec(wr.shape, const2),
            pl.BlockSpec((1, LANES), const2),
        ],
        out_specs=[pl.BlockSpec((1, ts, D_MODEL), row),
                   pl.BlockSpec((1, ts, D_MODEL), row),
                   pl.BlockSpec((1, ts, LANES), row)],
        out_shape=[jax.ShapeDtypeStruct((bsz, seq, D_MODEL), F32),
                   jax.ShapeDtypeStruct((bsz, seq, D_MODEL), BF16),
                   jax.ShapeDtypeStruct((bsz, seq, LANES), F32)],
        compiler_params=_params("parallel", "parallel"),
        name="mid",
    )(att, ret, x, woa, wor, gpost, gate1, gpre, scale2, shift2, wr, br)


MOE_EPS = 8


def _moe_kernel(h2_ref, comb_ref, h1_ref, gate2_ref, gpost_ref, wg_ref, wu_ref, wd_ref,
                o_ref, acc_ref):
    step = pl.program_id(2)

    @pl.when(step == 0)
    def _():
        acc_ref[...] = jnp.zeros_like(acc_ref)

    t = h2_ref[0]
    comb = comb_ref[0]
    lane = lax.broadcasted_iota(jnp.int32, comb.shape, 1)
    hids = []
    for j in range(MOE_EPS):
        gate = jnp.dot(t, wg_ref[j], preferred_element_type=F32)
        up = jnp.dot(t, wu_ref[j], preferred_element_type=F32)
        hid = _silu(gate) * up
        e_lane = ROUTER_E0 + step * MOE_EPS + j
        cw = jnp.sum(jnp.where(lane == e_lane, comb, 0.0), axis=1, keepdims=True)
        hids.append((hid * cw).astype(BF16))
    wd = wd_ref[...].reshape(MOE_EPS * D_EXPERT, D_MODEL)
    acc_ref[...] += jnp.dot(jnp.concatenate(hids, axis=1), wd, preferred_element_type=F32)

    @pl.when(step == N_EXPERTS // MOE_EPS - 1)
    def _():
        o_ref[0] = h1_ref[0] + gate2_ref[0] * (_rms(acc_ref[...]) * gpost_ref[...])


def _moe(h2, comb, h1, gate2, gpost, wg, wu, wd, tm):
    bsz, seq, _ = h2.shape
    row = lambda b, i, e: (b, i, 0)
    return pl.pallas_call(
        _moe_kernel,
        grid=(bsz, seq // tm, N_EXPERTS // MOE_EPS),
        in_specs=[
            pl.BlockSpec((1, tm, D_MODEL), row),
            pl.BlockSpec((1, tm, LANES), row),
            pl.BlockSpec((1, tm, D_MODEL), row),
            pl.BlockSpec((1, 1, D_MODEL), lambda b, i, e: (b, 0, 0)),
            pl.BlockSpec((1, D_MODEL), lambda b, i, e: (0, 0)),
            pl.BlockSpec((MOE_EPS, D_MODEL, D_EXPERT), lambda b, i, e: (e, 0, 0)),
            pl.BlockSpec((MOE_EPS, D_MODEL, D_EXPERT), lambda b, i, e: (e, 0, 0)),
            pl.BlockSpec((MOE_EPS, D_EXPERT, D_MODEL), lambda b, i, e: (e, 0, 0)),
        ],
        out_specs=pl.BlockSpec((1, tm, D_MODEL), row),
        out_shape=jax.ShapeDtypeStruct((bsz, seq, D_MODEL), F32),
        scratch_shapes=[pltpu.VMEM((tm, D_MODEL), F32)],
        compiler_params=_params("parallel", "parallel", "arbitrary"),
        name="moe",
    )(h2, comb, h1, gate2, gpost, wg, wu, wd)


def _layer(x, positions, mod, g_pre_mix, w_in, w_out, g_post_mix, g_pre_ffn,
           w_group, b_group, w_expert, b_expert, w_gate_exp, w_up_exp, w_down_exp, g_post_ffn):
    bsz, seq, _ = x.shape
    shift1, scale1, gate1, shift2, scale2, gate2 = [
        m.reshape(bsz, 1, D_MODEL) for m in jnp.split(mod, N_MOD, axis=-1)]
    vec = lambda g: g.reshape(1, D_MODEL)

    a, i8, r = ATT_WIDTH, IDX_HEADS, RET_WIDTH
    offs = np.cumsum([0, a, a, a, a, IDX_HEAD_DIM, i8, r, r, r, r])
    seg = lambda j: w_in[:, offs[j]:offs[j + 1]].astype(BF16)
    wki = jnp.concatenate([seg(4), seg(4), jnp.pad(seg(5), ((0, 0), (0, LANES - i8)))], axis=1)
    weights = [seg(0), seg(1), seg(2), seg(3), wki,
               seg(6), seg(7), seg(8), seg(9)]

    tables = _rope_tables(positions, 512)
    qa, ka, vt, qi, ki, wit, qr, kr, vr, gr = _inproj(
        x, shift1, scale1, vec(g_pre_mix), weights, tables, 512)
    att = _dsa(qi, ki, wit, qa, ka, vt)
    ret = _retention(qr, kr, vr, gr)

    w_router = jnp.pad(jnp.concatenate([w_group, w_expert], axis=1),
                       ((0, 0), (0, LANES - N_GROUPS - N_EXPERTS))).astype(BF16)
    b_router = jnp.pad(jnp.concatenate([b_group, b_expert]),
                       (0, LANES - N_GROUPS - N_EXPERTS)).reshape(1, LANES)
    h1, h2, comb = _mid(att, ret, x, w_out[:ATT_WIDTH].astype(BF16), w_out[ATT_WIDTH:].astype(BF16),
                        vec(g_post_mix), gate1, vec(g_pre_ffn), scale2, shift2, w_router, b_router, 512)

    return _moe(h2, comb, h1, gate2, vec(g_post_ffn), w_gate_exp.astype(BF16),
                w_up_exp.astype(BF16), w_down_exp.astype(BF16), 1024)


def kernel(x, c, positions, g_pre_mix, w_ada, b_ada, w_in, w_out, g_post_mix, g_pre_ffn,
           w_group, b_group, w_expert, b_expert, w_gate_exp, w_up_exp, w_down_exp, g_post_ffn):
    h = x
    for l in range(w_in.shape[0]):
        mod = _ada(c, w_ada[l], b_ada[l])
        h = _layer(h, positions, mod, g_pre_mix[l], w_in[l], w_out[l], g_post_mix[l], g_pre_ffn[l],
                   w_group[l], b_group[l], w_expert[l], b_expert[l],
                   w_gate_exp[l], w_up_exp[l], w_down_exp[l], g_post_ffn[l])
    return h
```

```python
import functools
import math

import numpy as np
import jax
import jax.numpy as jnp
from jax import lax
from jax.experimental import pallas as pl
from jax.experimental.pallas import tpu as pltpu

D_MODEL = 1024
ATT_HEADS = 8
ATT_HEAD_DIM = 64
IDX_HEADS = 8
IDX_HEAD_DIM = 64
TOPK_MAX = 256
RET_HEADS = 4
RET_HEAD_DIM = 128
ROPE_THETA = 10000.0
ATT_WIDTH = ATT_HEADS * ATT_HEAD_DIM
RET_WIDTH = RET_HEADS * RET_HEAD_DIM
N_GROUPS = 4
EXPERTS_PER_GROUP = 8
N_EXPERTS = N_GROUPS * EXPERTS_PER_GROUP
D_EXPERT = 256
N_MOD = 6
EPS = 1e-6

LANES = 128
SUB = 8
VMEM_LIMIT = 56 * 1024 * 1024

DSA_TQ = 256
DSA_TK = 256
V_ROWS = 80
F32 = jnp.float32
BF16 = jnp.bfloat16
NEG_INF = float("-inf")
F32_LOWEST = float(np.finfo(np.float32).min)
INT_MIN = -2 ** 31
BF16_KEY_BITS = 16
NO_CHUNK = 2 ** 30
DENOM_FLOOR = 1e-30

_NT = (((1,), (1,)), ((), ()))
_TN = (((0,), (0,)), ((), ()))


def _params(*sem):
    return pltpu.CompilerParams(dimension_semantics=sem, vmem_limit_bytes=VMEM_LIMIT)


def _rms(x):
    return x * lax.rsqrt(jnp.mean(x * x, axis=-1, keepdims=True) + EPS)


def _silu(x):
    return x * (1.0 / (1.0 + jnp.exp(-x)))


def _ada_kernel(c_ref, w_ref, b_ref, o_ref):
    a = _silu(c_ref[...]).astype(BF16)
    o_ref[...] = jnp.dot(a, w_ref[...].astype(BF16), preferred_element_type=F32) + b_ref[...]


def _ada(c, w, b):
    bsz = c.shape[0]
    n = w.shape[1]
    tn = D_MODEL
    return pl.pallas_call(
        _ada_kernel,
        grid=(n // tn,),
        in_specs=[
            pl.BlockSpec((bsz, D_MODEL), lambda j: (0, 0)),
            pl.BlockSpec((D_MODEL, tn), lambda j: (0, j)),
            pl.BlockSpec((1, tn), lambda j: (0, j)),
        ],
        out_specs=pl.BlockSpec((bsz, tn), lambda j: (0, j)),
        out_shape=jax.ShapeDtypeStruct((bsz, n), F32),
        compiler_params=_params("arbitrary"),
        name="ada",
    )(c, w, b.reshape(1, n))


def _tables_kernel(pos_ref, f_ref, cs64_ref, sn64_ref, cs128_ref, sn128_ref):
    ang = pos_ref[0].astype(F32) * f_ref[...]
    c = jnp.cos(ang)
    s = jnp.sin(ang)
    c_sw = pltpu.roll(c, 64, 1)
    s_sw = pltpu.roll(s, 64, 1)
    lane = lax.broadcasted_iota(jnp.int32, c.shape, 1)
    lo = lane < 64
    cs128_ref[0] = jnp.where(lo, c, c_sw)
    sn128_ref[0] = jnp.where(lo, -s, s_sw)
    cs64_ref[0] = jnp.where(lo, c_sw, c)
    s64 = jnp.where(lo, s_sw, s)
    sn64_ref[0] = jnp.where((lane % 64) < 32, -s64, s64)


def _rope_tables(positions, ts):
    bsz, seq = positions.shape
    f64 = ROPE_THETA ** (-jnp.arange(32, dtype=F32) / 32)
    f128 = ROPE_THETA ** (-jnp.arange(64, dtype=F32) / 64)
    frow = jnp.concatenate([f128, f64, f64]).reshape(1, LANES)
    tab = jax.ShapeDtypeStruct((bsz, seq, LANES), F32)
    tspec = pl.BlockSpec((1, ts, LANES), lambda b, i: (b, i, 0))
    return pl.pallas_call(
        _tables_kernel,
        grid=(bsz, seq // ts),
        in_specs=[
            pl.BlockSpec((1, ts, 1), lambda b, i: (b, i, 0)),
            pl.BlockSpec((1, LANES), lambda b, i: (0, 0)),
        ],
        out_specs=[tspec] * 4,
        out_shape=[tab] * 4,
        compiler_params=_params("parallel", "parallel"),
        name="tables",
    )(positions.reshape(bsz, seq, 1), frow)


def _rope64(y, cs, sn):
    lane = lax.broadcasted_iota(jnp.int32, y.shape, 1)
    rot = jnp.where((lane % 64) < 32, pltpu.roll(y, 96, 1), pltpu.roll(y, 32, 1))
    return y * cs + rot * sn


def _rope128(y, cs, sn):
    return y * cs + pltpu.roll(y, 64, 1) * sn


def _inproj_kernel(x_ref, shift_ref, scale_ref, g_ref,
                   wqa_ref, wka_ref, wv_ref, wqi_ref, wki_ref,
                   wqr_ref, wkr_ref, wvr_ref, wgr_ref,
                   cs64_ref, sn64_ref, cs128_ref, sn128_ref,
                   qa_ref, ka_ref, vt_ref, qi_ref, ki_ref, wit_ref,
                   qr_ref, kr_ref, vr_ref, gr_ref):
    x = x_ref[0]
    h = _rms(x) * g_ref[...] * (1.0 + scale_ref[0]) + shift_ref[0]
    hb = h.astype(BF16)
    cs64, sn64 = cs64_ref[0], sn64_ref[0]
    cs128, sn128 = cs128_ref[0], sn128_ref[0]

    lane = lax.broadcasted_iota(jnp.int32, (hb.shape[0], LANES), 1)
    wide = 2 * LANES

    def pairs(w_ref, rope, cs, sn):
        for c in range(w_ref.shape[1] // wide):
            y = jnp.dot(hb, w_ref[:, c * wide:(c + 1) * wide], preferred_element_type=F32)
            for j in range(wide // LANES):
                yield 2 * c + j, rope(y[:, j * LANES:(j + 1) * LANES], cs, sn)

    def query_heads(w_ref, o_ref, mult):
        for c, y in pairs(w_ref, _rope64, cs64, sn64):
            y = y * mult
            o_ref[0, 2 * c] = jnp.where(lane < 64, y, 0.0).astype(BF16)
            o_ref[0, 2 * c + 1] = jnp.where(lane >= 64, y, 0.0).astype(BF16)

    query_heads(wqa_ref, qa_ref, ATT_HEAD_DIM ** -0.5)
    query_heads(wqi_ref, qi_ref, 1.0)
    for c, y in pairs(wka_ref, _rope64, cs64, sn64):
        ka_ref[0, c] = y.astype(BF16)

    ts = hb.shape[0]
    vt = jnp.dot(hb, wv_ref[...], preferred_element_type=F32).T
    for h in range(ATT_HEADS):
        for j in range(ts // DSA_TK):
            blk = vt[h * 64:(h + 1) * 64, j * DSA_TK:(j + 1) * DSA_TK]
            vt_ref[0, h, j, :64, :] = blk.astype(BF16)
            vt_ref[0, h, j, 64:, :] = jnp.ones((V_ROWS - 64, DSA_TK), BF16)

    yk = jnp.dot(hb, wki_ref[...], preferred_element_type=F32)
    ki_ref[0] = _rope64(yk[:, :LANES], cs64, sn64).astype(BF16)
    wit_ref[0] = yk[:, LANES:].T[:IDX_HEADS] * ((IDX_HEADS ** -0.5) * (IDX_HEAD_DIM ** -0.5))

    for c, y in pairs(wqr_ref, _rope128, cs128, sn128):
        qr_ref[0, :, c * LANES:(c + 1) * LANES] = y.astype(BF16)
    for c, y in pairs(wkr_ref, _rope128, cs128, sn128):
        kr_ref[0, :, c * LANES:(c + 1) * LANES] = (y * (RET_HEAD_DIM ** -0.5)).astype(BF16)
    vr_ref[0] = jnp.dot(hb, wvr_ref[...], preferred_element_type=F32).astype(BF16)
    gr_ref[0] = jnp.dot(hb, wgr_ref[...], preferred_element_type=F32).astype(BF16)


def _inproj(x, shift, scale, g, weights, tables, ts):
    bsz, seq, _ = x.shape
    row = lambda b, i: (b, i, 0)
    per_b = lambda b, i: (b, 0, 0)
    const2 = lambda b, i: (0, 0)
    head_major = lambda b, i: (b, 0, i, 0)

    def wspec(w):
        return pl.BlockSpec(w.shape, const2)

    hm_shape = jax.ShapeDtypeStruct((bsz, ATT_HEADS, seq, LANES), BF16)
    hm_spec = pl.BlockSpec((1, ATT_HEADS, ts, LANES), head_major)
    pair_shape = jax.ShapeDtypeStruct((bsz, ATT_HEADS // 2, seq, LANES), BF16)
    pair_spec = pl.BlockSpec((1, ATT_HEADS // 2, ts, LANES), head_major)
    wide = lambda n, dt=BF16: jax.ShapeDtypeStruct((bsz, seq, n), dt)
    wspec_out = lambda n: pl.BlockSpec((1, ts, n), row)
    return pl.pallas_call(
        _inproj_kernel,
        grid=(bsz, seq // ts),
        in_specs=[
            pl.BlockSpec((1, ts, D_MODEL), row),
            pl.BlockSpec((1, 1, D_MODEL), per_b),
            pl.BlockSpec((1, 1, D_MODEL), per_b),
            pl.BlockSpec((1, D_MODEL), const2),
            *[wspec(w) for w in weights],
            *[pl.BlockSpec((1, ts, LANES), row)] * 4,
        ],
        out_specs=[hm_spec, pair_spec,
                   pl.BlockSpec((1, ATT_HEADS, ts // DSA_TK, V_ROWS, DSA_TK), lambda b, i: (b, 0, i, 0, 0)),
                   hm_spec, wspec_out(LANES),
                   pl.BlockSpec((1, IDX_HEADS, ts), lambda b, i: (b, 0, i)),
                   wspec_out(RET_WIDTH), wspec_out(RET_WIDTH), wspec_out(RET_WIDTH), wspec_out(RET_WIDTH)],
        out_shape=[hm_shape, pair_shape,
                   jax.ShapeDtypeStruct((bsz, ATT_HEADS, seq // DSA_TK, V_ROWS, DSA_TK), BF16),
                   hm_shape, wide(LANES),
                   jax.ShapeDtypeStruct((bsz, IDX_HEADS, seq), F32),
                   wide(RET_WIDTH), wide(RET_WIDTH), wide(RET_WIDTH), wide(RET_WIDTH)],
        compiler_params=_params("parallel", "parallel"),
        name="inproj",
    )(x, shift, scale, g, *weights, *tables)


def _key_to_float(u):
    key = u ^ INT_MIN
    bits = jnp.where(key >= 0, key, key ^ 0x7FFFFFFF)
    return lax.bitcast_convert_type(bits, F32)


def _dsa_kernel(qi_ref, ki_ref, wit_ref, qa_ref, ka_ref, vt_ref, o_ref,
                sc_ref, sb_ref, crit_ref, m_ref, acc_ref, kmax_ref, qit_ref, qat_ref, *, topk):
    tq, tk = DSA_TQ, DSA_TK
    qb = pl.program_id(1)
    nk = (qb + 1) * (tq // tk)
    grp = (tk // SUB, SUB, tq)
    grp16 = (tk // (2 * SUB), 2 * SUB, tq)
    q_pos = qb * tq + lax.broadcasted_iota(jnp.int32, (SUB, tq), 1)

    def key_index(kc):
        return (kc * tk + lax.broadcasted_iota(jnp.int32, grp, 0) * SUB
                + lax.broadcasted_iota(jnp.int32, grp, 1))

    for h in range(IDX_HEADS):
        qit_ref[h] = qi_ref[0, h].astype(F32).T.astype(BF16)
        qat_ref[h] = qa_ref[0, h].astype(F32).T.astype(BF16)

    wt = wit_ref[0]

    def for_chunks(body):
        def pair(i, carry):
            body(2 * i)
            body(2 * i + 1)
            return carry
        lax.fori_loop(0, nk // 2, pair, 0)

        @pl.when(nk % 2 == 1)
        def _():
            body(nk - 1)

    def sum_chunks(body, init):
        acc = lax.fori_loop(0, nk // 2, lambda i, a: body(2 * i + 1, body(2 * i, a)), init)
        return lax.cond(nk % 2 == 1, lambda a: body(nk - 1, a), lambda a: a, acc)

    def score_chunk(kc):
        k0 = pl.multiple_of(kc * tk, tk)
        kch = ki_ref[0, pl.ds(k0, tk), :]
        acc = jnp.zeros((tk, tq), F32)
        for h in range(IDX_HEADS):
            logit = jnp.dot(kch, qit_ref[h], preferred_element_type=F32)
            acc = acc + jnp.maximum(logit, 0.0) * wt[h:h + 1, :]
        score = jnp.where(key_index(kc) <= q_pos[None], acc.reshape(grp), NEG_INF)
        sc_ref[kc] = score
        sb_ref[kc] = score.reshape(tk, tq).astype(BF16).reshape(grp16)

    for_chunks(score_chunk)

    def count(pred):
        def body(kc, acc):
            hit = jnp.where(pred(sc_ref[kc], kc), 1.0, 0.0)
            parts = [hit[g] for g in range(grp[0])]
            while len(parts) > 1:
                parts = [parts[i] + parts[i + 1] for i in range(0, len(parts), 2)]
            return acc + parts[0]
        acc = sum_chunks(body, jnp.zeros((SUB, tq), F32))
        return jnp.sum(acc, axis=0, keepdims=True)

    def rows(v):
        return jnp.broadcast_to(v, (SUB, tq))[None]

    few_keys = q_pos[:1] < topk - 1

    def count_bf16(cand):
        cb = jnp.broadcast_to(cand, (2 * SUB, tq)).astype(BF16)[None]

        def body(kc, acc):
            hit = jnp.where(sb_ref[kc] >= cb, jnp.ones((), BF16), jnp.zeros((), BF16))
            parts = [hit[g] for g in range(grp16[0])]
            while len(parts) > 1:
                parts = [parts[i] + parts[i + 1] for i in range(0, len(parts), 2)]
            return acc + parts[0].astype(F32)
        acc = sum_chunks(body, jnp.zeros((2 * SUB, tq), F32))
        return jnp.sum(acc, axis=0, keepdims=True)

    def high_body(i, u):
        trial = u | lax.shift_left(jnp.int32(1), 31 - i)
        return jnp.where(count_bf16(_key_to_float(trial)) >= topk, trial, u)

    u_hi = lax.fori_loop(0, BF16_KEY_BITS, high_body, jnp.zeros((1, tq), jnp.int32))
    base = u_hi - 2 ** (32 - BF16_KEY_BITS)

    def low_body(i, state):
        d, cnt_u = state
        trial = d | lax.shift_left(jnp.int32(1), 32 - BF16_KEY_BITS - i)
        cand = rows(_key_to_float(base + trial))
        cnt = count(lambda x, kc: x >= cand)
        accept = cnt >= topk
        return jnp.where(accept, trial, d), jnp.where(accept, cnt, cnt_u)

    d, cnt_u = lax.fori_loop(
        0, 32 - BF16_KEY_BITS + 1, low_body,
        (jnp.zeros((1, tq), jnp.int32), jnp.full((1, tq), 2.0 * topk, F32)))
    u = base + d
    thr = rows(jnp.where(few_keys, F32_LOWEST, _key_to_float(u)))
    has_excess_ties = jnp.max(jnp.where(few_keys, 0.0, cnt_u)) > topk

    def tie_plan():
        need = topk - count(lambda x, kc: x > thr)

        def chunk_body(kc, carry):
            seen, crit_chunk, seen_before = carry
            tie = jnp.where(sc_ref[kc] == thr, 1.0, 0.0)
            parts = [tie[g] for g in range(grp[0])]
            while len(parts) > 1:
                parts = [parts[i] + parts[i + 1] for i in range(0, len(parts), 2)]
            seen_now = seen + jnp.sum(parts[0], axis=0, keepdims=True)
            reached = jnp.where((crit_chunk == NO_CHUNK) & (seen_now >= need), 1.0, 0.0)
            crit_ref[...] = crit_ref[...] + rows(reached) * (tie - crit_ref[...])
            return (seen_now, jnp.where(reached > 0.5, kc, crit_chunk),
                    jnp.where(reached > 0.5, seen, seen_before))

        crit_ref[...] = jnp.zeros(crit_ref.shape, F32)
        zero = jnp.zeros((1, tq), F32)
        _, crit_chunk, seen_before = lax.fori_loop(
            0, nk, chunk_body, (zero, jnp.full((1, tq), NO_CHUNK, jnp.int32), zero))
        tie = crit_ref[...].reshape(tk, tq)
        lower = (lax.broadcasted_iota(jnp.int32, (tk, tk), 0)
                 >= lax.broadcasted_iota(jnp.int32, (tk, tk), 1))
        rank = jnp.dot(jnp.where(lower, 1.0, 0.0).astype(BF16), tie.astype(BF16),
                       preferred_element_type=F32)
        crit_ref[...] = jnp.where(rank <= need - seen_before, tie, 0.0).reshape(grp)
        return crit_chunk

    def no_excess_ties():
        crit_ref[...] = jnp.zeros(crit_ref.shape, F32)
        return jnp.full((1, tq), NO_CHUNK, jnp.int32)

    crit_chunk = lax.cond(has_excess_ties, tie_plan, no_excess_ties)

    def mask_body(kc, c):
        x = sc_ref[kc]
        tie_kept = (rows(jnp.where(kc < crit_chunk, 1.0, 0.0))
                    + rows(jnp.where(kc == crit_chunk, 1.0, 0.0)) * crit_ref[...])
        keep = (x > thr) | ((x == thr) & (tie_kept > 0.5))
        sc_ref[kc] = jnp.where(keep, 0.0, NEG_INF)
        return c

    lax.fori_loop(0, nk, mask_body, 0)

    def logits(kc, h):
        k0 = pl.multiple_of(kc * tk, tk)
        kch = ka_ref[0, h // 2, pl.ds(k0, tk), :]
        s = jnp.dot(kch, qat_ref[h], preferred_element_type=F32)
        return s.reshape(grp) + sc_ref[kc]

    def max_chunk(kc, m):
        return jnp.stack([jnp.maximum(m[h], jnp.max(logits(kc, h), axis=0))
                          for h in range(ATT_HEADS)])

    def pv_chunk(kc):
        ss = [logits(kc, h) for h in range(ATT_HEADS)]
        for h in range(ATT_HEADS):
            p = jnp.exp(ss[h] - m_ref[h][None]).reshape(tk, tq).astype(BF16)
            acc_ref[h] += jnp.dot(vt_ref[0, h, kc], p, preferred_element_type=F32)

    def exp_pv_pass():
        acc_ref[...] = jnp.zeros(acc_ref.shape, F32)
        for_chunks(pv_chunk)

    ones_r = jnp.ones((SUB, LANES), BF16)
    lane_r = lax.broadcasted_iota(jnp.int32, (SUB, LANES), 1)

    @pl.when(qb == 0)
    def _():
        for h in range(ATT_HEADS):
            k = ka_ref[0, h // 2]
            head_lanes = jnp.where((lane_r >= 64) == (h % 2 == 1), 1.0, 0.0).astype(BF16)
            k_norm2 = lax.dot_general(head_lanes, k * k, _NT, preferred_element_type=F32)
            kmax_ref[h] = jnp.broadcast_to(jnp.max(k_norm2, axis=1, keepdims=True), (SUB, tq))

    for h in range(ATT_HEADS):
        q = qa_ref[0, h]
        q_norm2 = lax.dot_general(ones_r, q * q, _NT, preferred_element_type=F32)
        m_ref[h] = jnp.sqrt(q_norm2 * kmax_ref[h])
    exp_pv_pass()
    denom_min = jnp.min(jnp.stack([acc_ref[h][64:65] for h in range(ATT_HEADS)]))

    @pl.when(jnp.logical_not(denom_min > DENOM_FLOOR))
    def _():
        m8 = lax.fori_loop(0, nk, max_chunk, jnp.full((ATT_HEADS, SUB, tq), NEG_INF, F32))
        m_ref[...] = jnp.broadcast_to(jnp.max(m8, axis=1, keepdims=True), m8.shape)
        exp_pv_pass()

    outs = []
    for h in range(ATT_HEADS):
        acc = acc_ref[h]
        outs.append(acc[:64] / acc[64:65])
    o_ref[0] = jnp.concatenate(outs, axis=0).T.astype(BF16)


def _dsa(qi, ki, wit, qa, ka, vt):
    bsz, _, seq, _ = qi.shape
    tq, tk = DSA_TQ, DSA_TK
    topk = min(TOPK_MAX, seq // 4)
    assert seq % tq == 0 and tq % tk == 0 and tk >= topk
    blk_q = lambda b, i: (b, 0, i, 0)
    return pl.pallas_call(
        functools.partial(_dsa_kernel, topk=topk),
        grid=(bsz, seq // tq),
        in_specs=[
            pl.BlockSpec((1, IDX_HEADS, tq, LANES), blk_q),
            pl.BlockSpec((1, seq, LANES), lambda b, i: (b, 0, 0)),
            pl.BlockSpec((1, IDX_HEADS, tq), lambda b, i: (b, 0, i)),
            pl.BlockSpec((1, ATT_HEADS, tq, LANES), blk_q),
            pl.BlockSpec((1, ATT_HEADS // 2, seq, LANES), lambda b, i: (b, 0, 0, 0)),
            pl.BlockSpec((1, ATT_HEADS, seq // tk, V_ROWS, tk), lambda b, i: (b, 0, 0, 0, 0)),
        ],
        out_specs=pl.BlockSpec((1, tq, ATT_WIDTH), lambda b, i: (b, i, 0)),
        out_shape=jax.ShapeDtypeStruct((bsz, seq, ATT_WIDTH), BF16),
        scratch_shapes=[
            pltpu.VMEM((seq // tk, tk // SUB, SUB, tq), F32),
            pltpu.VMEM((seq // tk, tk // (2 * SUB), 2 * SUB, tq), BF16),
            pltpu.VMEM((tk // SUB, SUB, tq), F32),
            pltpu.VMEM((ATT_HEADS, SUB, tq), F32),
            pltpu.VMEM((ATT_HEADS, V_ROWS, tq), F32),
            pltpu.VMEM((ATT_HEADS, SUB, tq), F32),
            pltpu.VMEM((IDX_HEADS, LANES, tq), BF16),
            pltpu.VMEM((ATT_HEADS, LANES, tq), BF16),
        ],
        compiler_params=_params("parallel", "arbitrary"),
        name="dsa",
    )(qi, ki, wit, qa, ka, vt)


RET_CHUNK = 256


def _ret_kernel(q_ref, k_ref, v_ref, g_ref, o_ref, state_ref, decay_ref):
    c = RET_CHUNK
    n = pl.program_id(1)
    row = lax.broadcasted_iota(jnp.int32, (c, c), 0)
    col = lax.broadcasted_iota(jnp.int32, (c, c), 1)
    idx = lax.broadcasted_iota(jnp.int32, (c, 1), 0).astype(F32)

    @pl.when(n == 0)
    def _():
        state_ref[...] = jnp.zeros_like(state_ref)
        for h in range(RET_HEADS):
            lg = math.log(1.0 - 2.0 ** (-5.0 - h))
            diff = (row - col).astype(F32)
            decay_ref[h] = jnp.where(row >= col, jnp.exp(lg * jnp.maximum(diff, 0.0)), 0.0)

    for h in range(RET_HEADS):
        lg = math.log(1.0 - 2.0 ** (-5.0 - h))
        sl = slice(h * LANES, (h + 1) * LANES)
        q = q_ref[0, :, sl]
        k = k_ref[0, :, sl]
        v = v_ref[0, :, sl]
        inner = lax.dot_general(q, k, _NT, preferred_element_type=F32) * decay_ref[h]
        o = jnp.dot(inner.astype(BF16), v, preferred_element_type=F32)
        qd = (q.astype(F32) * jnp.exp(lg * (idx + 1.0))).astype(BF16)
        state = state_ref[h]
        o = o + jnp.dot(qd, state.astype(BF16), preferred_element_type=F32)
        kd = (k.astype(F32) * jnp.exp(lg * (c - 1.0 - idx))).astype(BF16)
        kv = lax.dot_general(kd, v, _TN, preferred_element_type=F32)
        state_ref[h] = state * math.exp(lg * c) + kv
        gate = _silu(g_ref[0, :, sl].astype(F32))
        o_ref[0, :, sl] = (gate * _rms(o)).astype(BF16)


def _retention(qr, kr, vr, gr):
    bsz, seq, _ = qr.shape
    c = RET_CHUNK
    spec = pl.BlockSpec((1, c, RET_WIDTH), lambda b, i: (b, i, 0))
    return pl.pallas_call(
        _ret_kernel,
        grid=(bsz, seq // c),
        in_specs=[spec] * 4,
        out_specs=spec,
        out_shape=jax.ShapeDtypeStruct((bsz, seq, RET_WIDTH), BF16),
        scratch_shapes=[
            pltpu.VMEM((RET_HEADS, RET_HEAD_DIM, RET_HEAD_DIM), F32),
            pltpu.VMEM((RET_HEADS, c, c), F32),
        ],
        compiler_params=_params("arbitrary", "arbitrary"),
        name="ret",
    )(qr, kr, vr, gr)


ROUTER_E0 = N_GROUPS


def _mid_kernel(att_ref, ret_ref, x_ref, woa_ref, wor_ref, gpost_ref, gate1_ref,
                gpre_ref, scale2_ref, shift2_ref, wr_ref, br_ref,
                h1_ref, h2_ref, comb_ref):
    mix = jnp.dot(att_ref[0], woa_ref[...], preferred_element_type=F32)
    mix = mix + jnp.dot(ret_ref[0], wor_ref[...], preferred_element_type=F32)
    h1 = x_ref[0] + gate1_ref[0] * (_rms(mix) * gpost_ref[...])
    h1_ref[0] = h1
    h2 = (_rms(h1) * gpre_ref[...] * (1.0 + scale2_ref[0]) + shift2_ref[0]).astype(BF16)
    h2_ref[0] = h2

    logits = jnp.dot(h2, wr_ref[...], preferred_element_type=F32) + br_ref[...]
    lane = lax.broadcasted_iota(jnp.int32, logits.shape, 1)
    big = jnp.int32(LANES)

    def first_argmax(vals, vmax):
        return jnp.min(jnp.where(vals == vmax, lane, big), axis=1, keepdims=True)

    gl = jnp.where(lane < N_GROUPS, logits, NEG_INF)
    gexp = jnp.exp(gl - jnp.max(gl, axis=1, keepdims=True))
    gprob = gexp / jnp.sum(gexp, axis=1, keepdims=True)
    g_top = jnp.max(gprob, axis=1, keepdims=True)
    g_sel = first_argmax(gprob, g_top)
    e_lo = ROUTER_E0 + g_sel * EXPERTS_PER_GROUP
    in_group = (lane >= e_lo) & (lane < e_lo + EXPERTS_PER_GROUP)
    el = jnp.where(in_group, logits, NEG_INF)
    eexp = jnp.exp(el - jnp.max(el, axis=1, keepdims=True))
    eprob = jnp.where(in_group, eexp / jnp.sum(eexp, axis=1, keepdims=True), -1.0)
    top1 = jnp.max(eprob, axis=1, keepdims=True)
    idx1 = first_argmax(eprob, top1)
    rest = jnp.where(lane == idx1, -1.0, eprob)
    top2 = jnp.max(rest, axis=1, keepdims=True)
    idx2 = first_argmax(rest, top2)
    norm = g_top / (top1 + top2)
    comb_ref[0] = jnp.where(lane == idx1, top1 * norm, 0.0) + jnp.where(lane == idx2, top2 * norm, 0.0)


def _mid(att, ret, x, woa, wor, gpost, gate1, gpre, scale2, shift2, wr, br, ts):
    bsz, seq, _ = x.shape
    row = lambda b, i: (b, i, 0)
    per_b = lambda b, i: (b, 0, 0)
    const2 = lambda b, i: (0, 0)
    vec = pl.BlockSpec((1, D_MODEL), const2)
    bvec = pl.BlockSpec((1, 1, D_MODEL), per_b)
    return pl.pallas_call(
        _mid_kernel,
        grid=(bsz, seq // ts),
        in_specs=[
            pl.BlockSpec((1, ts, ATT_WIDTH), row),
            pl.BlockSpec((1, ts, RET_WIDTH), row),
            pl.BlockSpec((1, ts, D_MODEL), row),
            pl.BlockSpec(woa.shape, const2),
            pl.BlockSpec(wor.shape, const2),
            vec, bvec, vec, bvec, bvec,
            pl.BlockSpec(wr.shape, const2),
            pl.BlockSpec((1, LANES), const2),
        ],
        out_specs=[pl.BlockSpec((1, ts, D_MODEL), row),
                   pl.BlockSpec((1, ts, D_MODEL), row),
                   pl.BlockSpec((1, ts, LANES), row)],
        out_shape=[jax.ShapeDtypeStruct((bsz, seq, D_MODEL), F32),
                   jax.ShapeDtypeStruct((bsz, seq, D_MODEL), BF16),
                   jax.ShapeDtypeStruct((bsz, seq, LANES), F32)],
        compiler_params=_params("parallel", "parallel"),
        name="mid",
    )(att, ret, x, woa, wor, gpost, gate1, gpre, scale2, shift2, wr, br)


MOE_EPS = 8


def _moe_kernel(h2_ref, comb_ref, h1_ref, gate2_ref, gpost_ref, wg_ref, wu_ref, wd_ref,
                o_ref, acc_ref):
    step = pl.program_id(2)

    @pl.when(step == 0)
    def _():
        acc_ref[...] = jnp.zeros_like(acc_ref)

    t = h2_ref[0]
    comb = comb_ref[0]
    lane = lax.broadcasted_iota(jnp.int32, comb.shape, 1)
    hids = []
    for j in range(MOE_EPS):
        gate = jnp.dot(t, wg_ref[j], preferred_element_type=F32)
        up = jnp.dot(t, wu_ref[j], preferred_element_type=F32)
        hid = _silu(gate) * up
        e_lane = ROUTER_E0 + step * MOE_EPS + j
        cw = jnp.sum(jnp.where(lane == e_lane, comb, 0.0), axis=1, keepdims=True)
        hids.append((hid * cw).astype(BF16))
    wd = wd_ref[...].reshape(MOE_EPS * D_EXPERT, D_MODEL)
    acc_ref[...] += jnp.dot(jnp.concatenate(hids, axis=1), wd, preferred_element_type=F32)

    @pl.when(step == N_EXPERTS // MOE_EPS - 1)
    def _():
        o_ref[0] = h1_ref[0] + gate2_ref[0] * (_rms(acc_ref[...]) * gpost_ref[...])


def _moe(h2, comb, h1, gate2, gpost, wg, wu, wd, tm):
    bsz, seq, _ = h2.shape
    row = lambda b, i, e: (b, i, 0)
    return pl.pallas_call(
        _moe_kernel,
        grid=(bsz, seq // tm, N_EXPERTS // MOE_EPS),
        in_specs=[
            pl.BlockSpec((1, tm, D_MODEL), row),
            pl.BlockSpec((1, tm, LANES), row),
            pl.BlockSpec((1, tm, D_MODEL), row),
            pl.BlockSpec((1, 1, D_MODEL), lambda b, i, e: (b, 0, 0)),
            pl.BlockSpec((1, D_MODEL), lambda b, i, e: (0, 0)),
            pl.BlockSpec((MOE_EPS, D_MODEL, D_EXPERT), lambda b, i, e: (e, 0, 0)),
            pl.BlockSpec((MOE_EPS, D_MODEL, D_EXPERT), lambda b, i, e: (e, 0, 0)),
            pl.BlockSpec((MOE_EPS, D_EXPERT, D_MODEL), lambda b, i, e: (e, 0, 0)),
        ],
        out_specs=pl.BlockSpec((1, tm, D_MODEL), row),
        out_shape=jax.ShapeDtypeStruct((bsz, seq, D_MODEL), F32),
        scratch_shapes=[pltpu.VMEM((tm, D_MODEL), F32)],
        compiler_params=_params("parallel", "parallel", "arbitrary"),
        name="moe",
    )(h2, comb, h1, gate2, gpost, wg, wu, wd)


def _layer(x, positions, mod, g_pre_mix, w_in, w_out, g_post_mix, g_pre_ffn,
           w_group, b_group, w_expert, b_expert, w_gate_exp, w_up_exp, w_down_exp, g_post_ffn):
    bsz, seq, _ = x.shape
    shift1, scale1, gate1, shift2, scale2, gate2 = [
        m.reshape(bsz, 1, D_MODEL) for m in jnp.split(mod, N_MOD, axis=-1)]
    vec = lambda g: g.reshape(1, D_MODEL)

    a, i8, r = ATT_WIDTH, IDX_HEADS, RET_WIDTH
    offs = np.cumsum([0, a, a, a, a, IDX_HEAD_DIM, i8, r, r, r, r])
    seg = lambda j: w_in[:, offs[j]:offs[j + 1]].astype(BF16)
    wki = jnp.concatenate([seg(4), seg(4), jnp.pad(seg(5), ((0, 0), (0, LANES - i8)))], axis=1)
    weights = [seg(0), seg(1), seg(2), seg(3), wki,
               seg(6), seg(7), seg(8), seg(9)]

    tables = _rope_tables(positions, 512)
    qa, ka, vt, qi, ki, wit, qr, kr, vr, gr = _inproj(
        x, shift1, scale1, vec(g_pre_mix), weights, tables, 512)
    att = _dsa(qi, ki, wit, qa, ka, vt)
    ret = _retention(qr, kr, vr, gr)

    w_router = jnp.pad(jnp.concatenate([w_group, w_expert], axis=1),
                       ((0, 0), (0, LANES - N_GROUPS - N_EXPERTS))).astype(BF16)
    b_router = jnp.pad(jnp.concatenate([b_group, b_expert]),
                       (0, LANES - N_GROUPS - N_EXPERTS)).reshape(1, LANES)
    h1, h2, comb = _mid(att, ret, x, w_out[:ATT_WIDTH].astype(BF16), w_out[ATT_WIDTH:].astype(BF16),
                        vec(g_post_mix), gate1, vec(g_pre_ffn), scale2, shift2, w_router, b_router, 512)

    return _moe(h2, comb, h1, gate2, vec(g_post_ffn), w_gate_exp.astype(BF16),
                w_up_exp.astype(BF16), w_down_exp.astype(BF16), 1024)


def kernel(x, c, positions, g_pre_mix, w_ada, b_ada, w_in, w_out, g_post_mix, g_pre_ffn,
           w_group, b_group, w_expert, b_expert, w_gate_exp, w_up_exp, w_down_exp, g_post_ffn):
    h = x
    for l in range(w_in.shape[0]):
        mod = _ada(c, w_ada[l], b_ada[l])
        h = _layer(h, positions, mod, g_pre_mix[l], w_in[l], w_out[l], g_post_mix[l], g_pre_ffn[l],
                   w_group[l], b_group[l], w_expert[l], b_expert[l],
                   w_gate_exp[l], w_up_exp[l], w_down_exp[l], g_post_ffn[l])
    return h
```

```python
import functools
import math

import numpy as np
import jax
import jax.numpy as jnp
from jax import lax
from jax.experimental import pallas as pl
from jax.experimental.pallas import tpu as pltpu

D_MODEL = 1024
ATT_HEADS = 8
ATT_HEAD_DIM = 64
IDX_HEADS = 8
IDX_HEAD_DIM = 64
TOPK_MAX = 256
RET_HEADS = 4
RET_HEAD_DIM = 128
ROPE_THETA = 10000.0
ATT_WIDTH = ATT_HEADS * ATT_HEAD_DIM
RET_WIDTH = RET_HEADS * RET_HEAD_DIM
N_GROUPS = 4
EXPERTS_PER_GROUP = 8
N_EXPERTS = N_GROUPS * EXPERTS_PER_GROUP
D_EXPERT = 256
N_MOD = 6
EPS = 1e-6

LANES = 128
SUB = 8
VMEM_LIMIT = 56 * 1024 * 1024

DSA_TQ = 256
DSA_TK = 256
V_ROWS = 80
F32 = jnp.float32
BF16 = jnp.bfloat16
NEG_INF = float("-inf")
F32_LOWEST = float(np.finfo(np.float32).min)
INT_MIN = -2 ** 31
BF16_KEY_BITS = 16
NO_CHUNK = 2 ** 30
DENOM_FLOOR = 1e-30

_NT = (((1,), (1,)), ((), ()))
_TN = (((0,), (0,)), ((), ()))


def _params(*sem):
    return pltpu.CompilerParams(dimension_semantics=sem, vmem_limit_bytes=VMEM_LIMIT)


def _rms(x):
    return x * lax.rsqrt(jnp.mean(x * x, axis=-1, keepdims=True) + EPS)


def _silu(x):
    return x * (1.0 / (1.0 + jnp.exp(-x)))


def _ada_kernel(c_ref, w_ref, b_ref, o_ref):
    a = _silu(c_ref[...]).astype(BF16)
    o_ref[...] = jnp.dot(a, w_ref[...].astype(BF16), preferred_element_type=F32) + b_ref[...]


def _ada(c, w, b):
    bsz = c.shape[0]
    n = w.shape[1]
    tn = D_MODEL
    return pl.pallas_call(
        _ada_kernel,
        grid=(n // tn,),
        in_specs=[
            pl.BlockSpec((bsz, D_MODEL), lambda j: (0, 0)),
            pl.BlockSpec((D_MODEL, tn), lambda j: (0, j)),
            pl.BlockSpec((1, tn), lambda j: (0, j)),
        ],
        out_specs=pl.BlockSpec((bsz, tn), lambda j: (0, j)),
        out_shape=jax.ShapeDtypeStruct((bsz, n), F32),
        compiler_params=_params("arbitrary"),
        name="ada",
    )(c, w, b.reshape(1, n))


def _tables_kernel(pos_ref, f_ref, cs64_ref, sn64_ref, cs128_ref, sn128_ref):
    ang = pos_ref[0].astype(F32) * f_ref[...]
    c = jnp.cos(ang)
    s = jnp.sin(ang)
    c_sw = pltpu.roll(c, 64, 1)
    s_sw = pltpu.roll(s, 64, 1)
    lane = lax.broadcasted_iota(jnp.int32, c.shape, 1)
    lo = lane < 64
    cs128_ref[0] = jnp.where(lo, c, c_sw)
    sn128_ref[0] = jnp.where(lo, -s, s_sw)
    cs64_ref[0] = jnp.where(lo, c_sw, c)
    s64 = jnp.where(lo, s_sw, s)
    sn64_ref[0] = jnp.where((lane % 64) < 32, -s64, s64)


def _rope_tables(positions, ts):
    bsz, seq = positions.shape
    f64 = ROPE_THETA ** (-jnp.arange(32, dtype=F32) / 32)
    f128 = ROPE_THETA ** (-jnp.arange(64, dtype=F32) / 64)
    frow = jnp.concatenate([f128, f64, f64]).reshape(1, LANES)
    tab = jax.ShapeDtypeStruct((bsz, seq, LANES), F32)
    tspec = pl.BlockSpec((1, ts, LANES), lambda b, i: (b, i, 0))
    return pl.pallas_call(
        _tables_kernel,
        grid=(bsz, seq // ts),
        in_specs=[
            pl.BlockSpec((1, ts, 1), lambda b, i: (b, i, 0)),
            pl.BlockSpec((1, LANES), lambda b, i: (0, 0)),
        ],
        out_specs=[tspec] * 4,
        out_shape=[tab] * 4,
        compiler_params=_params("parallel", "parallel"),
        name="tables",
    )(positions.reshape(bsz, seq, 1), frow)


def _rope64(y, cs, sn):
    lane = lax.broadcasted_iota(jnp.int32, y.shape, 1)
    rot = jnp.where((lane % 64) < 32, pltpu.roll(y, 96, 1), pltpu.roll(y, 32, 1))
    return y * cs + rot * sn


def _rope128(y, cs, sn):
    return y * cs + pltpu.roll(y, 64, 1) * sn


def _inproj_kernel(x_ref, shift_ref, scale_ref, g_ref,
                   wqa_ref, wka_ref, wv_ref, wqi_ref, wki_ref,
                   wqr_ref, wkr_ref, wvr_ref, wgr_ref,
                   cs64_ref, sn64_ref, cs128_ref, sn128_ref,
                   qa_ref, ka_ref, vt_ref, qi_ref, ki_ref, wit_ref,
                   qr_ref, kr_ref, vr_ref, gr_ref):
    x = x_ref[0]
    h = _rms(x) * g_ref[...] * (1.0 + scale_ref[0]) + shift_ref[0]
    hb = h.astype(BF16)
    cs64, sn64 = cs64_ref[0], sn64_ref[0]
    cs128, sn128 = cs128_ref[0], sn128_ref[0]

    lane = lax.broadcasted_iota(jnp.int32, (hb.shape[0], LANES), 1)
    wide = 2 * LANES

    def pairs(w_ref, rope, cs, sn):
        for c in range(w_ref.shape[1] // wide):
            y = jnp.dot(hb, w_ref[:, c * wide:(c + 1) * wide], preferred_element_type=F32)
            for j in range(wide // LANES):
                yield 2 * c + j, rope(y[:, j * LANES:(j + 1) * LANES], cs, sn)

    def query_heads(w_ref, o_ref, mult):
        for c, y in pairs(w_ref, _rope64, cs64, sn64):
            y = y * mult
            o_ref[0, 2 * c] = jnp.where(lane < 64, y, 0.0).astype(BF16)
            o_ref[0, 2 * c + 1] = jnp.where(lane >= 64, y, 0.0).astype(BF16)

    query_heads(wqa_ref, qa_ref, ATT_HEAD_DIM ** -0.5)
    query_heads(wqi_ref, qi_ref, 1.0)
    for c, y in pairs(wka_ref, _rope64, cs64, sn64):
        ka_ref[0, c] = y.astype(BF16)

    ts = hb.shape[0]
    vt = jnp.dot(hb, wv_ref[...], preferred_element_type=F32).T
    for h in range(ATT_HEADS):
        for j in range(ts // DSA_TK):
            blk = vt[h * 64:(h + 1) * 64, j * DSA_TK:(j + 1) * DSA_TK]
            vt_ref[0, h, j, :64, :] = blk.astype(BF16)
            vt_ref[0, h, j, 64:, :] = jnp.ones((V_ROWS - 64, DSA_TK), BF16)

    yk = jnp.dot(hb, wki_ref[...], preferred_element_type=F32)
    ki_ref[0] = _rope64(yk[:, :LANES], cs64, sn64).astype(BF16)
    wit_ref[0] = yk[:, LANES:].T[:IDX_HEADS] * ((IDX_HEADS ** -0.5) * (IDX_HEAD_DIM ** -0.5))

    for c, y in pairs(wqr_ref, _rope128, cs128, sn128):
        qr_ref[0, :, c * LANES:(c + 1) * LANES] = y.astype(BF16)
    for c, y in pairs(wkr_ref, _rope128, cs128, sn128):
        kr_ref[0, :, c * LANES:(c + 1) * LANES] = (y * (RET_HEAD_DIM ** -0.5)).astype(BF16)
    vr_ref[0] = jnp.dot(hb, wvr_ref[...], preferred_element_type=F32).astype(BF16)
    gr_ref[0] = jnp.dot(hb, wgr_ref[...], preferred_element_type=F32).astype(BF16)


def _inproj(x, shift, scale, g, weights, tables, ts):
    bsz, seq, _ = x.shape
    row = lambda b, i: (b, i, 0)
    per_b = lambda b, i: (b, 0, 0)
    const2 = lambda b, i: (0, 0)
    head_major = lambda b, i: (b, 0, i, 0)

    def wspec(w):
        return pl.BlockSpec(w.shape, const2)

    hm_shape = jax.ShapeDtypeStruct((bsz, ATT_HEADS, seq, LANES), BF16)
    hm_spec = pl.BlockSpec((1, ATT_HEADS, ts, LANES), head_major)
    pair_shape = jax.ShapeDtypeStruct((bsz, ATT_HEADS // 2, seq, LANES), BF16)
    pair_spec = pl.BlockSpec((1, ATT_HEADS // 2, ts, LANES), head_major)
    wide = lambda n, dt=BF16: jax.ShapeDtypeStruct((bsz, seq, n), dt)
    wspec_out = lambda n: pl.BlockSpec((1, ts, n), row)
    return pl.pallas_call(
        _inproj_kernel,
        grid=(bsz, seq // ts),
        in_specs=[
            pl.BlockSpec((1, ts, D_MODEL), row),
            pl.BlockSpec((1, 1, D_MODEL), per_b),
            pl.BlockSpec((1, 1, D_MODEL), per_b),
            pl.BlockSpec((1, D_MODEL), const2),
            *[wspec(w) for w in weights],
            *[pl.BlockSpec((1, ts, LANES), row)] * 4,
        ],
        out_specs=[hm_spec, pair_spec,
                   pl.BlockSpec((1, ATT_HEADS, ts // DSA_TK, V_ROWS, DSA_TK), lambda b, i: (b, 0, i, 0, 0)),
                   hm_spec, wspec_out(LANES),
                   pl.BlockSpec((1, IDX_HEADS, ts), lambda b, i: (b, 0, i)),
                   wspec_out(RET_WIDTH), wspec_out(RET_WIDTH), wspec_out(RET_WIDTH), wspec_out(RET_WIDTH)],
        out_shape=[hm_shape, pair_shape,
                   jax.ShapeDtypeStruct((bsz, ATT_HEADS, seq // DSA_TK, V_ROWS, DSA_TK), BF16),
                   hm_shape, wide(LANES),
                   jax.ShapeDtypeStruct((bsz, IDX_HEADS, seq), F32),
                   wide(RET_WIDTH), wide(RET_WIDTH), wide(RET_WIDTH), wide(RET_WIDTH)],
        compiler_params=_params("parallel", "parallel"),
        name="inproj",
    )(x, shift, scale, g, *weights, *tables)


def _key_to_float(u):
    key = u ^ INT_MIN
    bits = jnp.where(key >= 0, key, key ^ 0x7FFFFFFF)
    return lax.bitcast_convert_type(bits, F32)


def _dsa_kernel(qi_ref, ki_ref, wit_ref, qa_ref, ka_ref, vt_ref, o_ref,
                sc_ref, sb_ref, crit_ref, m_ref, acc_ref, kmax_ref, qit_ref, qat_ref, *, topk):
    tq, tk = DSA_TQ, DSA_TK
    qb = pl.program_id(1)
    nk = (qb + 1) * (tq // tk)
    grp = (tk // SUB, SUB, tq)
    grp16 = (tk // (2 * SUB), 2 * SUB, tq)
    q_pos = qb * tq + lax.broadcasted_iota(jnp.int32, (SUB, tq), 1)

    def key_index(kc):
        return (kc * tk + lax.broadcasted_iota(jnp.int32, grp, 0) * SUB
                + lax.broadcasted_iota(jnp.int32, grp, 1))

    for h in range(IDX_HEADS):
        qit_ref[h] = qi_ref[0, h].astype(F32).T.astype(BF16)
        qat_ref[h] = qa_ref[0, h].astype(F32).T.astype(BF16)

    wt = wit_ref[0]

    def for_chunks(body):
        def quad(i, carry):
            for j in range(4):
                body(4 * i + j)
            return carry
        lax.fori_loop(0, nk // 4, quad, 0)

        @pl.when(nk % 4 >= 2)
        def _():
            body(nk // 4 * 4)
            body(nk // 4 * 4 + 1)

        @pl.when(nk % 2 == 1)
        def _():
            body(nk - 1)

    def sum_chunks(body, init):
        acc = lax.fori_loop(0, nk // 2, lambda i, a: body(2 * i + 1, body(2 * i, a)), init)
        return lax.cond(nk % 2 == 1, lambda a: body(nk - 1, a), lambda a: a, acc)

    def score_chunk(kc):
        k0 = pl.multiple_of(kc * tk, tk)
        kch = ki_ref[0, pl.ds(k0, tk), :]
        acc = jnp.zeros((tk, tq), F32)
        for h in range(IDX_HEADS):
            logit = jnp.dot(kch, qit_ref[h], preferred_element_type=F32)
            acc = acc + jnp.maximum(logit, 0.0) * wt[h:h + 1, :]
        score = jnp.where(key_index(kc) <= q_pos[None], acc.reshape(grp), NEG_INF)
        sc_ref[kc] = score
        sb_ref[kc] = score.reshape(tk, tq).astype(BF16).reshape(grp16)

    for_chunks(score_chunk)

    def count(pred):
        def body(kc, acc):
            hit = jnp.where(pred(sc_ref[kc], kc), 1.0, 0.0)
            parts = [hit[g] for g in range(grp[0])]
            while len(parts) > 1:
                parts = [parts[i] + parts[i + 1] for i in range(0, len(parts), 2)]
            return acc + parts[0]
        acc = sum_chunks(body, jnp.zeros((SUB, tq), F32))
        return jnp.sum(acc, axis=0, keepdims=True)

    def rows(v):
        return jnp.broadcast_to(v, (SUB, tq))[None]

    few_keys = q_pos[:1] < topk - 1

    def count_bf16(cand):
        cb = jnp.broadcast_to(cand, (2 * SUB, tq)).astype(BF16)[None]

        def body(kc, acc):
            hit = jnp.where(sb_ref[kc] >= cb, jnp.ones((), BF16), jnp.zeros((), BF16))
            parts = [hit[g] for g in range(grp16[0])]
            while len(parts) > 1:
                parts = [parts[i] + parts[i + 1] for i in range(0, len(parts), 2)]
            return acc + parts[0].astype(F32)
        acc = sum_chunks(body, jnp.zeros((2 * SUB, tq), F32))
        return jnp.sum(acc, axis=0, keepdims=True)

    def high_body(i, u):
        trial = u | lax.shift_left(jnp.int32(1), 31 - i)
        return jnp.where(count_bf16(_key_to_float(trial)) >= topk, trial, u)

    u_hi = lax.fori_loop(0, BF16_KEY_BITS, high_body, jnp.zeros((1, tq), jnp.int32))
    base = u_hi - 2 ** (32 - BF16_KEY_BITS)

    def low_body(i, state):
        d, cnt_u = state
        trial = d | lax.shift_left(jnp.int32(1), 32 - BF16_KEY_BITS - i)
        cand = rows(_key_to_float(base + trial))
        cnt = count(lambda x, kc: x >= cand)
        accept = cnt >= topk
        return jnp.where(accept, trial, d), jnp.where(accept, cnt, cnt_u)

    d, cnt_u = lax.fori_loop(
        0, 32 - BF16_KEY_BITS + 1, low_body,
        (jnp.zeros((1, tq), jnp.int32), jnp.full((1, tq), 2.0 * topk, F32)))
    u = base + d
    thr = rows(jnp.where(few_keys, F32_LOWEST, _key_to_float(u)))
    has_excess_ties = jnp.max(jnp.where(few_keys, 0.0, cnt_u)) > topk

    def tie_plan():
        need = topk - count(lambda x, kc: x > thr)

        def chunk_body(kc, carry):
            seen, crit_chunk, seen_before = carry
            tie = jnp.where(sc_ref[kc] == thr, 1.0, 0.0)
            parts = [tie[g] for g in range(grp[0])]
            while len(parts) > 1:
                parts = [parts[i] + parts[i + 1] for i in range(0, len(parts), 2)]
            seen_now = seen + jnp.sum(parts[0], axis=0, keepdims=True)
            reached = jnp.where((crit_chunk == NO_CHUNK) & (seen_now >= need), 1.0, 0.0)
            crit_ref[...] = crit_ref[...] + rows(reached) * (tie - crit_ref[...])
            return (seen_now, jnp.where(reached > 0.5, kc, crit_chunk),
                    jnp.where(reached > 0.5, seen, seen_before))

        crit_ref[...] = jnp.zeros(crit_ref.shape, F32)
        zero = jnp.zeros((1, tq), F32)
        _, crit_chunk, seen_before = lax.fori_loop(
            0, nk, chunk_body, (zero, jnp.full((1, tq), NO_CHUNK, jnp.int32), zero))
        tie = crit_ref[...].reshape(tk, tq)
        lower = (lax.broadcasted_iota(jnp.int32, (tk, tk), 0)
                 >= lax.broadcasted_iota(jnp.int32, (tk, tk), 1))
        rank = jnp.dot(jnp.where(lower, 1.0, 0.0).astype(BF16), tie.astype(BF16),
                       preferred_element_type=F32)
        crit_ref[...] = jnp.where(rank <= need - seen_before, tie, 0.0).reshape(grp)
        return crit_chunk

    def no_excess_ties():
        crit_ref[...] = jnp.zeros(crit_ref.shape, F32)
        return jnp.full((1, tq), NO_CHUNK, jnp.int32)

    crit_chunk = lax.cond(has_excess_ties, tie_plan, no_excess_ties)

    def mask_body(kc, c):
        x = sc_ref[kc]
        tie_kept = (rows(jnp.where(kc < crit_chunk, 1.0, 0.0))
                    + rows(jnp.where(kc == crit_chunk, 1.0, 0.0)) * crit_ref[...])
        keep = (x > thr) | ((x == thr) & (tie_kept > 0.5))
        sc_ref[kc] = jnp.where(keep, 0.0, NEG_INF)
        return c

    lax.fori_loop(0, nk, mask_body, 0)

    def logits(kc, h):
        k0 = pl.multiple_of(kc * tk, tk)
        kch = ka_ref[0, h // 2, pl.ds(k0, tk), :]
        s = jnp.dot(kch, qat_ref[h], preferred_element_type=F32)
        return s.reshape(grp) + sc_ref[kc]

    def max_chunk(kc, m):
        return jnp.stack([jnp.maximum(m[h], jnp.max(logits(kc, h), axis=0))
                          for h in range(ATT_HEADS)])

    def pv_chunk(kc):
        ss = [logits(kc, h) for h in range(ATT_HEADS)]
        for h in range(ATT_HEADS):
            p = jnp.exp(ss[h] - m_ref[h][None]).reshape(tk, tq).astype(BF16)
            acc_ref[h] += jnp.dot(vt_ref[0, h, kc], p, preferred_element_type=F32)

    def exp_pv_pass():
        acc_ref[...] = jnp.zeros(acc_ref.shape, F32)
        for_chunks(pv_chunk)

    ones_r = jnp.ones((SUB, LANES), BF16)
    lane_r = lax.broadcasted_iota(jnp.int32, (SUB, LANES), 1)

    @pl.when(qb == 0)
    def _():
        for h in range(ATT_HEADS):
            k = ka_ref[0, h // 2]
            head_lanes = jnp.where((lane_r >= 64) == (h % 2 == 1), 1.0, 0.0).astype(BF16)
            k_norm2 = lax.dot_general(head_lanes, k * k, _NT, preferred_element_type=F32)
            kmax_ref[h] = jnp.broadcast_to(jnp.max(k_norm2, axis=1, keepdims=True), (SUB, tq))

    for h in range(ATT_HEADS):
        q = qa_ref[0, h]
        q_norm2 = lax.dot_general(ones_r, q * q, _NT, preferred_element_type=F32)
        m_ref[h] = jnp.sqrt(q_norm2 * kmax_ref[h])
    exp_pv_pass()
    denom_min = jnp.min(jnp.stack([acc_ref[h][64:65] for h in range(ATT_HEADS)]))

    @pl.when(jnp.logical_not(denom_min > DENOM_FLOOR))
    def _():
        m8 = lax.fori_loop(0, nk, max_chunk, jnp.full((ATT_HEADS, SUB, tq), NEG_INF, F32))
        m_ref[...] = jnp.broadcast_to(jnp.max(m8, axis=1, keepdims=True), m8.shape)
        exp_pv_pass()

    outs = []
    for h in range(ATT_HEADS):
        acc = acc_ref[h]
        outs.append(acc[:64] / acc[64:65])
    o_ref[0] = jnp.concatenate(outs, axis=0).T.astype(BF16)


def _dsa(qi, ki, wit, qa, ka, vt):
    bsz, _, seq, _ = qi.shape
    tq, tk = DSA_TQ, DSA_TK
    topk = min(TOPK_MAX, seq // 4)
    assert seq % tq == 0 and tq % tk == 0 and tk >= topk
    blk_q = lambda b, i: (b, 0, i, 0)
    return pl.pallas_call(
        functools.partial(_dsa_kernel, topk=topk),
        grid=(bsz, seq // tq),
        in_specs=[
            pl.BlockSpec((1, IDX_HEADS, tq, LANES), blk_q),
            pl.BlockSpec((1, seq, LANES), lambda b, i: (b, 0, 0)),
            pl.BlockSpec((1, IDX_HEADS, tq), lambda b, i: (b, 0, i)),
            pl.BlockSpec((1, ATT_HEADS, tq, LANES), blk_q),
            pl.BlockSpec((1, ATT_HEADS // 2, seq, LANES), lambda b, i: (b, 0, 0, 0)),
            pl.BlockSpec((1, ATT_HEADS, seq // tk, V_ROWS, tk), lambda b, i: (b, 0, 0, 0, 0)),
        ],
        out_specs=pl.BlockSpec((1, tq, ATT_WIDTH), lambda b, i: (b, i, 0)),
        out_shape=jax.ShapeDtypeStruct((bsz, seq, ATT_WIDTH), BF16),
        scratch_shapes=[
            pltpu.VMEM((seq // tk, tk // SUB, SUB, tq), F32),
            pltpu.VMEM((seq // tk, tk // (2 * SUB), 2 * SUB, tq), BF16),
            pltpu.VMEM((tk // SUB, SUB, tq), F32),
            pltpu.VMEM((ATT_HEADS, SUB, tq), F32),
            pltpu.VMEM((ATT_HEADS, V_ROWS, tq), F32),
            pltpu.VMEM((ATT_HEADS, SUB, tq), F32),
            pltpu.VMEM((IDX_HEADS, LANES, tq), BF16),
            pltpu.VMEM((ATT_HEADS, LANES, tq), BF16),
        ],
        compiler_params=_params("parallel", "arbitrary"),
        name="dsa",
    )(qi, ki, wit, qa, ka, vt)


RET_CHUNK = 256


def _ret_kernel(q_ref, k_ref, v_ref, g_ref, o_ref, state_ref, decay_ref):
    c = RET_CHUNK
    n = pl.program_id(1)
    row = lax.broadcasted_iota(jnp.int32, (c, c), 0)
    col = lax.broadcasted_iota(jnp.int32, (c, c), 1)
    idx = lax.broadcasted_iota(jnp.int32, (c, 1), 0).astype(F32)

    @pl.when(n == 0)
    def _():
        state_ref[...] = jnp.zeros_like(state_ref)
        for h in range(RET_HEADS):
            lg = math.log(1.0 - 2.0 ** (-5.0 - h))
            diff = (row - col).astype(F32)
            decay_ref[h] = jnp.where(row >= col, jnp.exp(lg * jnp.maximum(diff, 0.0)), 0.0)

    for h in range(RET_HEADS):
        lg = math.log(1.0 - 2.0 ** (-5.0 - h))
        sl = slice(h * LANES, (h + 1) * LANES)
        q = q_ref[0, :, sl]
        k = k_ref[0, :, sl]
        v = v_ref[0, :, sl]
        inner = lax.dot_general(q, k, _NT, preferred_element_type=F32) * decay_ref[h]
        o = jnp.dot(inner.astype(BF16), v, preferred_element_type=F32)
        qd = (q.astype(F32) * jnp.exp(lg * (idx + 1.0))).astype(BF16)
        state = state_ref[h]
        o = o + jnp.dot(qd, state.astype(BF16), preferred_element_type=F32)
        kd = (k.astype(F32) * jnp.exp(lg * (c - 1.0 - idx))).astype(BF16)
        kv = lax.dot_general(kd, v, _TN, preferred_element_type=F32)
        state_ref[h] = state * math.exp(lg * c) + kv
        gate = _silu(g_ref[0, :, sl].astype(F32))
        o_ref[0, :, sl] = (gate * _rms(o)).astype(BF16)


def _retention(qr, kr, vr, gr):
    bsz, seq, _ = qr.shape
    c = RET_CHUNK
    spec = pl.BlockSpec((1, c, RET_WIDTH), lambda b, i: (b, i, 0))
    return pl.pallas_call(
        _ret_kernel,
        grid=(bsz, seq // c),
        in_specs=[spec] * 4,
        out_specs=spec,
        out_shape=jax.ShapeDtypeStruct((bsz, seq, RET_WIDTH), BF16),
        scratch_shapes=[
            pltpu.VMEM((RET_HEADS, RET_HEAD_DIM, RET_HEAD_DIM), F32),
            pltpu.VMEM((RET_HEADS, c, c), F32),
        ],
        compiler_params=_params("arbitrary", "arbitrary"),
        name="ret",
    )(qr, kr, vr, gr)


ROUTER_E0 = N_GROUPS


def _mid_kernel(att_ref, ret_ref, x_ref, woa_ref, wor_ref, gpost_ref, gate1_ref,
                gpre_ref, scale2_ref, shift2_ref, wr_ref, br_ref,
                h1_ref, h2_ref, comb_ref):
    mix = jnp.dot(att_ref[0], woa_ref[...], preferred_element_type=F32)
    mix = mix + jnp.dot(ret_ref[0], wor_ref[...], preferred_element_type=F32)
    h1 = x_ref[0] + gate1_ref[0] * (_rms(mix) * gpost_ref[...])
    h1_ref[0] = h1
    h2 = (_rms(h1) * gpre_ref[...] * (1.0 + scale2_ref[0]) + shift2_ref[0]).astype(BF16)
    h2_ref[0] = h2

    logits = jnp.dot(h2, wr_ref[...], preferred_element_type=F32) + br_ref[...]
    lane = lax.broadcasted_iota(jnp.int32, logits.shape, 1)
    big = jnp.int32(LANES)

    def first_argmax(vals, vmax):
        return jnp.min(jnp.where(vals == vmax, lane, big), axis=1, keepdims=True)

    gl = jnp.where(lane < N_GROUPS, logits, NEG_INF)
    gexp = jnp.exp(gl - jnp.max(gl, axis=1, keepdims=True))
    gprob = gexp / jnp.sum(gexp, axis=1, keepdims=True)
    g_top = jnp.max(gprob, axis=1, keepdims=True)
    g_sel = first_argmax(gprob, g_top)
    e_lo = ROUTER_E0 + g_sel * EXPERTS_PER_GROUP
    in_group = (lane >= e_lo) & (lane < e_lo + EXPERTS_PER_GROUP)
    el = jnp.where(in_group, logits, NEG_INF)
    eexp = jnp.exp(el - jnp.max(el, axis=1, keepdims=True))
    eprob = jnp.where(in_group, eexp / jnp.sum(eexp, axis=1, keepdims=True), -1.0)
    top1 = jnp.max(eprob, axis=1, keepdims=True)
    idx1 = first_argmax(eprob, top1)
    rest = jnp.where(lane == idx1, -1.0, eprob)
    top2 = jnp.max(rest, axis=1, keepdims=True)
    idx2 = first_argmax(rest, top2)
    norm = g_top / (top1 + top2)
    comb_ref[0] = jnp.where(lane == idx1, top1 * norm, 0.0) + jnp.where(lane == idx2, top2 * norm, 0.0)


def _mid(att, ret, x, woa, wor, gpost, gate1, gpre, scale2, shift2, wr, br, ts):
    bsz, seq, _ = x.shape
    row = lambda b, i: (b, i, 0)
    per_b = lambda b, i: (b, 0, 0)
    const2 = lambda b, i: (0, 0)
    vec = pl.BlockSpec((1, D_MODEL), const2)
    bvec = pl.BlockSpec((1, 1, D_MODEL), per_b)
    return pl.pallas_call(
        _mid_kernel,
        grid=(bsz, seq // ts),
        in_specs=[
            pl.BlockSpec((1, ts, ATT_WIDTH), row),
            pl.BlockSpec((1, ts, RET_WIDTH), row),
            pl.BlockSpec((1, ts, D_MODEL), row),
            pl.BlockSpec(woa.shape, const2),
            pl.BlockSpec(wor.shape, const2),
            vec, bvec, vec, bvec, bvec,
            pl.BlockSpec(wr.shape, const2),
            pl.BlockSpec((1, LANES), const2),
        ],
        out_specs=[pl.BlockSpec((1, ts, D_MODEL), row),
                   pl.BlockSpec((1, ts, D_MODEL), row),
                   pl.BlockSpec((1, ts, LANES), row)],
        out_shape=[jax.ShapeDtypeStruct((bsz, seq, D_MODEL), F32),
                   jax.ShapeDtypeStruct((bsz, seq, D_MODEL), BF16),
                   jax.ShapeDtypeStruct((bsz, seq, LANES), F32)],
        compiler_params=_params("parallel", "parallel"),
        name="mid",
    )(att, ret, x, woa, wor, gpost, gate1, gpre, scale2, shift2, wr, br)


MOE_EPS = 8


def _moe_kernel(h2_ref, comb_ref, h1_ref, gate2_ref, gpost_ref, wg_ref, wu_ref, wd_ref,
                o_ref, acc_ref):
    step = pl.program_id(2)

    @pl.when(step == 0)
    def _():
        acc_ref[...] = jnp.zeros_like(acc_ref)

    t = h2_ref[0]
    comb = comb_ref[0]
    lane = lax.broadcasted_iota(jnp.int32, comb.shape, 1)
    hids = []
    for j in range(MOE_EPS):
        gate = jnp.dot(t, wg_ref[j], preferred_element_type=F32)
        up = jnp.dot(t, wu_ref[j], preferred_element_type=F32)
        hid = _silu(gate) * up
        e_lane = ROUTER_E0 + step * MOE_EPS + j
        cw = jnp.sum(jnp.where(lane == e_lane, comb, 0.0), axis=1, keepdims=True)
        hids.append((hid * cw).astype(BF16))
    wd = wd_ref[...].reshape(MOE_EPS * D_EXPERT, D_MODEL)
    acc_ref[...] += jnp.dot(jnp.concatenate(hids, axis=1), wd, preferred_element_type=F32)

    @pl.when(step == N_EXPERTS // MOE_EPS - 1)
    def _():
        o_ref[0] = h1_ref[0] + gate2_ref[0] * (_rms(acc_ref[...]) * gpost_ref[...])


def _moe(h2, comb, h1, gate2, gpost, wg, wu, wd, tm):
    bsz, seq, _ = h2.shape
    row = lambda b, i, e: (b, i, 0)
    return pl.pallas_call(
        _moe_kernel,
        grid=(bsz, seq // tm, N_EXPERTS // MOE_EPS),
        in_specs=[
            pl.BlockSpec((1, tm, D_MODEL), row),
            pl.BlockSpec((1, tm, LANES), row),
            pl.BlockSpec((1, tm, D_MODEL), row),
            pl.BlockSpec((1, 1, D_MODEL), lambda b, i, e: (b, 0, 0)),
            pl.BlockSpec((1, D_MODEL), lambda b, i, e: (0, 0)),
            pl.BlockSpec((MOE_EPS, D_MODEL, D_EXPERT), lambda b, i, e: (e, 0, 0)),
            pl.BlockSpec((MOE_EPS, D_MODEL, D_EXPERT), lambda b, i, e: (e, 0, 0)),
            pl.BlockSpec((MOE_EPS, D_EXPERT, D_MODEL), lambda b, i, e: (e, 0, 0)),
        ],
        out_specs=pl.BlockSpec((1, tm, D_MODEL), row),
        out_shape=jax.ShapeDtypeStruct((bsz, seq, D_MODEL), F32),
        scratch_shapes=[pltpu.VMEM((tm, D_MODEL), F32)],
        compiler_params=_params("parallel", "parallel", "arbitrary"),
        name="moe",
    )(h2, comb, h1, gate2, gpost, wg, wu, wd)


def _layer(x, positions, mod, g_pre_mix, w_in, w_out, g_post_mix, g_pre_ffn,
           w_group, b_group, w_expert, b_expert, w_gate_exp, w_up_exp, w_down_exp, g_post_ffn):
    bsz, seq, _ = x.shape
    shift1, scale1, gate1, shift2, scale2, gate2 = [
        m.reshape(bsz, 1, D_MODEL) for m in jnp.split(mod, N_MOD, axis=-1)]
    vec = lambda g: g.reshape(1, D_MODEL)

    a, i8, r = ATT_WIDTH, IDX_HEADS, RET_WIDTH
    offs = np.cumsum([0, a, a, a, a, IDX_HEAD_DIM, i8, r, r, r, r])
    seg = lambda j: w_in[:, offs[j]:offs[j + 1]].astype(BF16)
    wki = jnp.concatenate([seg(4), seg(4), jnp.pad(seg(5), ((0, 0), (0, LANES - i8)))], axis=1)
    weights = [seg(0), seg(1), seg(2), seg(3), wki,
               seg(6), seg(7), seg(8), seg(9)]

    tables = _rope_tables(positions, 512)
    qa, ka, vt, qi, ki, wit, qr, kr, vr, gr = _inproj(
        x, shift1, scale1, vec(g_pre_mix), weights, tables, 512)
    att = _dsa(qi, ki, wit, qa, ka, vt)
    ret = _retention(qr, kr, vr, gr)

    w_router = jnp.pad(jnp.concatenate([w_group, w_expert], axis=1),
                       ((0, 0), (0, LANES - N_GROUPS - N_EXPERTS))).astype(BF16)
    b_router = jnp.pad(jnp.concatenate([b_group, b_expert]),
                       (0, LANES - N_GROUPS - N_EXPERTS)).reshape(1, LANES)
    h1, h2, comb = _mid(att, ret, x, w_out[:ATT_WIDTH].astype(BF16), w_out[ATT_WIDTH:].astype(BF16),
                        vec(g_post_mix), gate1, vec(g_pre_ffn), scale2, shift2, w_router, b_router, 512)

    return _moe(h2, comb, h1, gate2, vec(g_post_ffn), w_gate_exp.astype(BF16),
                w_up_exp.astype(BF16), w_down_exp.astype(BF16), 1024)


def kernel(x, c, positions, g_pre_mix, w_ada, b_ada, w_in, w_out, g_post_mix, g_pre_ffn,
           w_group, b_group, w_expert, b_expert, w_gate_exp, w_up_exp, w_down_exp, g_post_ffn):
    h = x
    for l in range(w_in.shape[0]):
        mod = _ada(c, w_ada[l], b_ada[l])
        h = _layer(h, positions, mod, g_pre_mix[l], w_in[l], w_out[l], g_post_mix[l], g_pre_ffn[l],
                   w_group[l], b_group[l], w_expert[l], b_expert[l],
                   w_gate_exp[l], w_up_exp[l], w_down_exp[l], g_post_ffn[l])
    return h
```

```python
import functools
import math

import numpy as np
import jax
import jax.numpy as jnp
from jax import lax
from jax.experimental import pallas as pl
from jax.experimental.pallas import tpu as pltpu

D_MODEL = 1024
ATT_HEADS = 8
ATT_HEAD_DIM = 64
IDX_HEADS = 8
IDX_HEAD_DIM = 64
TOPK_MAX = 256
RET_HEADS = 4
RET_HEAD_DIM = 128
ROPE_THETA = 10000.0
ATT_WIDTH = ATT_HEADS * ATT_HEAD_DIM
RET_WIDTH = RET_HEADS * RET_HEAD_DIM
N_GROUPS = 4
EXPERTS_PER_GROUP = 8
N_EXPERTS = N_GROUPS * EXPERTS_PER_GROUP
D_EXPERT = 256
N_MOD = 6
EPS = 1e-6

LANES = 128
SUB = 8
VMEM_LIMIT = 56 * 1024 * 1024

DSA_TQ = 256
DSA_TK = 256
V_ROWS = 80
F32 = jnp.float32
BF16 = jnp.bfloat16
NEG_INF = float("-inf")
F32_LOWEST = float(np.finfo(np.float32).min)
INT_MIN = -2 ** 31
BF16_KEY_BITS = 16
NO_CHUNK = 2 ** 30
DENOM_FLOOR = 1e-30

_NT = (((1,), (1,)), ((), ()))
_TN = (((0,), (0,)), ((), ()))


def _params(*sem):
    return pltpu.CompilerParams(dimension_semantics=sem, vmem_limit_bytes=VMEM_LIMIT)


def _rms(x):
    return x * lax.rsqrt(jnp.mean(x * x, axis=-1, keepdims=True) + EPS)


def _silu(x):
    return x * (1.0 / (1.0 + jnp.exp(-x)))


def _ada_kernel(c_ref, w_ref, b_ref, o_ref):
    a = _silu(c_ref[...]).astype(BF16)
    o_ref[...] = jnp.dot(a, w_ref[...].astype(BF16), preferred_element_type=F32) + b_ref[...]


def _ada(c, w, b):
    bsz = c.shape[0]
    n = w.shape[1]
    tn = D_MODEL
    return pl.pallas_call(
        _ada_kernel,
        grid=(n // tn,),
        in_specs=[
            pl.BlockSpec((bsz, D_MODEL), lambda j: (0, 0)),
            pl.BlockSpec((D_MODEL, tn), lambda j: (0, j)),
            pl.BlockSpec((1, tn), lambda j: (0, j)),
        ],
        out_specs=pl.BlockSpec((bsz, tn), lambda j: (0, j)),
        out_shape=jax.ShapeDtypeStruct((bsz, n), F32),
        compiler_params=_params("arbitrary"),
        name="ada",
    )(c, w, b.reshape(1, n))


def _tables_kernel(pos_ref, f_ref, cs64_ref, sn64_ref, cs128_ref, sn128_ref):
    ang = pos_ref[0].astype(F32) * f_ref[...]
    c = jnp.cos(ang)
    s = jnp.sin(ang)
    c_sw = pltpu.roll(c, 64, 1)
    s_sw = pltpu.roll(s, 64, 1)
    lane = lax.broadcasted_iota(jnp.int32, c.shape, 1)
    lo = lane < 64
    cs128_ref[0] = jnp.where(lo, c, c_sw)
    sn128_ref[0] = jnp.where(lo, -s, s_sw)
    cs64_ref[0] = jnp.where(lo, c_sw, c)
    s64 = jnp.where(lo, s_sw, s)
    sn64_ref[0] = jnp.where((lane % 64) < 32, -s64, s64)


def _rope_tables(positions, ts):
    bsz, seq = positions.shape
    f64 = ROPE_THETA ** (-jnp.arange(32, dtype=F32) / 32)
    f128 = ROPE_THETA ** (-jnp.arange(64, dtype=F32) / 64)
    frow = jnp.concatenate([f128, f64, f64]).reshape(1, LANES)
    tab = jax.ShapeDtypeStruct((bsz, seq, LANES), F32)
    tspec = pl.BlockSpec((1, ts, LANES), lambda b, i: (b, i, 0))
    return pl.pallas_call(
        _tables_kernel,
        grid=(bsz, seq // ts),
        in_specs=[
            pl.BlockSpec((1, ts, 1), lambda b, i: (b, i, 0)),
            pl.BlockSpec((1, LANES), lambda b, i: (0, 0)),
        ],
        out_specs=[tspec] * 4,
        out_shape=[tab] * 4,
        compiler_params=_params("parallel", "parallel"),
        name="tables",
    )(positions.reshape(bsz, seq, 1), frow)


def _rope64(y, cs, sn):
    lane = lax.broadcasted_iota(jnp.int32, y.shape, 1)
    rot = jnp.where((lane % 64) < 32, pltpu.roll(y, 96, 1), pltpu.roll(y, 32, 1))
    return y * cs + rot * sn


def _rope128(y, cs, sn):
    return y * cs + pltpu.roll(y, 64, 1) * sn


def _inproj_kernel(x_ref, shift_ref, scale_ref, g_ref,
                   wqa_ref, wka_ref, wv_ref, wqi_ref, wki_ref,
                   wqr_ref, wkr_ref, wvr_ref, wgr_ref,
                   cs64_ref, sn64_ref, cs128_ref, sn128_ref,
                   qa_ref, ka_ref, vt_ref, qi_ref, ki_ref, wit_ref,
                   qr_ref, kr_ref, vr_ref, gr_ref):
    x = x_ref[0]
    h = _rms(x) * g_ref[...] * (1.0 + scale_ref[0]) + shift_ref[0]
    hb = h.astype(BF16)
    cs64, sn64 = cs64_ref[0], sn64_ref[0]
    cs128, sn128 = cs128_ref[0], sn128_ref[0]

    lane = lax.broadcasted_iota(jnp.int32, (hb.shape[0], LANES), 1)
    wide = 2 * LANES

    def pairs(w_ref, rope, cs, sn):
        for c in range(w_ref.shape[1] // wide):
            y = jnp.dot(hb, w_ref[:, c * wide:(c + 1) * wide], preferred_element_type=F32)
            for j in range(wide // LANES):
                yield 2 * c + j, rope(y[:, j * LANES:(j + 1) * LANES], cs, sn)

    def query_heads(w_ref, o_ref, mult):
        for c, y in pairs(w_ref, _rope64, cs64, sn64):
            y = y * mult
            o_ref[0, 2 * c] = jnp.where(lane < 64, y, 0.0).T.astype(BF16)
            o_ref[0, 2 * c + 1] = jnp.where(lane >= 64, y, 0.0).T.astype(BF16)

    query_heads(wqa_ref, qa_ref, ATT_HEAD_DIM ** -0.5)
    query_heads(wqi_ref, qi_ref, 1.0)
    for c, y in pairs(wka_ref, _rope64, cs64, sn64):
        ka_ref[0, c] = y.astype(BF16)

    ts = hb.shape[0]
    vt = jnp.dot(hb, wv_ref[...], preferred_element_type=F32).T
    for h in range(ATT_HEADS):
        for j in range(ts // DSA_TK):
            blk = vt[h * 64:(h + 1) * 64, j * DSA_TK:(j + 1) * DSA_TK]
            vt_ref[0, h, j, :64, :] = blk.astype(BF16)
            vt_ref[0, h, j, 64:, :] = jnp.ones((V_ROWS - 64, DSA_TK), BF16)

    yk = jnp.dot(hb, wki_ref[...], preferred_element_type=F32)
    ki_ref[0] = _rope64(yk[:, :LANES], cs64, sn64).astype(BF16)
    wit_ref[0] = yk[:, LANES:].T[:IDX_HEADS] * ((IDX_HEADS ** -0.5) * (IDX_HEAD_DIM ** -0.5))

    for c, y in pairs(wqr_ref, _rope128, cs128, sn128):
        qr_ref[0, :, c * LANES:(c + 1) * LANES] = y.astype(BF16)
    for c, y in pairs(wkr_ref, _rope128, cs128, sn128):
        kr_ref[0, :, c * LANES:(c + 1) * LANES] = (y * (RET_HEAD_DIM ** -0.5)).astype(BF16)
    vr_ref[0] = jnp.dot(hb, wvr_ref[...], preferred_element_type=F32).astype(BF16)
    gr_ref[0] = jnp.dot(hb, wgr_ref[...], preferred_element_type=F32).astype(BF16)


def _inproj(x, shift, scale, g, weights, tables, ts):
    bsz, seq, _ = x.shape
    row = lambda b, i: (b, i, 0)
    per_b = lambda b, i: (b, 0, 0)
    const2 = lambda b, i: (0, 0)
    head_major = lambda b, i: (b, 0, i, 0)

    def wspec(w):
        return pl.BlockSpec(w.shape, const2)

    qt_shape = jax.ShapeDtypeStruct((bsz, ATT_HEADS, LANES, seq), BF16)
    qt_spec = pl.BlockSpec((1, ATT_HEADS, LANES, ts), lambda b, i: (b, 0, 0, i))
    pair_shape = jax.ShapeDtypeStruct((bsz, ATT_HEADS // 2, seq, LANES), BF16)
    pair_spec = pl.BlockSpec((1, ATT_HEADS // 2, ts, LANES), head_major)
    wide = lambda n, dt=BF16: jax.ShapeDtypeStruct((bsz, seq, n), dt)
    wspec_out = lambda n: pl.BlockSpec((1, ts, n), row)
    return pl.pallas_call(
        _inproj_kernel,
        grid=(bsz, seq // ts),
        in_specs=[
            pl.BlockSpec((1, ts, D_MODEL), row),
            pl.BlockSpec((1, 1, D_MODEL), per_b),
            pl.BlockSpec((1, 1, D_MODEL), per_b),
            pl.BlockSpec((1, D_MODEL), const2),
            *[wspec(w) for w in weights],
            *[pl.BlockSpec((1, ts, LANES), row)] * 4,
        ],
        out_specs=[qt_spec, pair_spec,
                   pl.BlockSpec((1, ATT_HEADS, ts // DSA_TK, V_ROWS, DSA_TK), lambda b, i: (b, 0, i, 0, 0)),
                   qt_spec, wspec_out(LANES),
                   pl.BlockSpec((1, IDX_HEADS, ts), lambda b, i: (b, 0, i)),
                   wspec_out(RET_WIDTH), wspec_out(RET_WIDTH), wspec_out(RET_WIDTH), wspec_out(RET_WIDTH)],
        out_shape=[qt_shape, pair_shape,
                   jax.ShapeDtypeStruct((bsz, ATT_HEADS, seq // DSA_TK, V_ROWS, DSA_TK), BF16),
                   qt_shape, wide(LANES),
                   jax.ShapeDtypeStruct((bsz, IDX_HEADS, seq), F32),
                   wide(RET_WIDTH), wide(RET_WIDTH), wide(RET_WIDTH), wide(RET_WIDTH)],
        compiler_params=_params("parallel", "parallel"),
        name="inproj",
    )(x, shift, scale, g, *weights, *tables)


def _key_to_float(u):
    key = u ^ INT_MIN
    bits = jnp.where(key >= 0, key, key ^ 0x7FFFFFFF)
    return lax.bitcast_convert_type(bits, F32)


def _dsa_kernel(qi_ref, ki_ref, wit_ref, qa_ref, ka_ref, vt_ref, o_ref,
                sc_ref, sb_ref, crit_ref, m_ref, acc_ref, kmax_ref, *, topk):
    tq, tk = DSA_TQ, DSA_TK
    qb = pl.program_id(1)
    nk = (qb + 1) * (tq // tk)
    grp = (tk // SUB, SUB, tq)
    grp16 = (tk // (2 * SUB), 2 * SUB, tq)
    q_pos = qb * tq + lax.broadcasted_iota(jnp.int32, (SUB, tq), 1)

    def key_index(kc):
        return (kc * tk + lax.broadcasted_iota(jnp.int32, grp, 0) * SUB
                + lax.broadcasted_iota(jnp.int32, grp, 1))

    wt = wit_ref[0]

    def for_chunks(body):
        def quad(i, carry):
            for j in range(4):
                body(4 * i + j)
            return carry
        lax.fori_loop(0, nk // 4, quad, 0)

        @pl.when(nk % 4 >= 2)
        def _():
            body(nk // 4 * 4)
            body(nk // 4 * 4 + 1)

        @pl.when(nk % 2 == 1)
        def _():
            body(nk - 1)

    def sum_chunks(body, init):
        acc = lax.fori_loop(0, nk // 2, lambda i, a: body(2 * i + 1, body(2 * i, a)), init)
        return lax.cond(nk % 2 == 1, lambda a: body(nk - 1, a), lambda a: a, acc)

    def score_chunk(kc):
        k0 = pl.multiple_of(kc * tk, tk)
        kch = ki_ref[0, pl.ds(k0, tk), :]
        acc = jnp.zeros((tk, tq), F32)
        for h in range(IDX_HEADS):
            logit = jnp.dot(kch, qi_ref[0, h], preferred_element_type=F32)
            acc = acc + jnp.maximum(logit, 0.0) * wt[h:h + 1, :]
        score = jnp.where(key_index(kc) <= q_pos[None], acc.reshape(grp), NEG_INF)
        sc_ref[kc] = score
        sb_ref[kc] = score.reshape(tk, tq).astype(BF16).reshape(grp16)

    for_chunks(score_chunk)

    def count(pred):
        def body(kc, acc):
            hit = jnp.where(pred(sc_ref[kc], kc), 1.0, 0.0)
            parts = [hit[g] for g in range(grp[0])]
            while len(parts) > 1:
                parts = [parts[i] + parts[i + 1] for i in range(0, len(parts), 2)]
            return acc + parts[0]
        acc = sum_chunks(body, jnp.zeros((SUB, tq), F32))
        return jnp.sum(acc, axis=0, keepdims=True)

    def rows(v):
        return jnp.broadcast_to(v, (SUB, tq))[None]

    few_keys = q_pos[:1] < topk - 1

    def count_bf16(cand):
        cb = jnp.broadcast_to(cand, (2 * SUB, tq)).astype(BF16)[None]

        def body(kc, acc):
            hit = jnp.where(sb_ref[kc] >= cb, jnp.ones((), BF16), jnp.zeros((), BF16))
            parts = [hit[g] for g in range(grp16[0])]
            while len(parts) > 1:
                parts = [parts[i] + parts[i + 1] for i in range(0, len(parts), 2)]
            return acc + parts[0].astype(F32)
        acc = sum_chunks(body, jnp.zeros((2 * SUB, tq), F32))
        return jnp.sum(acc, axis=0, keepdims=True)

    def high_body(i, u):
        trial = u | lax.shift_left(jnp.int32(1), 31 - i)
        return jnp.where(count_bf16(_key_to_float(trial)) >= topk, trial, u)

    u_hi = lax.fori_loop(0, BF16_KEY_BITS, high_body, jnp.zeros((1, tq), jnp.int32))
    base = u_hi - 2 ** (32 - BF16_KEY_BITS)

    def low_body(i, state):
        d, cnt_u = state
        trial = d | lax.shift_left(jnp.int32(1), 32 - BF16_KEY_BITS - i)
        cand = rows(_key_to_float(base + trial))
        cnt = count(lambda x, kc: x >= cand)
        accept = cnt >= topk
        return jnp.where(accept, trial, d), jnp.where(accept, cnt, cnt_u)

    d, cnt_u = lax.fori_loop(
        0, 32 - BF16_KEY_BITS + 1, low_body,
        (jnp.zeros((1, tq), jnp.int32), jnp.full((1, tq), 2.0 * topk, F32)))
    u = base + d
    thr = rows(jnp.where(few_keys, F32_LOWEST, _key_to_float(u)))
    has_excess_ties = jnp.max(jnp.where(few_keys, 0.0, cnt_u)) > topk

    def tie_plan():
        need = topk - count(lambda x, kc: x > thr)

        def chunk_body(kc, carry):
            seen, crit_chunk, seen_before = carry
            tie = jnp.where(sc_ref[kc] == thr, 1.0, 0.0)
            parts = [tie[g] for g in range(grp[0])]
            while len(parts) > 1:
                parts = [parts[i] + parts[i + 1] for i in range(0, len(parts), 2)]
            seen_now = seen + jnp.sum(parts[0], axis=0, keepdims=True)
            reached = jnp.where((crit_chunk == NO_CHUNK) & (seen_now >= need), 1.0, 0.0)
            crit_ref[...] = crit_ref[...] + rows(reached) * (tie - crit_ref[...])
            return (seen_now, jnp.where(reached > 0.5, kc, crit_chunk),
                    jnp.where(reached > 0.5, seen, seen_before))

        crit_ref[...] = jnp.zeros(crit_ref.shape, F32)
        zero = jnp.zeros((1, tq), F32)
        _, crit_chunk, seen_before = lax.fori_loop(
            0, nk, chunk_body, (zero, jnp.full((1, tq), NO_CHUNK, jnp.int32), zero))
        tie = crit_ref[...].reshape(tk, tq)
        lower = (lax.broadcasted_iota(jnp.int32, (tk, tk), 0)
                 >= lax.broadcasted_iota(jnp.int32, (tk, tk), 1))
        rank = jnp.dot(jnp.where(lower, 1.0, 0.0).astype(BF16), tie.astype(BF16),
                       preferred_element_type=F32)
        crit_ref[...] = jnp.where(rank <= need - seen_before, tie, 0.0).reshape(grp)
        return crit_chunk

    def no_excess_ties():
        crit_ref[...] = jnp.zeros(crit_ref.shape, F32)
        return jnp.full((1, tq), NO_CHUNK, jnp.int32)

    crit_chunk = lax.cond(has_excess_ties, tie_plan, no_excess_ties)

    def mask_body(kc, c):
        x = sc_ref[kc]
        tie_kept = (rows(jnp.where(kc < crit_chunk, 1.0, 0.0))
                    + rows(jnp.where(kc == crit_chunk, 1.0, 0.0)) * crit_ref[...])
        keep = (x > thr) | ((x == thr) & (tie_kept > 0.5))
        sc_ref[kc] = jnp.where(keep, 0.0, NEG_INF)
        return c

    lax.fori_loop(0, nk, mask_body, 0)

    def logits(kc, h):
        k0 = pl.multiple_of(kc * tk, tk)
        kch = ka_ref[0, h // 2, pl.ds(k0, tk), :]
        s = jnp.dot(kch, qa_ref[0, h], preferred_element_type=F32)
        return s.reshape(grp) + sc_ref[kc]

    def max_chunk(kc, m):
        return jnp.stack([jnp.maximum(m[h], jnp.max(logits(kc, h), axis=0))
                          for h in range(ATT_HEADS)])

    def pv_chunk(kc):
        ss = [logits(kc, h) for h in range(ATT_HEADS)]
        for h in range(ATT_HEADS):
            p = jnp.exp(ss[h] - m_ref[h][None]).reshape(tk, tq).astype(BF16)
            acc_ref[h] += jnp.dot(vt_ref[0, h, kc], p, preferred_element_type=F32)

    def exp_pv_pass():
        acc_ref[...] = jnp.zeros(acc_ref.shape, F32)
        for_chunks(pv_chunk)

    ones_r = jnp.ones((SUB, LANES), BF16)
    lane_r = lax.broadcasted_iota(jnp.int32, (SUB, LANES), 1)

    @pl.when(qb == 0)
    def _():
        for h in range(ATT_HEADS):
            k = ka_ref[0, h // 2]
            head_lanes = jnp.where((lane_r >= 64) == (h % 2 == 1), 1.0, 0.0).astype(BF16)
            k_norm2 = lax.dot_general(head_lanes, k * k, _NT, preferred_element_type=F32)
            kmax_ref[h] = jnp.broadcast_to(jnp.max(k_norm2, axis=1, keepdims=True), (SUB, tq))

    for h in range(ATT_HEADS):
        q = qa_ref[0, h]
        q_norm2 = jnp.dot(ones_r, q * q, preferred_element_type=F32)
        m_ref[h] = jnp.sqrt(q_norm2 * kmax_ref[h])
    exp_pv_pass()
    denom_min = jnp.min(jnp.stack([acc_ref[h][64:65] for h in range(ATT_HEADS)]))

    @pl.when(jnp.logical_not(denom_min > DENOM_FLOOR))
    def _():
        m8 = lax.fori_loop(0, nk, max_chunk, jnp.full((ATT_HEADS, SUB, tq), NEG_INF, F32))
        m_ref[...] = jnp.broadcast_to(jnp.max(m8, axis=1, keepdims=True), m8.shape)
        exp_pv_pass()

    outs = []
    for h in range(ATT_HEADS):
        acc = acc_ref[h]
        outs.append(acc[:64] / acc[64:65])
    o_ref[0] = jnp.concatenate(outs, axis=0).T.astype(BF16)


def _dsa(qi, ki, wit, qa, ka, vt):
    bsz, seq, _ = ki.shape
    tq, tk = DSA_TQ, DSA_TK
    topk = min(TOPK_MAX, seq // 4)
    assert seq % tq == 0 and tq % tk == 0 and tk >= topk
    blk_q = lambda b, i: (b, 0, 0, i)
    return pl.pallas_call(
        functools.partial(_dsa_kernel, topk=topk),
        grid=(bsz, seq // tq),
        in_specs=[
            pl.BlockSpec((1, IDX_HEADS, LANES, tq), blk_q),
            pl.BlockSpec((1, seq, LANES), lambda b, i: (b, 0, 0)),
            pl.BlockSpec((1, IDX_HEADS, tq), lambda b, i: (b, 0, i)),
            pl.BlockSpec((1, ATT_HEADS, LANES, tq), blk_q),
            pl.BlockSpec((1, ATT_HEADS // 2, seq, LANES), lambda b, i: (b, 0, 0, 0)),
            pl.BlockSpec((1, ATT_HEADS, seq // tk, V_ROWS, tk), lambda b, i: (b, 0, 0, 0, 0)),
        ],
        out_specs=pl.BlockSpec((1, tq, ATT_WIDTH), lambda b, i: (b, i, 0)),
        out_shape=jax.ShapeDtypeStruct((bsz, seq, ATT_WIDTH), BF16),
        scratch_shapes=[
            pltpu.VMEM((seq // tk, tk // SUB, SUB, tq), F32),
            pltpu.VMEM((seq // tk, tk // (2 * SUB), 2 * SUB, tq), BF16),
            pltpu.VMEM((tk // SUB, SUB, tq), F32),
            pltpu.VMEM((ATT_HEADS, SUB, tq), F32),
            pltpu.VMEM((ATT_HEADS, V_ROWS, tq), F32),
            pltpu.VMEM((ATT_HEADS, SUB, tq), F32),
        ],
        compiler_params=_params("parallel", "arbitrary"),
        name="dsa",
    )(qi, ki, wit, qa, ka, vt)


RET_CHUNK = 256


def _ret_kernel(q_ref, k_ref, v_ref, g_ref, o_ref, state_ref, decay_ref):
    c = RET_CHUNK
    n = pl.program_id(1)
    row = lax.broadcasted_iota(jnp.int32, (c, c), 0)
    col = lax.broadcasted_iota(jnp.int32, (c, c), 1)
    idx = lax.broadcasted_iota(jnp.int32, (c, 1), 0).astype(F32)

    @pl.when(n == 0)
    def _():
        state_ref[...] = jnp.zeros_like(state_ref)
        for h in range(RET_HEADS):
            lg = math.log(1.0 - 2.0 ** (-5.0 - h))
            diff = (row - col).astype(F32)
            decay_ref[h] = jnp.where(row >= col, jnp.exp(lg * jnp.maximum(diff, 0.0)), 0.0)

    for h in range(RET_HEADS):
        lg = math.log(1.0 - 2.0 ** (-5.0 - h))
        sl = slice(h * LANES, (h + 1) * LANES)
        q = q_ref[0, :, sl]
        k = k_ref[0, :, sl]
        v = v_ref[0, :, sl]
        inner = lax.dot_general(q, k, _NT, preferred_element_type=F32) * decay_ref[h]
        o = jnp.dot(inner.astype(BF16), v, preferred_element_type=F32)
        qd = (q.astype(F32) * jnp.exp(lg * (idx + 1.0))).astype(BF16)
        state = state_ref[h]
        o = o + jnp.dot(qd, state.astype(BF16), preferred_element_type=F32)
        kd = (k.astype(F32) * jnp.exp(lg * (c - 1.0 - idx))).astype(BF16)
        kv = lax.dot_general(kd, v, _TN, preferred_element_type=F32)
        state_ref[h] = state * math.exp(lg * c) + kv
        gate = _silu(g_ref[0, :, sl].astype(F32))
        o_ref[0, :, sl] = (gate * _rms(o)).astype(BF16)


def _retention(qr, kr, vr, gr):
    bsz, seq, _ = qr.shape
    c = RET_CHUNK
    spec = pl.BlockSpec((1, c, RET_WIDTH), lambda b, i: (b, i, 0))
    return pl.pallas_call(
        _ret_kernel,
        grid=(bsz, seq // c),
        in_specs=[spec] * 4,
        out_specs=spec,
        out_shape=jax.ShapeDtypeStruct((bsz, seq, RET_WIDTH), BF16),
        scratch_shapes=[
            pltpu.VMEM((RET_HEADS, RET_HEAD_DIM, RET_HEAD_DIM), F32),
            pltpu.VMEM((RET_HEADS, c, c), F32),
        ],
        compiler_params=_params("arbitrary", "arbitrary"),
        name="ret",
    )(qr, kr, vr, gr)


ROUTER_E0 = N_GROUPS


def _mid_kernel(att_ref, ret_ref, x_ref, woa_ref, wor_ref, gpost_ref, gate1_ref,
                gpre_ref, scale2_ref, shift2_ref, wr_ref, br_ref,
                h1_ref, h2_ref, comb_ref):
    mix = jnp.dot(att_ref[0], woa_ref[...], preferred_element_type=F32)
    mix = mix + jnp.dot(ret_ref[0], wor_ref[...], preferred_element_type=F32)
    h1 = x_ref[0] + gate1_ref[0] * (_rms(mix) * gpost_ref[...])
    h1_ref[0] = h1
    h2 = (_rms(h1) * gpre_ref[...] * (1.0 + scale2_ref[0]) + shift2_ref[0]).astype(BF16)
    h2_ref[0] = h2

    logits = jnp.dot(h2, wr_ref[...], preferred_element_type=F32) + br_ref[...]
    lane = lax.broadcasted_iota(jnp.int32, logits.shape, 1)
    big = jnp.int32(LANES)

    def first_argmax(vals, vmax):
        return jnp.min(jnp.where(vals == vmax, lane, big), axis=1, keepdims=True)

    gl = jnp.where(lane < N_GROUPS, logits, NEG_INF)
    gexp = jnp.exp(gl - jnp.max(gl, axis=1, keepdims=True))
    gprob = gexp / jnp.sum(gexp, axis=1, keepdims=True)
    g_top = jnp.max(gprob, axis=1, keepdims=True)
    g_sel = first_argmax(gprob, g_top)
    e_lo = ROUTER_E0 + g_sel * EXPERTS_PER_GROUP
    in_group = (lane >= e_lo) & (lane < e_lo + EXPERTS_PER_GROUP)
    el = jnp.where(in_group, logits, NEG_INF)
    eexp = jnp.exp(el - jnp.max(el, axis=1, keepdims=True))
    eprob = jnp.where(in_group, eexp / jnp.sum(eexp, axis=1, keepdims=True), -1.0)
    top1 = jnp.max(eprob, axis=1, keepdims=True)
    idx1 = first_argmax(eprob, top1)
    rest = jnp.where(lane == idx1, -1.0, eprob)
    top2 = jnp.max(rest, axis=1, keepdims=True)
    idx2 = first_argmax(rest, top2)
    norm = g_top / (top1 + top2)
    comb_ref[0] = jnp.where(lane == idx1, top1 * norm, 0.0) + jnp.where(lane == idx2, top2 * norm, 0.0)


def _mid(att, ret, x, woa, wor, gpost, gate1, gpre, scale2, shift2, wr, br, ts):
    bsz, seq, _ = x.shape
    row = lambda b, i: (b, i, 0)
    per_b = lambda b, i: (b, 0, 0)
    const2 = lambda b, i: (0, 0)
    vec = pl.BlockSpec((1, D_MODEL), const2)
    bvec = pl.BlockSpec((1, 1, D_MODEL), per_b)
    return pl.pallas_call(
        _mid_kernel,
        grid=(bsz, seq // ts),
        in_specs=[
            pl.BlockSpec((1, ts, ATT_WIDTH), row),
            pl.BlockSpec((1, ts, RET_WIDTH), row),
            pl.BlockSpec((1, ts, D_MODEL), row),
            pl.BlockSpec(woa.shape, const2),
            pl.BlockSpec(wor.shape, const2),
            vec, bvec, vec, bvec, bvec,
            pl.BlockSpec(wr.shape, const2),
            pl.BlockSpec((1, LANES), const2),
        ],
        out_specs=[pl.BlockSpec((1, ts, D_MODEL), row),
                   pl.BlockSpec((1, ts, D_MODEL), row),
                   pl.BlockSpec((1, ts, LANES), row)],
        out_shape=[jax.ShapeDtypeStruct((bsz, seq, D_MODEL), F32),
                   jax.ShapeDtypeStruct((bsz, seq, D_MODEL), BF16),
                   jax.ShapeDtypeStruct((bsz, seq, LANES), F32)],
        compiler_params=_params("parallel", "parallel"),
        name="mid",
    )(att, ret, x, woa, wor, gpost, gate1, gpre, scale2, shift2, wr, br)


MOE_EPS = 8


def _moe_kernel(h2_ref, comb_ref, h1_ref, gate2_ref, gpost_ref, wg_ref, wu_ref, wd_ref,
                o_ref, acc_ref):
    step = pl.program_id(2)

    @pl.when(step == 0)
    def _():
        acc_ref[...] = jnp.zeros_like(acc_ref)

    t = h2_ref[0]
    comb = comb_ref[0]
    lane = lax.broadcasted_iota(jnp.int32, comb.shape, 1)
    hids = []
    for j in range(MOE_EPS):
        gate = jnp.dot(t, wg_ref[j], preferred_element_type=F32)
        up = jnp.dot(t, wu_ref[j], preferred_element_type=F32)
        hid = _silu(gate) * up
        e_lane = ROUTER_E0 + step * MOE_EPS + j
        cw = jnp.sum(jnp.where(lane == e_lane, comb, 0.0), axis=1, keepdims=True)
        hids.append((hid * cw).astype(BF16))
    wd = wd_ref[...].reshape(MOE_EPS * D_EXPERT, D_MODEL)
    acc_ref[...] += jnp.dot(jnp.concatenate(hids, axis=1), wd, preferred_element_type=F32)

    @pl.when(step == N_EXPERTS // MOE_EPS - 1)
    def _():
        o_ref[0] = h1_ref[0] + gate2_ref[0] * (_rms(acc_ref[...]) * gpost_ref[...])


def _moe(h2, comb, h1, gate2, gpost, wg, wu, wd, tm):
    bsz, seq, _ = h2.shape
    row = lambda b, i, e: (b, i, 0)
    return pl.pallas_call(
        _moe_kernel,
        grid=(bsz, seq // tm, N_EXPERTS // MOE_EPS),
        in_specs=[
            pl.BlockSpec((1, tm, D_MODEL), row),
            pl.BlockSpec((1, tm, LANES), row),
            pl.BlockSpec((1, tm, D_MODEL), row),
            pl.BlockSpec((1, 1, D_MODEL), lambda b, i, e: (b, 0, 0)),
            pl.BlockSpec((1, D_MODEL), lambda b, i, e: (0, 0)),
            pl.BlockSpec((MOE_EPS, D_MODEL, D_EXPERT), lambda b, i, e: (e, 0, 0)),
            pl.BlockSpec((MOE_EPS, D_MODEL, D_EXPERT), lambda b, i, e: (e, 0, 0)),
            pl.BlockSpec((MOE_EPS, D_EXPERT, D_MODEL), lambda b, i, e: (e, 0, 0)),
        ],
        out_specs=pl.BlockSpec((1, tm, D_MODEL), row),
        out_shape=jax.ShapeDtypeStruct((bsz, seq, D_MODEL), F32),
        scratch_shapes=[pltpu.VMEM((tm, D_MODEL), F32)],
        compiler_params=_params("parallel", "parallel", "arbitrary"),
        name="moe",
    )(h2, comb, h1, gate2, gpost, wg, wu, wd)


def _layer(x, positions, mod, g_pre_mix, w_in, w_out, g_post_mix, g_pre_ffn,
           w_group, b_group, w_expert, b_expert, w_gate_exp, w_up_exp, w_down_exp, g_post_ffn):
    bsz, seq, _ = x.shape
    shift1, scale1, gate1, shift2, scale2, gate2 = [
        m.reshape(bsz, 1, D_MODEL) for m in jnp.split(mod, N_MOD, axis=-1)]
    vec = lambda g: g.reshape(1, D_MODEL)

    a, i8, r = ATT_WIDTH, IDX_HEADS, RET_WIDTH
    offs = np.cumsum([0, a, a, a, a, IDX_HEAD_DIM, i8, r, r, r, r])
    seg = lambda j: w_in[:, offs[j]:offs[j + 1]].astype(BF16)
    wki = jnp.concatenate([seg(4), seg(4), jnp.pad(seg(5), ((0, 0), (0, LANES - i8)))], axis=1)
    weights = [seg(0), seg(1), seg(2), seg(3), wki,
               seg(6), seg(7), seg(8), seg(9)]

    tables = _rope_tables(positions, 512)
    qa, ka, vt, qi, ki, wit, qr, kr, vr, gr = _inproj(
        x, shift1, scale1, vec(g_pre_mix), weights, tables, 512)
    att = _dsa(qi, ki, wit, qa, ka, vt)
    ret = _retention(qr, kr, vr, gr)

    w_router = jnp.pad(jnp.concatenate([w_group, w_expert], axis=1),
                       ((0, 0), (0, LANES - N_GROUPS - N_EXPERTS))).astype(BF16)
    b_router = jnp.pad(jnp.concatenate([b_group, b_expert]),
                       (0, LANES - N_GROUPS - N_EXPERTS)).reshape(1, LANES)
    h1, h2, comb = _mid(att, ret, x, w_out[:ATT_WIDTH].astype(BF16), w_out[ATT_WIDTH:].astype(BF16),
                        vec(g_post_mix), gate1, vec(g_pre_ffn), scale2, shift2, w_router, b_router, 1024)

    return _moe(h2, comb, h1, gate2, vec(g_post_ffn), w_gate_exp.astype(BF16),
                w_up_exp.astype(BF16), w_down_exp.astype(BF16), 1024)


def kernel(x, c, positions, g_pre_mix, w_ada, b_ada, w_in, w_out, g_post_mix, g_pre_ffn,
           w_group, b_group, w_expert, b_expert, w_gate_exp, w_up_exp, w_down_exp, g_post_ffn):
    h = x
    for l in range(w_in.shape[0]):
        mod = _ada(c, w_ada[l], b_ada[l])
        h = _layer(h, positions, mod, g_pre_mix[l], w_in[l], w_out[l], g_post_mix[l], g_pre_ffn[l],
                   w_group[l], b_group[l], w_expert[l], b_expert[l],
                   w_gate_exp[l], w_up_exp[l], w_down_exp[l], g_post_ffn[l])
    return h
```

```python
import functools
import math

import numpy as np
import jax
import jax.numpy as jnp
from jax import lax
from jax.experimental import pallas as pl
from jax.experimental.pallas import tpu as pltpu

D_MODEL = 1024
ATT_HEADS = 8
ATT_HEAD_DIM = 64
IDX_HEADS = 8
IDX_HEAD_DIM = 64
TOPK_MAX = 256
RET_HEADS = 4
RET_HEAD_DIM = 128
ROPE_THETA = 10000.0
ATT_WIDTH = ATT_HEADS * ATT_HEAD_DIM
RET_WIDTH = RET_HEADS * RET_HEAD_DIM
N_GROUPS = 4
EXPERTS_PER_GROUP = 8
N_EXPERTS = N_GROUPS * EXPERTS_PER_GROUP
D_EXPERT = 256
N_MOD = 6
EPS = 1e-6

LANES = 128
SUB = 8
VMEM_LIMIT = 56 * 1024 * 1024

DSA_TQ = 256
DSA_TK = 256
V_ROWS = 80
F32 = jnp.float32
BF16 = jnp.bfloat16
NEG_INF = float("-inf")
F32_LOWEST = float(np.finfo(np.float32).min)
INT_MIN = -2 ** 31
BF16_KEY_BITS = 16
NO_CHUNK = 2 ** 30
DENOM_FLOOR = 1e-30

_NT = (((1,), (1,)), ((), ()))
_TN = (((0,), (0,)), ((), ()))


def _params(*sem):
    return pltpu.CompilerParams(dimension_semantics=sem, vmem_limit_bytes=VMEM_LIMIT)


def _rms(x):
    return x * lax.rsqrt(jnp.mean(x * x, axis=-1, keepdims=True) + EPS)


def _silu(x):
    return x * (1.0 / (1.0 + jnp.exp(-x)))


def _ada_kernel(c_ref, w_ref, b_ref, o_ref):
    a = _silu(c_ref[...]).astype(BF16)
    o_ref[...] = jnp.dot(a, w_ref[...].astype(BF16), preferred_element_type=F32) + b_ref[...]


def _ada(c, w, b):
    bsz = c.shape[0]
    n = w.shape[1]
    tn = D_MODEL
    return pl.pallas_call(
        _ada_kernel,
        grid=(n // tn,),
        in_specs=[
            pl.BlockSpec((bsz, D_MODEL), lambda j: (0, 0)),
            pl.BlockSpec((D_MODEL, tn), lambda j: (0, j)),
            pl.BlockSpec((1, tn), lambda j: (0, j)),
        ],
        out_specs=pl.BlockSpec((bsz, tn), lambda j: (0, j)),
        out_shape=jax.ShapeDtypeStruct((bsz, n), F32),
        compiler_params=_params("arbitrary"),
        name="ada",
    )(c, w, b.reshape(1, n))


def _tables_kernel(pos_ref, f_ref, cs64_ref, sn64_ref, cs128_ref, sn128_ref):
    ang = pos_ref[0].astype(F32) * f_ref[...]
    c = jnp.cos(ang)
    s = jnp.sin(ang)
    c_sw = pltpu.roll(c, 64, 1)
    s_sw = pltpu.roll(s, 64, 1)
    lane = lax.broadcasted_iota(jnp.int32, c.shape, 1)
    lo = lane < 64
    cs128_ref[0] = jnp.where(lo, c, c_sw)
    sn128_ref[0] = jnp.where(lo, -s, s_sw)
    cs64_ref[0] = jnp.where(lo, c_sw, c)
    s64 = jnp.where(lo, s_sw, s)
    sn64_ref[0] = jnp.where((lane % 64) < 32, -s64, s64)


def _rope_tables(positions, ts):
    bsz, seq = positions.shape
    f64 = ROPE_THETA ** (-jnp.arange(32, dtype=F32) / 32)
    f128 = ROPE_THETA ** (-jnp.arange(64, dtype=F32) / 64)
    frow = jnp.concatenate([f128, f64, f64]).reshape(1, LANES)
    tab = jax.ShapeDtypeStruct((bsz, seq, LANES), F32)
    tspec = pl.BlockSpec((1, ts, LANES), lambda b, i: (b, i, 0))
    return pl.pallas_call(
        _tables_kernel,
        grid=(bsz, seq // ts),
        in_specs=[
            pl.BlockSpec((1, ts, 1), lambda b, i: (b, i, 0)),
            pl.BlockSpec((1, LANES), lambda b, i: (0, 0)),
        ],
        out_specs=[tspec] * 4,
        out_shape=[tab] * 4,
        compiler_params=_params("parallel", "parallel"),
        name="tables",
    )(positions.reshape(bsz, seq, 1), frow)


def _rope64(y, cs, sn):
    lane = lax.broadcasted_iota(jnp.int32, y.shape, 1)
    rot = jnp.where((lane % 64) < 32, pltpu.roll(y, 96, 1), pltpu.roll(y, 32, 1))
    return y * cs + rot * sn


def _rope128(y, cs, sn):
    return y * cs + pltpu.roll(y, 64, 1) * sn


def _inproj_kernel(x_ref, shift_ref, scale_ref, g_ref,
                   wqa_ref, wka_ref, wv_ref, wqi_ref, wki_ref,
                   wqr_ref, wkr_ref, wvr_ref, wgr_ref,
                   cs64_ref, sn64_ref, cs128_ref, sn128_ref,
                   qa_ref, ka_ref, vt_ref, qi_ref, ki_ref, wit_ref,
                   qr_ref, kr_ref, vr_ref, gr_ref):
    x = x_ref[0]
    h = _rms(x) * g_ref[...] * (1.0 + scale_ref[0]) + shift_ref[0]
    hb = h.astype(BF16)
    cs64, sn64 = cs64_ref[0], sn64_ref[0]
    cs128, sn128 = cs128_ref[0], sn128_ref[0]

    lane = lax.broadcasted_iota(jnp.int32, (hb.shape[0], LANES), 1)
    wide = 2 * LANES

    def pairs(w_ref, rope, cs, sn):
        for c in range(w_ref.shape[1] // wide):
            y = jnp.dot(hb, w_ref[:, c * wide:(c + 1) * wide], preferred_element_type=F32)
            for j in range(wide // LANES):
                yield 2 * c + j, rope(y[:, j * LANES:(j + 1) * LANES], cs, sn)

    def query_heads(w_ref, o_ref, mult):
        for c, y in pairs(w_ref, _rope64, cs64, sn64):
            y = y * mult
            o_ref[0, 2 * c] = jnp.where(lane < 64, y, 0.0).T.astype(BF16)
            o_ref[0, 2 * c + 1] = jnp.where(lane >= 64, y, 0.0).T.astype(BF16)

    query_heads(wqa_ref, qa_ref, ATT_HEAD_DIM ** -0.5)
    query_heads(wqi_ref, qi_ref, 1.0)
    for c, y in pairs(wka_ref, _rope64, cs64, sn64):
        ka_ref[0, c] = y.astype(BF16)

    ts = hb.shape[0]
    vt = jnp.dot(hb, wv_ref[...], preferred_element_type=F32).T
    for h in range(ATT_HEADS):
        for j in range(ts // DSA_TK):
            blk = vt[h * 64:(h + 1) * 64, j * DSA_TK:(j + 1) * DSA_TK]
            vt_ref[0, h, j, :64, :] = blk.astype(BF16)
            vt_ref[0, h, j, 64:, :] = jnp.ones((V_ROWS - 64, DSA_TK), BF16)

    yk = jnp.dot(hb, wki_ref[...], preferred_element_type=F32)
    ki_ref[0] = _rope64(yk[:, :LANES], cs64, sn64).astype(BF16)
    wit_ref[0] = yk[:, LANES:].T[:IDX_HEADS] * ((IDX_HEADS ** -0.5) * (IDX_HEAD_DIM ** -0.5))

    for c, y in pairs(wqr_ref, _rope128, cs128, sn128):
        qr_ref[0, :, c * LANES:(c + 1) * LANES] = y.astype(BF16)
    for c, y in pairs(wkr_ref, _rope128, cs128, sn128):
        kr_ref[0, :, c * LANES:(c + 1) * LANES] = (y * (RET_HEAD_DIM ** -0.5)).astype(BF16)
    vr_ref[0] = jnp.dot(hb, wvr_ref[...], preferred_element_type=F32).astype(BF16)
    gr_ref[0] = jnp.dot(hb, wgr_ref[...], preferred_element_type=F32).astype(BF16)


def _inproj(x, shift, scale, g, weights, tables, ts):
    bsz, seq, _ = x.shape
    row = lambda b, i: (b, i, 0)
    per_b = lambda b, i: (b, 0, 0)
    const2 = lambda b, i: (0, 0)
    head_major = lambda b, i: (b, 0, i, 0)

    def wspec(w):
        return pl.BlockSpec(w.shape, const2)

    qt_shape = jax.ShapeDtypeStruct((bsz, ATT_HEADS, LANES, seq), BF16)
    qt_spec = pl.BlockSpec((1, ATT_HEADS, LANES, ts), lambda b, i: (b, 0, 0, i))
    pair_shape = jax.ShapeDtypeStruct((bsz, ATT_HEADS // 2, seq, LANES), BF16)
    pair_spec = pl.BlockSpec((1, ATT_HEADS // 2, ts, LANES), head_major)
    wide = lambda n, dt=BF16: jax.ShapeDtypeStruct((bsz, seq, n), dt)
    wspec_out = lambda n: pl.BlockSpec((1, ts, n), row)
    return pl.pallas_call(
        _inproj_kernel,
        grid=(bsz, seq // ts),
        in_specs=[
            pl.BlockSpec((1, ts, D_MODEL), row),
            pl.BlockSpec((1, 1, D_MODEL), per_b),
            pl.BlockSpec((1, 1, D_MODEL), per_b),
            pl.BlockSpec((1, D_MODEL), const2),
            *[wspec(w) for w in weights],
            *[pl.BlockSpec((1, ts, LANES), row)] * 4,
        ],
        out_specs=[qt_spec, pair_spec,
                   pl.BlockSpec((1, ATT_HEADS, ts // DSA_TK, V_ROWS, DSA_TK), lambda b, i: (b, 0, i, 0, 0)),
                   qt_spec, wspec_out(LANES),
                   pl.BlockSpec((1, IDX_HEADS, ts), lambda b, i: (b, 0, i)),
                   wspec_out(RET_WIDTH), wspec_out(RET_WIDTH), wspec_out(RET_WIDTH), wspec_out(RET_WIDTH)],
        out_shape=[qt_shape, pair_shape,
                   jax.ShapeDtypeStruct((bsz, ATT_HEADS, seq // DSA_TK, V_ROWS, DSA_TK), BF16),
                   qt_shape, wide(LANES),
                   jax.ShapeDtypeStruct((bsz, IDX_HEADS, seq), F32),
                   wide(RET_WIDTH), wide(RET_WIDTH), wide(RET_WIDTH), wide(RET_WIDTH)],
        compiler_params=_params("parallel", "parallel"),
        name="inproj",
    )(x, shift, scale, g, *weights, *tables)


def _key_to_float(u):
    key = u ^ INT_MIN
    bits = jnp.where(key >= 0, key, key ^ 0x7FFFFFFF)
    return lax.bitcast_convert_type(bits, F32)


def _dsa_kernel(qi_ref, ki_ref, wit_ref, qa_ref, ka_ref, vt_ref, o_ref,
                sc_ref, sb_ref, crit_ref, m_ref, acc_ref, kmax_ref, *, topk):
    tq, tk = DSA_TQ, DSA_TK
    qb = pl.program_id(1)
    nk = (qb + 1) * (tq // tk)
    grp = (tk // SUB, SUB, tq)
    grp16 = (tk // (2 * SUB), 2 * SUB, tq)
    q_pos = qb * tq + lax.broadcasted_iota(jnp.int32, (SUB, tq), 1)

    def key_index(kc):
        return (kc * tk + lax.broadcasted_iota(jnp.int32, grp, 0) * SUB
                + lax.broadcasted_iota(jnp.int32, grp, 1))

    wt = wit_ref[0]

    def for_chunks(body):
        def quad(i, carry):
            for j in range(4):
                body(4 * i + j)
            return carry
        lax.fori_loop(0, nk // 4, quad, 0)

        @pl.when(nk % 4 >= 2)
        def _():
            body(nk // 4 * 4)
            body(nk // 4 * 4 + 1)

        @pl.when(nk % 2 == 1)
        def _():
            body(nk - 1)

    def sum_chunks(body, init):
        acc = lax.fori_loop(0, nk // 2, lambda i, a: body(2 * i + 1, body(2 * i, a)), init)
        return lax.cond(nk % 2 == 1, lambda a: body(nk - 1, a), lambda a: a, acc)

    def score_chunk(kc):
        k0 = pl.multiple_of(kc * tk, tk)
        kch = ki_ref[0, pl.ds(k0, tk), :]
        acc = jnp.zeros((tk, tq), F32)
        for h in range(IDX_HEADS):
            logit = jnp.dot(kch, qi_ref[0, h], preferred_element_type=F32)
            acc = acc + jnp.maximum(logit, 0.0) * wt[h:h + 1, :]
        score = jnp.where(key_index(kc) <= q_pos[None], acc.reshape(grp), NEG_INF)
        sc_ref[kc] = score
        sb_ref[kc] = score.reshape(tk, tq).astype(BF16).reshape(grp16)

    for_chunks(score_chunk)

    def count(pred):
        def body(kc, acc):
            hit = jnp.where(pred(sc_ref[kc], kc), 1.0, 0.0)
            parts = [hit[g] for g in range(grp[0])]
            while len(parts) > 1:
                parts = [parts[i] + parts[i + 1] for i in range(0, len(parts), 2)]
            return acc + parts[0]
        acc = sum_chunks(body, jnp.zeros((SUB, tq), F32))
        return jnp.sum(acc, axis=0, keepdims=True)

    def rows(v):
        return jnp.broadcast_to(v, (SUB, tq))[None]

    few_keys = q_pos[:1] < topk - 1

    def count_bf16(cand):
        cb = jnp.broadcast_to(cand, (2 * SUB, tq)).astype(BF16)[None]

        def body(kc, acc):
            hit = jnp.where(sb_ref[kc] >= cb, jnp.ones((), BF16), jnp.zeros((), BF16))
            parts = [hit[g] for g in range(grp16[0])]
            while len(parts) > 1:
                parts = [parts[i] + parts[i + 1] for i in range(0, len(parts), 2)]
            return acc + parts[0].astype(F32)
        acc = sum_chunks(body, jnp.zeros((2 * SUB, tq), F32))
        return jnp.sum(acc, axis=0, keepdims=True)

    def high_body(i, u):
        trial = u | lax.shift_left(jnp.int32(1), 31 - i)
        return jnp.where(count_bf16(_key_to_float(trial)) >= topk, trial, u)

    u_hi = lax.fori_loop(0, BF16_KEY_BITS, high_body, jnp.zeros((1, tq), jnp.int32))
    base = u_hi - 2 ** (32 - BF16_KEY_BITS)

    def low_body(i, state):
        d, cnt_u = state
        trial = d | lax.shift_left(jnp.int32(1), 32 - BF16_KEY_BITS - i)
        cand = rows(_key_to_float(base + trial))
        cnt = count(lambda x, kc: x >= cand)
        accept = cnt >= topk
        return jnp.where(accept, trial, d), jnp.where(accept, cnt, cnt_u)

    d, cnt_u = lax.fori_loop(
        0, 32 - BF16_KEY_BITS + 1, low_body,
        (jnp.zeros((1, tq), jnp.int32), jnp.full((1, tq), 2.0 * topk, F32)))
    u = base + d
    thr = rows(jnp.where(few_keys, F32_LOWEST, _key_to_float(u)))
    has_excess_ties = jnp.max(jnp.where(few_keys, 0.0, cnt_u)) > topk

    def tie_plan():
        need = topk - count(lambda x, kc: x > thr)

        def chunk_body(kc, carry):
            seen, crit_chunk, seen_before = carry
            tie = jnp.where(sc_ref[kc] == thr, 1.0, 0.0)
            parts = [tie[g] for g in range(grp[0])]
            while len(parts) > 1:
                parts = [parts[i] + parts[i + 1] for i in range(0, len(parts), 2)]
            seen_now = seen + jnp.sum(parts[0], axis=0, keepdims=True)
            reached = jnp.where((crit_chunk == NO_CHUNK) & (seen_now >= need), 1.0, 0.0)
            crit_ref[...] = crit_ref[...] + rows(reached) * (tie - crit_ref[...])
            return (seen_now, jnp.where(reached > 0.5, kc, crit_chunk),
                    jnp.where(reached > 0.5, seen, seen_before))

        crit_ref[...] = jnp.zeros(crit_ref.shape, F32)
        zero = jnp.zeros((1, tq), F32)
        _, crit_chunk, seen_before = lax.fori_loop(
            0, nk, chunk_body, (zero, jnp.full((1, tq), NO_CHUNK, jnp.int32), zero))
        tie = crit_ref[...].reshape(tk, tq)
        lower = (lax.broadcasted_iota(jnp.int32, (tk, tk), 0)
                 >= lax.broadcasted_iota(jnp.int32, (tk, tk), 1))
        rank = jnp.dot(jnp.where(lower, 1.0, 0.0).astype(BF16), tie.astype(BF16),
                       preferred_element_type=F32)
        crit_ref[...] = jnp.where(rank <= need - seen_before, tie, 0.0).reshape(grp)
        return crit_chunk

    def no_excess_ties():
        crit_ref[...] = jnp.zeros(crit_ref.shape, F32)
        return jnp.full((1, tq), NO_CHUNK, jnp.int32)

    crit_chunk = lax.cond(has_excess_ties, tie_plan, no_excess_ties)

    def mask_bias(kc):
        x = sc_ref[kc]
        tie_kept = (rows(jnp.where(kc < crit_chunk, 1.0, 0.0))
                    + rows(jnp.where(kc == crit_chunk, 1.0, 0.0)) * crit_ref[...])
        keep = (x > thr) | ((x == thr) & (tie_kept > 0.5))
        return jnp.where(keep, 0.0, NEG_INF)

    def logits(kc, h, bias):
        k0 = pl.multiple_of(kc * tk, tk)
        kch = ka_ref[0, h // 2, pl.ds(k0, tk), :]
        s = jnp.dot(kch, qa_ref[0, h], preferred_element_type=F32)
        return s.reshape(grp) + bias

    def max_chunk(kc, m):
        bias = mask_bias(kc)
        return jnp.stack([jnp.maximum(m[h], jnp.max(logits(kc, h, bias), axis=0))
                          for h in range(ATT_HEADS)])

    def pv_chunk(kc):
        bias = mask_bias(kc)
        ss = [logits(kc, h, bias) for h in range(ATT_HEADS)]
        for h in range(ATT_HEADS):
            p = jnp.exp(ss[h] - m_ref[h][None]).reshape(tk, tq).astype(BF16)
            acc_ref[h] += jnp.dot(vt_ref[0, h, kc], p, preferred_element_type=F32)

    def exp_pv_pass():
        acc_ref[...] = jnp.zeros(acc_ref.shape, F32)
        for_chunks(pv_chunk)

    ones_r = jnp.ones((SUB, LANES), BF16)
    lane_r = lax.broadcasted_iota(jnp.int32, (SUB, LANES), 1)

    @pl.when(qb == 0)
    def _():
        for h in range(ATT_HEADS):
            k = ka_ref[0, h // 2]
            head_lanes = jnp.where((lane_r >= 64) == (h % 2 == 1), 1.0, 0.0).astype(BF16)
            k_norm2 = lax.dot_general(head_lanes, k * k, _NT, preferred_element_type=F32)
            kmax_ref[h] = jnp.broadcast_to(jnp.max(k_norm2, axis=1, keepdims=True), (SUB, tq))

    for h in range(ATT_HEADS):
        q = qa_ref[0, h]
        q_norm2 = jnp.dot(ones_r, q * q, preferred_element_type=F32)
        m_ref[h] = jnp.sqrt(q_norm2 * kmax_ref[h])
    exp_pv_pass()
    denom_min = jnp.min(jnp.stack([acc_ref[h][64:65] for h in range(ATT_HEADS)]))

    @pl.when(jnp.logical_not(denom_min > DENOM_FLOOR))
    def _():
        m8 = lax.fori_loop(0, nk, max_chunk, jnp.full((ATT_HEADS, SUB, tq), NEG_INF, F32))
        m_ref[...] = jnp.broadcast_to(jnp.max(m8, axis=1, keepdims=True), m8.shape)
        exp_pv_pass()

    outs = []
    for h in range(ATT_HEADS):
        acc = acc_ref[h]
        outs.append(acc[:64] / acc[64:65])
    o_ref[0] = jnp.concatenate(outs, axis=0).T.astype(BF16)


def _dsa(qi, ki, wit, qa, ka, vt):
    bsz, seq, _ = ki.shape
    tq, tk = DSA_TQ, DSA_TK
    topk = min(TOPK_MAX, seq // 4)
    assert seq % tq == 0 and tq % tk == 0 and tk >= topk
    blk_q = lambda b, i: (b, 0, 0, i)
    return pl.pallas_call(
        functools.partial(_dsa_kernel, topk=topk),
        grid=(bsz, seq // tq),
        in_specs=[
            pl.BlockSpec((1, IDX_HEADS, LANES, tq), blk_q),
            pl.BlockSpec((1, seq, LANES), lambda b, i: (b, 0, 0)),
            pl.BlockSpec((1, IDX_HEADS, tq), lambda b, i: (b, 0, i)),
            pl.BlockSpec((1, ATT_HEADS, LANES, tq), blk_q),
            pl.BlockSpec((1, ATT_HEADS // 2, seq, LANES), lambda b, i: (b, 0, 0, 0)),
            pl.BlockSpec((1, ATT_HEADS, seq // tk, V_ROWS, tk), lambda b, i: (b, 0, 0, 0, 0)),
        ],
        out_specs=pl.BlockSpec((1, tq, ATT_WIDTH), lambda b, i: (b, i, 0)),
        out_shape=jax.ShapeDtypeStruct((bsz, seq, ATT_WIDTH), BF16),
        scratch_shapes=[
            pltpu.VMEM((seq // tk, tk // SUB, SUB, tq), F32),
            pltpu.VMEM((seq // tk, tk // (2 * SUB), 2 * SUB, tq), BF16),
            pltpu.VMEM((tk // SUB, SUB, tq), F32),
            pltpu.VMEM((ATT_HEADS, SUB, tq), F32),
            pltpu.VMEM((ATT_HEADS, V_ROWS, tq), F32),
            pltpu.VMEM((ATT_HEADS, SUB, tq), F32),
        ],
        compiler_params=_params("parallel", "arbitrary"),
        name="dsa",
    )(qi, ki, wit, qa, ka, vt)


RET_CHUNK = 256


def _ret_kernel(q_ref, k_ref, v_ref, g_ref, o_ref, state_ref, decay_ref):
    c = RET_CHUNK
    n = pl.program_id(1)
    row = lax.broadcasted_iota(jnp.int32, (c, c), 0)
    col = lax.broadcasted_iota(jnp.int32, (c, c), 1)
    idx = lax.broadcasted_iota(jnp.int32, (c, 1), 0).astype(F32)

    @pl.when(n == 0)
    def _():
        state_ref[...] = jnp.zeros_like(state_ref)
        for h in range(RET_HEADS):
            lg = math.log(1.0 - 2.0 ** (-5.0 - h))
            diff = (row - col).astype(F32)
            decay_ref[h] = jnp.where(row >= col, jnp.exp(lg * jnp.maximum(diff, 0.0)), 0.0)

    for h in range(RET_HEADS):
        lg = math.log(1.0 - 2.0 ** (-5.0 - h))
        sl = slice(h * LANES, (h + 1) * LANES)
        q = q_ref[0, :, sl]
        k = k_ref[0, :, sl]
        v = v_ref[0, :, sl]
        inner = lax.dot_general(q, k, _NT, preferred_element_type=F32) * decay_ref[h]
        o = jnp.dot(inner.astype(BF16), v, preferred_element_type=F32)
        qd = (q.astype(F32) * jnp.exp(lg * (idx + 1.0))).astype(BF16)
        state = state_ref[h]
        o = o + jnp.dot(qd, state.astype(BF16), preferred_element_type=F32)
        kd = (k.astype(F32) * jnp.exp(lg * (c - 1.0 - idx))).astype(BF16)
        kv = lax.dot_general(kd, v, _TN, preferred_element_type=F32)
        state_ref[h] = state * math.exp(lg * c) + kv
        gate = _silu(g_ref[0, :, sl].astype(F32))
        o_ref[0, :, sl] = (gate * _rms(o)).astype(BF16)


def _retention(qr, kr, vr, gr):
    bsz, seq, _ = qr.shape
    c = RET_CHUNK
    spec = pl.BlockSpec((1, c, RET_WIDTH), lambda b, i: (b, i, 0))
    return pl.pallas_call(
        _ret_kernel,
        grid=(bsz, seq // c),
        in_specs=[spec] * 4,
        out_specs=spec,
        out_shape=jax.ShapeDtypeStruct((bsz, seq, RET_WIDTH), BF16),
        scratch_shapes=[
            pltpu.VMEM((RET_HEADS, RET_HEAD_DIM, RET_HEAD_DIM), F32),
            pltpu.VMEM((RET_HEADS, c, c), F32),
        ],
        compiler_params=_params("arbitrary", "arbitrary"),
        name="ret",
    )(qr, kr, vr, gr)


ROUTER_E0 = N_GROUPS


def _mid_kernel(att_ref, ret_ref, x_ref, woa_ref, wor_ref, gpost_ref, gate1_ref,
                gpre_ref, scale2_ref, shift2_ref, wr_ref, br_ref,
                h1_ref, h2_ref, comb_ref):
    mix = jnp.dot(att_ref[0], woa_ref[...], preferred_element_type=F32)
    mix = mix + jnp.dot(ret_ref[0], wor_ref[...], preferred_element_type=F32)
    h1 = x_ref[0] + gate1_ref[0] * (_rms(mix) * gpost_ref[...])
    h1_ref[0] = h1
    h2 = (_rms(h1) * gpre_ref[...] * (1.0 + scale2_ref[0]) + shift2_ref[0]).astype(BF16)
    h2_ref[0] = h2

    logits = jnp.dot(h2, wr_ref[...], preferred_element_type=F32) + br_ref[...]
    lane = lax.broadcasted_iota(jnp.int32, logits.shape, 1)
    big = jnp.int32(LANES)

    def first_argmax(vals, vmax):
        return jnp.min(jnp.where(vals == vmax, lane, big), axis=1, keepdims=True)

    gl = jnp.where(lane < N_GROUPS, logits, NEG_INF)
    gexp = jnp.exp(gl - jnp.max(gl, axis=1, keepdims=True))
    gprob = gexp / jnp.sum(gexp, axis=1, keepdims=True)
    g_top = jnp.max(gprob, axis=1, keepdims=True)
    g_sel = first_argmax(gprob, g_top)
    e_lo = ROUTER_E0 + g_sel * EXPERTS_PER_GROUP
    in_group = (lane >= e_lo) & (lane < e_lo + EXPERTS_PER_GROUP)
    el = jnp.where(in_group, logits, NEG_INF)
    eexp = jnp.exp(el - jnp.max(el, axis=1, keepdims=True))
    eprob = jnp.where(in_group, eexp / jnp.sum(eexp, axis=1, keepdims=True), -1.0)
    top1 = jnp.max(eprob, axis=1, keepdims=True)
    idx1 = first_argmax(eprob, top1)
    rest = jnp.where(lane == idx1, -1.0, eprob)
    top2 = jnp.max(rest, axis=1, keepdims=True)
    idx2 = first_argmax(rest, top2)
    norm = g_top / (top1 + top2)
    comb_ref[0] = jnp.where(lane == idx1, top1 * norm, 0.0) + jnp.where(lane == idx2, top2 * norm, 0.0)


def _mid(att, ret, x, woa, wor, gpost, gate1, gpre, scale2, shift2, wr, br, ts):
    bsz, seq, _ = x.shape
    row = lambda b, i: (b, i, 0)
    per_b = lambda b, i: (b, 0, 0)
    const2 = lambda b, i: (0, 0)
    vec = pl.BlockSpec((1, D_MODEL), const2)
    bvec = pl.BlockSpec((1, 1, D_MODEL), per_b)
    return pl.pallas_call(
        _mid_kernel,
        grid=(bsz, seq // ts),
        in_specs=[
            pl.BlockSpec((1, ts, ATT_WIDTH), row),
            pl.BlockSpec((1, ts, RET_WIDTH), row),
            pl.BlockSpec((1, ts, D_MODEL), row),
            pl.BlockSpec(woa.shape, const2),
            pl.BlockSpec(wor.shape, const2),
            vec, bvec, vec, bvec, bvec,
            pl.BlockSpec(wr.shape, const2),
            pl.BlockSpec((1, LANES), const2),
        ],
        out_specs=[pl.BlockSpec((1, ts, D_MODEL), row),
                   pl.BlockSpec((1, ts, D_MODEL), row),
                   pl.BlockSpec((1, ts, LANES), row)],
        out_shape=[jax.ShapeDtypeStruct((bsz, seq, D_MODEL), F32),
                   jax.ShapeDtypeStruct((bsz, seq, D_MODEL), BF16),
                   jax.ShapeDtypeStruct((bsz, seq, LANES), F32)],
        compiler_params=_params("parallel", "parallel"),
        name="mid",
    )(att, ret, x, woa, wor, gpost, gate1, gpre, scale2, shift2, wr, br)


MOE_EPS = 8


def _moe_kernel(h2_ref, comb_ref, h1_ref, gate2_ref, gpost_ref, wg_ref, wu_ref, wd_ref,
                o_ref, acc_ref):
    step = pl.program_id(2)

    @pl.when(step == 0)
    def _():
        acc_ref[...] = jnp.zeros_like(acc_ref)

    t = h2_ref[0]
    comb = comb_ref[0]
    lane = lax.broadcasted_iota(jnp.int32, comb.shape, 1)
    hids = []
    for j in range(MOE_EPS):
        gate = jnp.dot(t, wg_ref[j], preferred_element_type=F32)
        up = jnp.dot(t, wu_ref[j], preferred_element_type=F32)
        hid = _silu(gate) * up
        e_lane = ROUTER_E0 + step * MOE_EPS + j
        cw = jnp.sum(jnp.where(lane == e_lane, comb, 0.0), axis=1, keepdims=True)
        hids.append((hid * cw).astype(BF16))
    wd = wd_ref[...].reshape(MOE_EPS * D_EXPERT, D_MODEL)
    acc_ref[...] += jnp.dot(jnp.concatenate(hids, axis=1), wd, preferred_element_type=F32)

    @pl.when(step == N_EXPERTS // MOE_EPS - 1)
    def _():
        o_ref[0] = h1_ref[0] + gate2_ref[0] * (_rms(acc_ref[...]) * gpost_ref[...])


def _moe(h2, comb, h1, gate2, gpost, wg, wu, wd, tm):
    bsz, seq, _ = h2.shape
    row = lambda b, i, e: (b, i, 0)
    return pl.pallas_call(
        _moe_kernel,
        grid=(bsz, seq // tm, N_EXPERTS // MOE_EPS),
        in_specs=[
            pl.BlockSpec((1, tm, D_MODEL), row),
            pl.BlockSpec((1, tm, LANES), row),
            pl.BlockSpec((1, tm, D_MODEL), row),
            pl.BlockSpec((1, 1, D_MODEL), lambda b, i, e: (b, 0, 0)),
            pl.BlockSpec((1, D_MODEL), lambda b, i, e: (0, 0)),
            pl.BlockSpec((MOE_EPS, D_MODEL, D_EXPERT), lambda b, i, e: (e, 0, 0)),
            pl.BlockSpec((MOE_EPS, D_MODEL, D_EXPERT), lambda b, i, e: (e, 0, 0)),
            pl.BlockSpec((MOE_EPS, D_EXPERT, D_MODEL), lambda b, i, e: (e, 0, 0)),
        ],
        out_specs=pl.BlockSpec((1, tm, D_MODEL), row),
        out_shape=jax.ShapeDtypeStruct((bsz, seq, D_MODEL), F32),
        scratch_shapes=[pltpu.VMEM((tm, D_MODEL), F32)],
        compiler_params=_params("parallel", "parallel", "arbitrary"),
        name="moe",
    )(h2, comb, h1, gate2, gpost, wg, wu, wd)


def _layer(x, positions, mod, g_pre_mix, w_in, w_out, g_post_mix, g_pre_ffn,
           w_group, b_group, w_expert, b_expert, w_gate_exp, w_up_exp, w_down_exp, g_post_ffn):
    bsz, seq, _ = x.shape
    shift1, scale1, gate1, shift2, scale2, gate2 = [
        m.reshape(bsz, 1, D_MODEL) for m in jnp.split(mod, N_MOD, axis=-1)]
    vec = lambda g: g.reshape(1, D_MODEL)

    a, i8, r = ATT_WIDTH, IDX_HEADS, RET_WIDTH
    offs = np.cumsum([0, a, a, a, a, IDX_HEAD_DIM, i8, r, r, r, r])
    seg = lambda j: w_in[:, offs[j]:offs[j + 1]].astype(BF16)
    wki = jnp.concatenate([seg(4), seg(4), jnp.pad(seg(5), ((0, 0), (0, LANES - i8)))], axis=1)
    weights = [seg(0), seg(1), seg(2), seg(3), wki,
               seg(6), seg(7), seg(8), seg(9)]

    tables = _rope_tables(positions, 512)
    qa, ka, vt, qi, ki, wit, qr, kr, vr, gr = _inproj(
        x, shift1, scale1, vec(g_pre_mix), weights, tables, 512)
    att = _dsa(qi, ki, wit, qa, ka, vt)
    ret = _retention(qr, kr, vr, gr)

    w_router = jnp.pad(jnp.concatenate([w_group, w_expert], axis=1),
                       ((0, 0), (0, LANES - N_GROUPS - N_EXPERTS))).astype(BF16)
    b_router = jnp.pad(jnp.concatenate([b_group, b_expert]),
                       (0, LANES - N_GROUPS - N_EXPERTS)).reshape(1, LANES)
    h1, h2, comb = _mid(att, ret, x, w_out[:ATT_WIDTH].astype(BF16), w_out[ATT_WIDTH:].astype(BF16),
                        vec(g_post_mix), gate1, vec(g_pre_ffn), scale2, shift2, w_router, b_router, 1024)

    return _moe(h2, comb, h1, gate2, vec(g_post_ffn), w_gate_exp.astype(BF16),
                w_up_exp.astype(BF16), w_down_exp.astype(BF16), 1024)


def kernel(x, c, positions, g_pre_mix, w_ada, b_ada, w_in, w_out, g_post_mix, g_pre_ffn,
           w_group, b_group, w_expert, b_expert, w_gate_exp, w_up_exp, w_down_exp, g_post_ffn):
    h = x
    for l in range(w_in.shape[0]):
        mod = _ada(c, w_ada[l], b_ada[l])
        h = _layer(h, positions, mod, g_pre_mix[l], w_in[l], w_out[l], g_post_mix[l], g_pre_ffn[l],
                   w_group[l], b_group[l], w_expert[l], b_expert[l],
                   w_gate_exp[l], w_up_exp[l], w_down_exp[l], g_post_ffn[l])
    return h
```

```python
import functools
import math

import numpy as np
import jax
import jax.numpy as jnp
from jax import lax
from jax.experimental import pallas as pl
from jax.experimental.pallas import tpu as pltpu

D_MODEL = 1024
ATT_HEADS = 8
ATT_HEAD_DIM = 64
IDX_HEADS = 8
IDX_HEAD_DIM = 64
TOPK_MAX = 256
RET_HEADS = 4
RET_HEAD_DIM = 128
ROPE_THETA = 10000.0
ATT_WIDTH = ATT_HEADS * ATT_HEAD_DIM
RET_WIDTH = RET_HEADS * RET_HEAD_DIM
N_GROUPS = 4
EXPERTS_PER_GROUP = 8
N_EXPERTS = N_GROUPS * EXPERTS_PER_GROUP
D_EXPERT = 256
N_MOD = 6
EPS = 1e-6

LANES = 128
SUB = 8
VMEM_LIMIT = 56 * 1024 * 1024

DSA_TQ = 256
DSA_TK = 256
V_ROWS = 80
F32 = jnp.float32
BF16 = jnp.bfloat16
NEG_INF = float("-inf")
F32_LOWEST = float(np.finfo(np.float32).min)
INT_MIN = -2 ** 31
LOG2_E = math.log2(math.e)
BF16_KEY_BITS = 16
NO_CHUNK = 2 ** 30
DENOM_FLOOR = 1e-30

_NT = (((1,), (1,)), ((), ()))
_TN = (((0,), (0,)), ((), ()))


def _params(*sem):
    return pltpu.CompilerParams(dimension_semantics=sem, vmem_limit_bytes=VMEM_LIMIT)


def _rms(x):
    return x * lax.rsqrt(jnp.mean(x * x, axis=-1, keepdims=True) + EPS)


def _silu(x):
    return x * (1.0 / (1.0 + jnp.exp(-x)))


def _ada_kernel(c_ref, w_ref, b_ref, o_ref):
    a = _silu(c_ref[...]).astype(BF16)
    o_ref[...] = jnp.dot(a, w_ref[...].astype(BF16), preferred_element_type=F32) + b_ref[...]


def _ada(c, w, b):
    bsz = c.shape[0]
    n = w.shape[1]
    tn = D_MODEL
    return pl.pallas_call(
        _ada_kernel,
        grid=(n // tn,),
        in_specs=[
            pl.BlockSpec((bsz, D_MODEL), lambda j: (0, 0)),
            pl.BlockSpec((D_MODEL, tn), lambda j: (0, j)),
            pl.BlockSpec((1, tn), lambda j: (0, j)),
        ],
        out_specs=pl.BlockSpec((bsz, tn), lambda j: (0, j)),
        out_shape=jax.ShapeDtypeStruct((bsz, n), F32),
        compiler_params=_params("arbitrary"),
        name="ada",
    )(c, w, b.reshape(1, n))


def _tables_kernel(pos_ref, f_ref, cs64_ref, sn64_ref, cs128_ref, sn128_ref):
    ang = pos_ref[0].astype(F32) * f_ref[...]
    c = jnp.cos(ang)
    s = jnp.sin(ang)
    c_sw = pltpu.roll(c, 64, 1)
    s_sw = pltpu.roll(s, 64, 1)
    lane = lax.broadcasted_iota(jnp.int32, c.shape, 1)
    lo = lane < 64
    cs128_ref[0] = jnp.where(lo, c, c_sw)
    sn128_ref[0] = jnp.where(lo, -s, s_sw)
    cs64_ref[0] = jnp.where(lo, c_sw, c)
    s64 = jnp.where(lo, s_sw, s)
    sn64_ref[0] = jnp.where((lane % 64) < 32, -s64, s64)


def _rope_tables(positions, ts):
    bsz, seq = positions.shape
    f64 = ROPE_THETA ** (-jnp.arange(32, dtype=F32) / 32)
    f128 = ROPE_THETA ** (-jnp.arange(64, dtype=F32) / 64)
    frow = jnp.concatenate([f128, f64, f64]).reshape(1, LANES)
    tab = jax.ShapeDtypeStruct((bsz, seq, LANES), F32)
    tspec = pl.BlockSpec((1, ts, LANES), lambda b, i: (b, i, 0))
    return pl.pallas_call(
        _tables_kernel,
        grid=(bsz, seq // ts),
        in_specs=[
            pl.BlockSpec((1, ts, 1), lambda b, i: (b, i, 0)),
            pl.BlockSpec((1, LANES), lambda b, i: (0, 0)),
        ],
        out_specs=[tspec] * 4,
        out_shape=[tab] * 4,
        compiler_params=_params("parallel", "parallel"),
        name="tables",
    )(positions.reshape(bsz, seq, 1), frow)


def _rope64(y, cs, sn):
    lane = lax.broadcasted_iota(jnp.int32, y.shape, 1)
    rot = jnp.where((lane % 64) < 32, pltpu.roll(y, 96, 1), pltpu.roll(y, 32, 1))
    return y * cs + rot * sn


def _rope128(y, cs, sn):
    return y * cs + pltpu.roll(y, 64, 1) * sn


def _inproj_kernel(x_ref, shift_ref, scale_ref, g_ref,
                   wqa_ref, wka_ref, wv_ref, wqi_ref, wki_ref,
                   wqr_ref, wkr_ref, wvr_ref, wgr_ref,
                   cs64_ref, sn64_ref, cs128_ref, sn128_ref,
                   qa_ref, ka_ref, vt_ref, qi_ref, ki_ref, wit_ref,
                   qr_ref, kr_ref, vr_ref, gr_ref):
    x = x_ref[0]
    h = _rms(x) * g_ref[...] * (1.0 + scale_ref[0]) + shift_ref[0]
    hb = h.astype(BF16)
    cs64, sn64 = cs64_ref[0], sn64_ref[0]
    cs128, sn128 = cs128_ref[0], sn128_ref[0]

    lane = lax.broadcasted_iota(jnp.int32, (hb.shape[0], LANES), 1)
    wide = 2 * LANES

    def pairs(w_ref, rope, cs, sn):
        for c in range(w_ref.shape[1] // wide):
            y = jnp.dot(hb, w_ref[:, c * wide:(c + 1) * wide], preferred_element_type=F32)
            for j in range(wide // LANES):
                yield 2 * c + j, rope(y[:, j * LANES:(j + 1) * LANES], cs, sn)

    def query_heads(w_ref, o_ref, mult):
        for c, y in pairs(w_ref, _rope64, cs64, sn64):
            y = y * mult
            o_ref[0, 2 * c] = jnp.where(lane < 64, y, 0.0).T.astype(BF16)
            o_ref[0, 2 * c + 1] = jnp.where(lane >= 64, y, 0.0).T.astype(BF16)

    query_heads(wqa_ref, qa_ref, ATT_HEAD_DIM ** -0.5 * LOG2_E)
    query_heads(wqi_ref, qi_ref, 1.0)
    for c, y in pairs(wka_ref, _rope64, cs64, sn64):
        ka_ref[0, c] = y.astype(BF16)

    ts = hb.shape[0]
    vt = jnp.dot(hb, wv_ref[...], preferred_element_type=F32).T
    for h in range(ATT_HEADS):
        for j in range(ts // DSA_TK):
            blk = vt[h * 64:(h + 1) * 64, j * DSA_TK:(j + 1) * DSA_TK]
            vt_ref[0, h, j, :64, :] = blk.astype(BF16)
            vt_ref[0, h, j, 64:, :] = jnp.ones((V_ROWS - 64, DSA_TK), BF16)

    yk = jnp.dot(hb, wki_ref[...], preferred_element_type=F32)
    ki_ref[0] = _rope64(yk[:, :LANES], cs64, sn64).astype(BF16)
    wit_ref[0] = yk[:, LANES:].T[:IDX_HEADS] * ((IDX_HEADS ** -0.5) * (IDX_HEAD_DIM ** -0.5))

    for c, y in pairs(wqr_ref, _rope128, cs128, sn128):
        qr_ref[0, :, c * LANES:(c + 1) * LANES] = y.astype(BF16)
    for c, y in pairs(wkr_ref, _rope128, cs128, sn128):
        kr_ref[0, :, c * LANES:(c + 1) * LANES] = (y * (RET_HEAD_DIM ** -0.5)).astype(BF16)
    vr_ref[0] = jnp.dot(hb, wvr_ref[...], preferred_element_type=F32).astype(BF16)
    gr_ref[0] = jnp.dot(hb, wgr_ref[...], preferred_element_type=F32).astype(BF16)


def _inproj(x, shift, scale, g, weights, tables, ts):
    bsz, seq, _ = x.shape
    row = lambda b, i: (b, i, 0)
    per_b = lambda b, i: (b, 0, 0)
    const2 = lambda b, i: (0, 0)
    head_major = lambda b, i: (b, 0, i, 0)

    def wspec(w):
        return pl.BlockSpec(w.shape, const2)

    qt_shape = jax.ShapeDtypeStruct((bsz, ATT_HEADS, LANES, seq), BF16)
    qt_spec = pl.BlockSpec((1, ATT_HEADS, LANES, ts), lambda b, i: (b, 0, 0, i))
    pair_shape = jax.ShapeDtypeStruct((bsz, ATT_HEADS // 2, seq, LANES), BF16)
    pair_spec = pl.BlockSpec((1, ATT_HEADS // 2, ts, LANES), head_major)
    wide = lambda n, dt=BF16: jax.ShapeDtypeStruct((bsz, seq, n), dt)
    wspec_out = lambda n: pl.BlockSpec((1, ts, n), row)
    return pl.pallas_call(
        _inproj_kernel,
        grid=(bsz, seq // ts),
        in_specs=[
            pl.BlockSpec((1, ts, D_MODEL), row),
            pl.BlockSpec((1, 1, D_MODEL), per_b),
            pl.BlockSpec((1, 1, D_MODEL), per_b),
            pl.BlockSpec((1, D_MODEL), const2),
            *[wspec(w) for w in weights],
            *[pl.BlockSpec((1, ts, LANES), row)] * 4,
        ],
        out_specs=[qt_spec, pair_spec,
                   pl.BlockSpec((1, ATT_HEADS, ts // DSA_TK, V_ROWS, DSA_TK), lambda b, i: (b, 0, i, 0, 0)),
                   qt_spec, wspec_out(LANES),
                   pl.BlockSpec((1, IDX_HEADS, ts), lambda b, i: (b, 0, i)),
                   wspec_out(RET_WIDTH), wspec_out(RET_WIDTH), wspec_out(RET_WIDTH), wspec_out(RET_WIDTH)],
        out_shape=[qt_shape, pair_shape,
                   jax.ShapeDtypeStruct((bsz, ATT_HEADS, seq // DSA_TK, V_ROWS, DSA_TK), BF16),
                   qt_shape, wide(LANES),
                   jax.ShapeDtypeStruct((bsz, IDX_HEADS, seq), F32),
                   wide(RET_WIDTH), wide(RET_WIDTH), wide(RET_WIDTH), wide(RET_WIDTH)],
        compiler_params=_params("parallel", "parallel"),
        name="inproj",
    )(x, shift, scale, g, *weights, *tables)


def _key_to_float(u):
    key = u ^ INT_MIN
    bits = jnp.where(key >= 0, key, key ^ 0x7FFFFFFF)
    return lax.bitcast_convert_type(bits, F32)


def _dsa_kernel(qi_ref, ki_ref, wit_ref, qa_ref, ka_ref, vt_ref, o_ref,
                sc_ref, sb_ref, crit_ref, m_ref, acc_ref, kmax_ref, *, topk):
    tq, tk = DSA_TQ, DSA_TK
    qb = pl.program_id(1)
    nk = (qb + 1) * (tq // tk)
    grp = (tk // SUB, SUB, tq)
    grp16 = (tk // (2 * SUB), 2 * SUB, tq)
    q_pos = qb * tq + lax.broadcasted_iota(jnp.int32, (SUB, tq), 1)

    def key_index(kc):
        return (kc * tk + lax.broadcasted_iota(jnp.int32, grp, 0) * SUB
                + lax.broadcasted_iota(jnp.int32, grp, 1))

    wt = wit_ref[0]

    def for_chunks(body):
        def quad(i, carry):
            for j in range(4):
                body(4 * i + j)
            return carry
        lax.fori_loop(0, nk // 4, quad, 0)

        @pl.when(nk % 4 >= 2)
        def _():
            body(nk // 4 * 4)
            body(nk // 4 * 4 + 1)

        @pl.when(nk % 2 == 1)
        def _():
            body(nk - 1)

    def sum_chunks(body, init):
        acc = lax.fori_loop(0, nk // 2, lambda i, a: body(2 * i + 1, body(2 * i, a)), init)
        return lax.cond(nk % 2 == 1, lambda a: body(nk - 1, a), lambda a: a, acc)

    def score_chunk(kc):
        k0 = pl.multiple_of(kc * tk, tk)
        kch = ki_ref[0, pl.ds(k0, tk), :]
        acc = jnp.zeros((tk, tq), F32)
        for h in range(IDX_HEADS):
            logit = jnp.dot(kch, qi_ref[0, h], preferred_element_type=F32)
            acc = acc + jnp.maximum(logit, 0.0) * wt[h:h + 1, :]
        score = jnp.where(key_index(kc) <= q_pos[None], acc.reshape(grp), NEG_INF)
        sc_ref[kc] = score
        sb_ref[kc] = score.reshape(tk, tq).astype(BF16).reshape(grp16)

    for_chunks(score_chunk)

    def count(pred):
        def body(kc, acc):
            hit = jnp.where(pred(sc_ref[kc], kc), 1.0, 0.0)
            parts = [hit[g] for g in range(grp[0])]
            while len(parts) > 1:
                parts = [parts[i] + parts[i + 1] for i in range(0, len(parts), 2)]
            return acc + parts[0]
        acc = sum_chunks(body, jnp.zeros((SUB, tq), F32))
        return jnp.sum(acc, axis=0, keepdims=True)

    def rows(v):
        return jnp.broadcast_to(v, (SUB, tq))[None]

    few_keys = q_pos[:1] < topk - 1

    def count_bf16(cand):
        cb = jnp.broadcast_to(cand, (2 * SUB, tq)).astype(BF16)[None]

        def body(kc, acc):
            hit = jnp.where(sb_ref[kc] >= cb, jnp.ones((), BF16), jnp.zeros((), BF16))
            parts = [hit[g] for g in range(grp16[0])]
            while len(parts) > 1:
                parts = [parts[i] + parts[i + 1] for i in range(0, len(parts), 2)]
            return acc + parts[0].astype(F32)
        acc = sum_chunks(body, jnp.zeros((2 * SUB, tq), F32))
        return jnp.sum(acc, axis=0, keepdims=True)

    def high_body(i, u):
        trial = u | lax.shift_left(jnp.int32(1), 31 - i)
        return jnp.where(count_bf16(_key_to_float(trial)) >= topk, trial, u)

    u_hi = lax.fori_loop(0, BF16_KEY_BITS, high_body, jnp.zeros((1, tq), jnp.int32))
    base = u_hi - 2 ** (32 - BF16_KEY_BITS)

    def low_body(i, state):
        d, cnt_u = state
        trial = d | lax.shift_left(jnp.int32(1), 32 - BF16_KEY_BITS - i)
        cand = rows(_key_to_float(base + trial))
        cnt = count(lambda x, kc: x >= cand)
        accept = cnt >= topk
        return jnp.where(accept, trial, d), jnp.where(accept, cnt, cnt_u)

    d, cnt_u = lax.fori_loop(
        0, 32 - BF16_KEY_BITS + 1, low_body,
        (jnp.zeros((1, tq), jnp.int32), jnp.full((1, tq), 2.0 * topk, F32)))
    u = base + d
    thr = rows(jnp.where(few_keys, F32_LOWEST, _key_to_float(u)))
    has_excess_ties = jnp.max(jnp.where(few_keys, 0.0, cnt_u)) > topk

    def tie_plan():
        need = topk - count(lambda x, kc: x > thr)

        def chunk_body(kc, carry):
            seen, crit_chunk, seen_before = carry
            tie = jnp.where(sc_ref[kc] == thr, 1.0, 0.0)
            parts = [tie[g] for g in range(grp[0])]
            while len(parts) > 1:
                parts = [parts[i] + parts[i + 1] for i in range(0, len(parts), 2)]
            seen_now = seen + jnp.sum(parts[0], axis=0, keepdims=True)
            reached = jnp.where((crit_chunk == NO_CHUNK) & (seen_now >= need), 1.0, 0.0)
            crit_ref[...] = crit_ref[...] + rows(reached) * (tie - crit_ref[...])
            return (seen_now, jnp.where(reached > 0.5, kc, crit_chunk),
                    jnp.where(reached > 0.5, seen, seen_before))

        crit_ref[...] = jnp.zeros(crit_ref.shape, F32)
        zero = jnp.zeros((1, tq), F32)
        _, crit_chunk, seen_before = lax.fori_loop(
            0, nk, chunk_body, (zero, jnp.full((1, tq), NO_CHUNK, jnp.int32), zero))
        tie = crit_ref[...].reshape(tk, tq)
        lower = (lax.broadcasted_iota(jnp.int32, (tk, tk), 0)
                 >= lax.broadcasted_iota(jnp.int32, (tk, tk), 1))
        rank = jnp.dot(jnp.where(lower, 1.0, 0.0).astype(BF16), tie.astype(BF16),
                       preferred_element_type=F32)
        crit_ref[...] = jnp.where(rank <= need - seen_before, tie, 0.0).reshape(grp)
        return crit_chunk

    def no_excess_ties():
        crit_ref[...] = jnp.zeros(crit_ref.shape, F32)
        return jnp.full((1, tq), NO_CHUNK, jnp.int32)

    crit_chunk = lax.cond(has_excess_ties, tie_plan, no_excess_ties)

    def mask_bias(kc):
        x = sc_ref[kc]
        tie_kept = (rows(jnp.where(kc < crit_chunk, 1.0, 0.0))
                    + rows(jnp.where(kc == crit_chunk, 1.0, 0.0)) * crit_ref[...])
        keep = (x > thr) | ((x == thr) & (tie_kept > 0.5))
        return jnp.where(keep, 0.0, NEG_INF)

    def logits(kc, h, bias):
        k0 = pl.multiple_of(kc * tk, tk)
        kch = ka_ref[0, h // 2, pl.ds(k0, tk), :]
        s = jnp.dot(kch, qa_ref[0, h], preferred_element_type=F32)
        return s.reshape(grp) + bias

    def max_chunk(kc, m):
        bias = mask_bias(kc)
        return jnp.stack([jnp.maximum(m[h], jnp.max(logits(kc, h, bias), axis=0))
                          for h in range(ATT_HEADS)])

    def pv_chunk(kc):
        bias = mask_bias(kc)
        ss = [logits(kc, h, bias) for h in range(ATT_HEADS)]
        for h in range(ATT_HEADS):
            p = jnp.exp2(ss[h] - m_ref[h][None]).reshape(tk, tq).astype(BF16)
            acc_ref[h] += jnp.dot(vt_ref[0, h, kc], p, preferred_element_type=F32)

    def exp_pv_pass():
        acc_ref[...] = jnp.zeros(acc_ref.shape, F32)
        for_chunks(pv_chunk)

    ones_r = jnp.ones((SUB, LANES), BF16)
    lane_r = lax.broadcasted_iota(jnp.int32, (SUB, LANES), 1)

    @pl.when(qb == 0)
    def _():
        for h in range(ATT_HEADS):
            k = ka_ref[0, h // 2]
            head_lanes = jnp.where((lane_r >= 64) == (h % 2 == 1), 1.0, 0.0).astype(BF16)
            k_norm2 = lax.dot_general(head_lanes, k * k, _NT, preferred_element_type=F32)
            kmax_ref[h] = jnp.broadcast_to(jnp.max(k_norm2, axis=1, keepdims=True), (SUB, tq))

    for h in range(ATT_HEADS):
        q = qa_ref[0, h]
        q_norm2 = jnp.dot(ones_r, q * q, preferred_element_type=F32)
        m_ref[h] = jnp.sqrt(q_norm2 * kmax_ref[h])
    exp_pv_pass()
    denom_min = jnp.min(jnp.stack([acc_ref[h][64:65] for h in range(ATT_HEADS)]))

    @pl.when(jnp.logical_not(denom_min > DENOM_FLOOR))
    def _():
        m8 = lax.fori_loop(0, nk, max_chunk, jnp.full((ATT_HEADS, SUB, tq), NEG_INF, F32))
        m_ref[...] = jnp.broadcast_to(jnp.max(m8, axis=1, keepdims=True), m8.shape)
        exp_pv_pass()

    outs = []
    for h in range(ATT_HEADS):
        acc = acc_ref[h]
        outs.append(acc[:64] / acc[64:65])
    o_ref[0] = jnp.concatenate(outs, axis=0).T.astype(BF16)


def _dsa(qi, ki, wit, qa, ka, vt):
    bsz, seq, _ = ki.shape
    tq, tk = DSA_TQ, DSA_TK
    topk = min(TOPK_MAX, seq // 4)
    assert seq % tq == 0 and tq % tk == 0 and tk >= topk
    blk_q = lambda b, i: (b, 0, 0, i)
    return pl.pallas_call(
        functools.partial(_dsa_kernel, topk=topk),
        grid=(bsz, seq // tq),
        in_specs=[
            pl.BlockSpec((1, IDX_HEADS, LANES, tq), blk_q),
            pl.BlockSpec((1, seq, LANES), lambda b, i: (b, 0, 0)),
            pl.BlockSpec((1, IDX_HEADS, tq), lambda b, i: (b, 0, i)),
            pl.BlockSpec((1, ATT_HEADS, LANES, tq), blk_q),
            pl.BlockSpec((1, ATT_HEADS // 2, seq, LANES), lambda b, i: (b, 0, 0, 0)),
            pl.BlockSpec((1, ATT_HEADS, seq // tk, V_ROWS, tk), lambda b, i: (b, 0, 0, 0, 0)),
        ],
        out_specs=pl.BlockSpec((1, tq, ATT_WIDTH), lambda b, i: (b, i, 0)),
        out_shape=jax.ShapeDtypeStruct((bsz, seq, ATT_WIDTH), BF16),
        scratch_shapes=[
            pltpu.VMEM((seq // tk, tk // SUB, SUB, tq), F32),
            pltpu.VMEM((seq // tk, tk // (2 * SUB), 2 * SUB, tq), BF16),
            pltpu.VMEM((tk // SUB, SUB, tq), F32),
            pltpu.VMEM((ATT_HEADS, SUB, tq), F32),
            pltpu.VMEM((ATT_HEADS, V_ROWS, tq), F32),
            pltpu.VMEM((ATT_HEADS, SUB, tq), F32),
        ],
        compiler_params=_params("parallel", "arbitrary"),
        name="dsa",
    )(qi, ki, wit, qa, ka, vt)


RET_CHUNK = 256


def _ret_kernel(q_ref, k_ref, v_ref, g_ref, o_ref, state_ref, decay_ref):
    c = RET_CHUNK
    n = pl.program_id(1)
    row = lax.broadcasted_iota(jnp.int32, (c, c), 0)
    col = lax.broadcasted_iota(jnp.int32, (c, c), 1)
    idx = lax.broadcasted_iota(jnp.int32, (c, 1), 0).astype(F32)

    @pl.when(n == 0)
    def _():
        state_ref[...] = jnp.zeros_like(state_ref)
        for h in range(RET_HEADS):
            lg = math.log(1.0 - 2.0 ** (-5.0 - h))
            diff = (row - col).astype(F32)
            decay_ref[h] = jnp.where(row >= col, jnp.exp(lg * jnp.maximum(diff, 0.0)), 0.0)

    for h in range(RET_HEADS):
        lg = math.log(1.0 - 2.0 ** (-5.0 - h))
        sl = slice(h * LANES, (h + 1) * LANES)
        q = q_ref[0, :, sl]
        k = k_ref[0, :, sl]
        v = v_ref[0, :, sl]
        inner = lax.dot_general(q, k, _NT, preferred_element_type=F32) * decay_ref[h]
        o = jnp.dot(inner.astype(BF16), v, preferred_element_type=F32)
        qd = (q.astype(F32) * jnp.exp(lg * (idx + 1.0))).astype(BF16)
        state = state_ref[h]
        o = o + jnp.dot(qd, state.astype(BF16), preferred_element_type=F32)
        kd = (k.astype(F32) * jnp.exp(lg * (c - 1.0 - idx))).astype(BF16)
        kv = lax.dot_general(kd, v, _TN, preferred_element_type=F32)
        state_ref[h] = state * math.exp(lg * c) + kv
        gate = _silu(g_ref[0, :, sl].astype(F32))
        o_ref[0, :, sl] = (gate * _rms(o)).astype(BF16)


def _retention(qr, kr, vr, gr):
    bsz, seq, _ = qr.shape
    c = RET_CHUNK
    spec = pl.BlockSpec((1, c, RET_WIDTH), lambda b, i: (b, i, 0))
    return pl.pallas_call(
        _ret_kernel,
        grid=(bsz, seq // c),
        in_specs=[spec] * 4,
        out_specs=spec,
        out_shape=jax.ShapeDtypeStruct((bsz, seq, RET_WIDTH), BF16),
        scratch_shapes=[
            pltpu.VMEM((RET_HEADS, RET_HEAD_DIM, RET_HEAD_DIM), F32),
            pltpu.VMEM((RET_HEADS, c, c), F32),
        ],
        compiler_params=_params("arbitrary", "arbitrary"),
        name="ret",
    )(qr, kr, vr, gr)


ROUTER_E0 = N_GROUPS


def _mid_kernel(att_ref, ret_ref, x_ref, woa_ref, wor_ref, gpost_ref, gate1_ref,
                gpre_ref, scale2_ref, shift2_ref, wr_ref, br_ref,
                h1_ref, h2_ref, comb_ref):
    mix = jnp.dot(att_ref[0], woa_ref[...], preferred_element_type=F32)
    mix = mix + jnp.dot(ret_ref[0], wor_ref[...], preferred_element_type=F32)
    h1 = x_ref[0] + gate1_ref[0] * (_rms(mix) * gpost_ref[...])
    h1_ref[0] = h1
    h2 = (_rms(h1) * gpre_ref[...] * (1.0 + scale2_ref[0]) + shift2_ref[0]).astype(BF16)
    h2_ref[0] = h2

    logits = jnp.dot(h2, wr_ref[...], preferred_element_type=F32) + br_ref[...]
    lane = lax.broadcasted_iota(jnp.int32, logits.shape, 1)
    big = jnp.int32(LANES)

    def first_argmax(vals, vmax):
        return jnp.min(jnp.where(vals == vmax, lane, big), axis=1, keepdims=True)

    gl = jnp.where(lane < N_GROUPS, logits, NEG_INF)
    gexp = jnp.exp(gl - jnp.max(gl, axis=1, keepdims=True))
    gprob = gexp / jnp.sum(gexp, axis=1, keepdims=True)
    g_top = jnp.max(gprob, axis=1, keepdims=True)
    g_sel = first_argmax(gprob, g_top)
    e_lo = ROUTER_E0 + g_sel * EXPERTS_PER_GROUP
    in_group = (lane >= e_lo) & (lane < e_lo + EXPERTS_PER_GROUP)
    el = jnp.where(in_group, logits, NEG_INF)
    eexp = jnp.exp(el - jnp.max(el, axis=1, keepdims=True))
    eprob = jnp.where(in_group, eexp / jnp.sum(eexp, axis=1, keepdims=True), -1.0)
    top1 = jnp.max(eprob, axis=1, keepdims=True)
    idx1 = first_argmax(eprob, top1)
    rest = jnp.where(lane == idx1, -1.0, eprob)
    top2 = jnp.max(rest, axis=1, keepdims=True)
    idx2 = first_argmax(rest, top2)
    norm = g_top / (top1 + top2)
    comb_ref[0] = jnp.where(lane == idx1, top1 * norm, 0.0) + jnp.where(lane == idx2, top2 * norm, 0.0)


def _mid(att, ret, x, woa, wor, gpost, gate1, gpre, scale2, shift2, wr, br, ts):
    bsz, seq, _ = x.shape
    row = lambda b, i: (b, i, 0)
    per_b = lambda b, i: (b, 0, 0)
    const2 = lambda b, i: (0, 0)
    vec = pl.BlockSpec((1, D_MODEL), const2)
    bvec = pl.BlockSpec((1, 1, D_MODEL), per_b)
    return pl.pallas_call(
        _mid_kernel,
        grid=(bsz, seq // ts),
        in_specs=[
            pl.BlockSpec((1, ts, ATT_WIDTH), row),
            pl.BlockSpec((1, ts, RET_WIDTH), row),
            pl.BlockSpec((1, ts, D_MODEL), row),
            pl.BlockSpec(woa.shape, const2),
            pl.BlockSpec(wor.shape, const2),
            vec, bvec, vec, bvec, bvec,
            pl.BlockSpec(wr.shape, const2),
            pl.BlockSpec((1, LANES), const2),
        ],
        out_specs=[pl.BlockSpec((1, ts, D_MODEL), row),
                   pl.BlockSpec((1, ts, D_MODEL), row),
                   pl.BlockSpec((1, ts, LANES), row)],
        out_shape=[jax.ShapeDtypeStruct((bsz, seq, D_MODEL), F32),
                   jax.ShapeDtypeStruct((bsz, seq, D_MODEL), BF16),
                   jax.ShapeDtypeStruct((bsz, seq, LANES), F32)],
        compiler_params=_params("parallel", "parallel"),
        name="mid",
    )(att, ret, x, woa, wor, gpost, gate1, gpre, scale2, shift2, wr, br)


MOE_EPS = 8


def _moe_kernel(h2_ref, comb_ref, h1_ref, gate2_ref, gpost_ref, wg_ref, wu_ref, wd_ref,
                o_ref, acc_ref):
    step = pl.program_id(2)

    @pl.when(step == 0)
    def _():
        acc_ref[...] = jnp.zeros_like(acc_ref)

    t = h2_ref[0]
    comb = comb_ref[0]
    lane = lax.broadcasted_iota(jnp.int32, comb.shape, 1)
    hids = []
    for j in range(MOE_EPS):
        gate = jnp.dot(t, wg_ref[j], preferred_element_type=F32)
        up = jnp.dot(t, wu_ref[j], preferred_element_type=F32)
        hid = _silu(gate) * up
        e_lane = ROUTER_E0 + step * MOE_EPS + j
        cw = jnp.sum(jnp.where(lane == e_lane, comb, 0.0), axis=1, keepdims=True)
        hids.append((hid * cw).astype(BF16))
    wd = wd_ref[...].reshape(MOE_EPS * D_EXPERT, D_MODEL)
    acc_ref[...] += jnp.dot(jnp.concatenate(hids, axis=1), wd, preferred_element_type=F32)

    @pl.when(step == N_EXPERTS // MOE_EPS - 1)
    def _():
        o_ref[0] = h1_ref[0] + gate2_ref[0] * (_rms(acc_ref[...]) * gpost_ref[...])


def _moe(h2, comb, h1, gate2, gpost, wg, wu, wd, tm):
    bsz, seq, _ = h2.shape
    row = lambda b, i, e: (b, i, 0)
    return pl.pallas_call(
        _moe_kernel,
        grid=(bsz, seq // tm, N_EXPERTS // MOE_EPS),
        in_specs=[
            pl.BlockSpec((1, tm, D_MODEL), row),
            pl.BlockSpec((1, tm, LANES), row),
            pl.BlockSpec((1, tm, D_MODEL), row),
            pl.BlockSpec((1, 1, D_MODEL), lambda b, i, e: (b, 0, 0)),
            pl.BlockSpec((1, D_MODEL), lambda b, i, e: (0, 0)),
            pl.BlockSpec((MOE_EPS, D_MODEL, D_EXPERT), lambda b, i, e: (e, 0, 0)),
            pl.BlockSpec((MOE_EPS, D_MODEL, D_EXPERT), lambda b, i, e: (e, 0, 0)),
            pl.BlockSpec((MOE_EPS, D_EXPERT, D_MODEL), lambda b, i, e: (e, 0, 0)),
        ],
        out_specs=pl.BlockSpec((1, tm, D_MODEL), row),
        out_shape=jax.ShapeDtypeStruct((bsz, seq, D_MODEL), F32),
        scratch_shapes=[pltpu.VMEM((tm, D_MODEL), F32)],
        compiler_params=_params("parallel", "parallel", "arbitrary"),
        name="moe",
    )(h2, comb, h1, gate2, gpost, wg, wu, wd)


def _layer(x, positions, mod, g_pre_mix, w_in, w_out, g_post_mix, g_pre_ffn,
           w_group, b_group, w_expert, b_expert, w_gate_exp, w_up_exp, w_down_exp, g_post_ffn):
    bsz, seq, _ = x.shape
    shift1, scale1, gate1, shift2, scale2, gate2 = [
        m.reshape(bsz, 1, D_MODEL) for m in jnp.split(mod, N_MOD, axis=-1)]
    vec = lambda g: g.reshape(1, D_MODEL)

    a, i8, r = ATT_WIDTH, IDX_HEADS, RET_WIDTH
    offs = np.cumsum([0, a, a, a, a, IDX_HEAD_DIM, i8, r, r, r, r])
    seg = lambda j: w_in[:, offs[j]:offs[j + 1]].astype(BF16)
    wki = jnp.concatenate([seg(4), seg(4), jnp.pad(seg(5), ((0, 0), (0, LANES - i8)))], axis=1)
    weights = [seg(0), seg(1), seg(2), seg(3), wki,
               seg(6), seg(7), seg(8), seg(9)]

    tables = _rope_tables(positions, 512)
    qa, ka, vt, qi, ki, wit, qr, kr, vr, gr = _inproj(
        x, shift1, scale1, vec(g_pre_mix), weights, tables, 512)
    att = _dsa(qi, ki, wit, qa, ka, vt)
    ret = _retention(qr, kr, vr, gr)

    w_router = jnp.pad(jnp.concatenate([w_group, w_expert], axis=1),
                       ((0, 0), (0, LANES - N_GROUPS - N_EXPERTS))).astype(BF16)
    b_router = jnp.pad(jnp.concatenate([b_group, b_expert]),
                       (0, LANES - N_GROUPS - N_EXPERTS)).reshape(1, LANES)
    h1, h2, comb = _mid(att, ret, x, w_out[:ATT_WIDTH].astype(BF16), w_out[ATT_WIDTH:].astype(BF16),
                        vec(g_post_mix), gate1, vec(g_pre_ffn), scale2, shift2, w_router, b_router, 1024)

    return _moe(h2, comb, h1, gate2, vec(g_post_ffn), w_gate_exp.astype(BF16),
                w_up_exp.astype(BF16), w_down_exp.astype(BF16), 1024)


def kernel(x, c, positions, g_pre_mix, w_ada, b_ada, w_in, w_out, g_post_mix, g_pre_ffn,
           w_group, b_group, w_expert, b_expert, w_gate_exp, w_up_exp, w_down_exp, g_post_ffn):
    h = x
    for l in range(w_in.shape[0]):
        mod = _ada(c, w_ada[l], b_ada[l])
        h = _layer(h, positions, mod, g_pre_mix[l], w_in[l], w_out[l], g_post_mix[l], g_pre_ffn[l],
                   w_group[l], b_group[l], w_expert[l], b_expert[l],
                   w_gate_exp[l], w_up_exp[l], w_down_exp[l], g_post_ffn[l])
    return h
```

```python
import functools
import math

import numpy as np
import jax
import jax.numpy as jnp
from jax import lax
from jax.experimental import pallas as pl
from jax.experimental.pallas import tpu as pltpu

D_MODEL = 1024
ATT_HEADS = 8
ATT_HEAD_DIM = 64
IDX_HEADS = 8
IDX_HEAD_DIM = 64
TOPK_MAX = 256
RET_HEADS = 4
RET_HEAD_DIM = 128
ROPE_THETA = 10000.0
ATT_WIDTH = ATT_HEADS * ATT_HEAD_DIM
RET_WIDTH = RET_HEADS * RET_HEAD_DIM
N_GROUPS = 4
EXPERTS_PER_GROUP = 8
N_EXPERTS = N_GROUPS * EXPERTS_PER_GROUP
D_EXPERT = 256
N_MOD = 6
EPS = 1e-6

LANES = 128
SUB = 8
VMEM_LIMIT = 56 * 1024 * 1024

DSA_TQ = 256
DSA_TK = 256
V_ROWS = 80
F32 = jnp.float32
BF16 = jnp.bfloat16
NEG_INF = float("-inf")
F32_LOWEST = float(np.finfo(np.float32).min)
INT_MIN = -2 ** 31
LOG2_E = math.log2(math.e)
BF16_KEY_BITS = 16
NO_CHUNK = 2 ** 30
DENOM_FLOOR = 1e-30

_NT = (((1,), (1,)), ((), ()))
_TN = (((0,), (0,)), ((), ()))


def _params(*sem):
    return pltpu.CompilerParams(dimension_semantics=sem, vmem_limit_bytes=VMEM_LIMIT)


def _rms(x):
    return x * lax.rsqrt(jnp.mean(x * x, axis=-1, keepdims=True) + EPS)


def _silu(x):
    return x * (1.0 / (1.0 + jnp.exp(-x)))


def _ada_kernel(c_ref, w_ref, b_ref, o_ref):
    a = _silu(c_ref[...]).astype(BF16)
    o_ref[...] = jnp.dot(a, w_ref[...].astype(BF16), preferred_element_type=F32) + b_ref[...]


def _ada(c, w, b):
    bsz = c.shape[0]
    n = w.shape[1]
    tn = D_MODEL
    return pl.pallas_call(
        _ada_kernel,
        grid=(n // tn,),
        in_specs=[
            pl.BlockSpec((bsz, D_MODEL), lambda j: (0, 0)),
            pl.BlockSpec((D_MODEL, tn), lambda j: (0, j)),
            pl.BlockSpec((1, tn), lambda j: (0, j)),
        ],
        out_specs=pl.BlockSpec((bsz, tn), lambda j: (0, j)),
        out_shape=jax.ShapeDtypeStruct((bsz, n), F32),
        compiler_params=_params("arbitrary"),
        name="ada",
    )(c, w, b.reshape(1, n))


def _tables_kernel(pos_ref, f_ref, cs64_ref, sn64_ref, cs128_ref, sn128_ref):
    ang = pos_ref[0].astype(F32) * f_ref[...]
    c = jnp.cos(ang)
    s = jnp.sin(ang)
    c_sw = pltpu.roll(c, 64, 1)
    s_sw = pltpu.roll(s, 64, 1)
    lane = lax.broadcasted_iota(jnp.int32, c.shape, 1)
    lo = lane < 64
    cs128_ref[0] = jnp.where(lo, c, c_sw)
    sn128_ref[0] = jnp.where(lo, -s, s_sw)
    cs64_ref[0] = jnp.where(lo, c_sw, c)
    s64 = jnp.where(lo, s_sw, s)
    sn64_ref[0] = jnp.where((lane % 64) < 32, -s64, s64)


def _rope_tables(positions, ts):
    bsz, seq = positions.shape
    f64 = ROPE_THETA ** (-jnp.arange(32, dtype=F32) / 32)
    f128 = ROPE_THETA ** (-jnp.arange(64, dtype=F32) / 64)
    frow = jnp.concatenate([f128, f64, f64]).reshape(1, LANES)
    tab = jax.ShapeDtypeStruct((bsz, seq, LANES), F32)
    tspec = pl.BlockSpec((1, ts, LANES), lambda b, i: (b, i, 0))
    return pl.pallas_call(
        _tables_kernel,
        grid=(bsz, seq // ts),
        in_specs=[
            pl.BlockSpec((1, ts, 1), lambda b, i: (b, i, 0)),
            pl.BlockSpec((1, LANES), lambda b, i: (0, 0)),
        ],
        out_specs=[tspec] * 4,
        out_shape=[tab] * 4,
        compiler_params=_params("parallel", "parallel"),
        name="tables",
    )(positions.reshape(bsz, seq, 1), frow)


def _rope64(y, cs, sn):
    lane = lax.broadcasted_iota(jnp.int32, y.shape, 1)
    rot = jnp.where((lane % 64) < 32, pltpu.roll(y, 96, 1), pltpu.roll(y, 32, 1))
    return y * cs + rot * sn


def _rope128(y, cs, sn):
    return y * cs + pltpu.roll(y, 64, 1) * sn


def _inproj_kernel(x_ref, shift_ref, scale_ref, g_ref,
                   wqa_ref, wka_ref, wv_ref, wqi_ref, wki_ref,
                   wqr_ref, wkr_ref, wvr_ref, wgr_ref,
                   cs64_ref, sn64_ref, cs128_ref, sn128_ref,
                   qa_ref, ka_ref, vt_ref, qi_ref, ki_ref, wit_ref,
                   qr_ref, kr_ref, vr_ref, gr_ref):
    x = x_ref[0]
    h = _rms(x) * g_ref[...] * (1.0 + scale_ref[0]) + shift_ref[0]
    hb = h.astype(BF16)
    cs64, sn64 = cs64_ref[0], sn64_ref[0]
    cs128, sn128 = cs128_ref[0], sn128_ref[0]

    lane = lax.broadcasted_iota(jnp.int32, (hb.shape[0], LANES), 1)
    wide = 2 * LANES

    def pairs(w_ref, rope, cs, sn):
        for c in range(w_ref.shape[1] // wide):
            y = jnp.dot(hb, w_ref[:, c * wide:(c + 1) * wide], preferred_element_type=F32)
            for j in range(wide // LANES):
                yield 2 * c + j, rope(y[:, j * LANES:(j + 1) * LANES], cs, sn)

    def query_heads(w_ref, o_ref, mult):
        for c, y in pairs(w_ref, _rope64, cs64, sn64):
            y = y * mult
            o_ref[0, 2 * c] = jnp.where(lane < 64, y, 0.0).T.astype(BF16)
            o_ref[0, 2 * c + 1] = jnp.where(lane >= 64, y, 0.0).T.astype(BF16)

    query_heads(wqa_ref, qa_ref, ATT_HEAD_DIM ** -0.5 * LOG2_E)
    query_heads(wqi_ref, qi_ref, 1.0)
    for c, y in pairs(wka_ref, _rope64, cs64, sn64):
        ka_ref[0, c] = y.astype(BF16)

    ts = hb.shape[0]
    vt = jnp.dot(hb, wv_ref[...], preferred_element_type=F32).T
    for h in range(ATT_HEADS):
        for j in range(ts // DSA_TK):
            blk = vt[h * 64:(h + 1) * 64, j * DSA_TK:(j + 1) * DSA_TK]
            vt_ref[0, h, j, :64, :] = blk.astype(BF16)
            vt_ref[0, h, j, 64:, :] = jnp.ones((V_ROWS - 64, DSA_TK), BF16)

    yk = jnp.dot(hb, wki_ref[...], preferred_element_type=F32)
    ki_ref[0] = _rope64(yk[:, :LANES], cs64, sn64).astype(BF16)
    wit_ref[0] = yk[:, LANES:].T[:IDX_HEADS] * ((IDX_HEADS ** -0.5) * (IDX_HEAD_DIM ** -0.5))

    for c, y in pairs(wqr_ref, _rope128, cs128, sn128):
        qr_ref[0, :, c * LANES:(c + 1) * LANES] = y.astype(BF16)
    for c, y in pairs(wkr_ref, _rope128, cs128, sn128):
        kr_ref[0, :, c * LANES:(c + 1) * LANES] = (y * (RET_HEAD_DIM ** -0.5)).astype(BF16)
    vr_ref[0] = jnp.dot(hb, wvr_ref[...], preferred_element_type=F32).astype(BF16)
    gr_ref[0] = jnp.dot(hb, wgr_ref[...], preferred_element_type=F32).astype(BF16)


def _inproj(x, shift, scale, g, weights, tables, ts):
    bsz, seq, _ = x.shape
    row = lambda b, i: (b, i, 0)
    per_b = lambda b, i: (b, 0, 0)
    const2 = lambda b, i: (0, 0)
    head_major = lambda b, i: (b, 0, i, 0)

    def wspec(w):
        return pl.BlockSpec(w.shape, const2)

    qt_shape = jax.ShapeDtypeStruct((bsz, ATT_HEADS, LANES, seq), BF16)
    qt_spec = pl.BlockSpec((1, ATT_HEADS, LANES, ts), lambda b, i: (b, 0, 0, i))
    pair_shape = jax.ShapeDtypeStruct((bsz, ATT_HEADS // 2, seq, LANES), BF16)
    pair_spec = pl.BlockSpec((1, ATT_HEADS // 2, ts, LANES), head_major)
    wide = lambda n, dt=BF16: jax.ShapeDtypeStruct((bsz, seq, n), dt)
    wspec_out = lambda n: pl.BlockSpec((1, ts, n), row)
    return pl.pallas_call(
        _inproj_kernel,
        grid=(bsz, seq // ts),
        in_specs=[
            pl.BlockSpec((1, ts, D_MODEL), row),
            pl.BlockSpec((1, 1, D_MODEL), per_b),
            pl.BlockSpec((1, 1, D_MODEL), per_b),
            pl.BlockSpec((1, D_MODEL), const2),
            *[wspec(w) for w in weights],
            *[pl.BlockSpec((1, ts, LANES), row)] * 4,
        ],
        out_specs=[qt_spec, pair_spec,
                   pl.BlockSpec((1, ATT_HEADS, ts // DSA_TK, V_ROWS, DSA_TK), lambda b, i: (b, 0, i, 0, 0)),
                   qt_spec, wspec_out(LANES),
                   pl.BlockSpec((1, IDX_HEADS, ts), lambda b, i: (b, 0, i)),
                   wspec_out(RET_WIDTH), wspec_out(RET_WIDTH), wspec_out(RET_WIDTH), wspec_out(RET_WIDTH)],
        out_shape=[qt_shape, pair_shape,
                   jax.ShapeDtypeStruct((bsz, ATT_HEADS, seq // DSA_TK, V_ROWS, DSA_TK), BF16),
                   qt_shape, wide(LANES),
                   jax.ShapeDtypeStruct((bsz, IDX_HEADS, seq), F32),
                   wide(RET_WIDTH), wide(RET_WIDTH), wide(RET_WIDTH), wide(RET_WIDTH)],
        compiler_params=_params("parallel", "parallel"),
        name="inproj",
    )(x, shift, scale, g, *weights, *tables)


def _key_to_float(u):
    key = u ^ INT_MIN
    bits = jnp.where(key >= 0, key, key ^ 0x7FFFFFFF)
    return lax.bitcast_convert_type(bits, F32)


def _dsa_kernel(qi_ref, ki_ref, wit_ref, qa_ref, ka_ref, vt_ref, o_ref,
                sc_ref, sb_ref, crit_ref, m_ref, acc_ref, kmax_ref, *, topk):
    tq, tk = DSA_TQ, DSA_TK
    qb = pl.program_id(1)
    nk = (qb + 1) * (tq // tk)
    grp = (tk // SUB, SUB, tq)
    grp16 = (tk // (2 * SUB), 2 * SUB, tq)
    q_pos = qb * tq + lax.broadcasted_iota(jnp.int32, (SUB, tq), 1)

    def key_index(kc):
        return (kc * tk + lax.broadcasted_iota(jnp.int32, grp, 0) * SUB
                + lax.broadcasted_iota(jnp.int32, grp, 1))

    wt = wit_ref[0]

    def for_chunks(body):
        def quad(i, carry):
            for j in range(4):
                body(4 * i + j)
            return carry
        lax.fori_loop(0, nk // 4, quad, 0)

        @pl.when(nk % 4 >= 2)
        def _():
            body(nk // 4 * 4)
            body(nk // 4 * 4 + 1)

        @pl.when(nk % 2 == 1)
        def _():
            body(nk - 1)

    def sum_chunks(body, init):
        acc = lax.fori_loop(0, nk // 2, lambda i, a: body(2 * i + 1, body(2 * i, a)), init)
        return lax.cond(nk % 2 == 1, lambda a: body(nk - 1, a), lambda a: a, acc)

    def score_chunk(kc):
        k0 = pl.multiple_of(kc * tk, tk)
        kch = ki_ref[0, pl.ds(k0, tk), :]
        acc = jnp.zeros((tk, tq), F32)
        for h in range(IDX_HEADS):
            logit = jnp.dot(kch, qi_ref[0, h], preferred_element_type=F32)
            acc = acc + jnp.maximum(logit, 0.0) * wt[h:h + 1, :]
        score = jnp.where(key_index(kc) <= q_pos[None], acc.reshape(grp), NEG_INF)
        sc_ref[kc] = score
        sb_ref[kc] = score.reshape(tk, tq).astype(BF16).reshape(grp16)

    for_chunks(score_chunk)

    def count(pred):
        def body(kc, acc):
            hit = jnp.where(pred(sc_ref[kc], kc), 1.0, 0.0)
            parts = [hit[g] for g in range(grp[0])]
            while len(parts) > 1:
                parts = [parts[i] + parts[i + 1] for i in range(0, len(parts), 2)]
            return acc + parts[0]
        acc = sum_chunks(body, jnp.zeros((SUB, tq), F32))
        return jnp.sum(acc, axis=0, keepdims=True)

    def rows(v):
        return jnp.broadcast_to(v, (SUB, tq))[None]

    few_keys = q_pos[:1] < topk - 1

    def count_bf16(cand):
        cb = jnp.broadcast_to(cand, (2 * SUB, tq)).astype(BF16)[None]

        def body(kc, acc):
            hit = jnp.where(sb_ref[kc] >= cb, jnp.ones((), BF16), jnp.zeros((), BF16))
            parts = [hit[g] for g in range(grp16[0])]
            while len(parts) > 1:
                parts = [parts[i] + parts[i + 1] for i in range(0, len(parts), 2)]
            return acc + parts[0].astype(F32)
        acc = sum_chunks(body, jnp.zeros((2 * SUB, tq), F32))
        return jnp.sum(acc, axis=0, keepdims=True)

    def high_body(i, u):
        trial = u | lax.shift_left(jnp.int32(1), 31 - i)
        return jnp.where(count_bf16(_key_to_float(trial)) >= topk, trial, u)

    u_hi = lax.fori_loop(0, BF16_KEY_BITS, high_body, jnp.zeros((1, tq), jnp.int32))
    base = u_hi - 2 ** (32 - BF16_KEY_BITS)

    def low_body(i, state):
        d, cnt_u = state
        trial = d | lax.shift_left(jnp.int32(1), 32 - BF16_KEY_BITS - i)
        cand = rows(_key_to_float(base + trial))
        cnt = count(lambda x, kc: x >= cand)
        accept = cnt >= topk
        return jnp.where(accept, trial, d), jnp.where(accept, cnt, cnt_u)

    d, cnt_u = lax.fori_loop(
        0, 32 - BF16_KEY_BITS + 1, low_body,
        (jnp.zeros((1, tq), jnp.int32), jnp.full((1, tq), 2.0 * topk, F32)))
    u = base + d
    thr = rows(jnp.where(few_keys, F32_LOWEST, _key_to_float(u)))
    has_excess_ties = jnp.max(jnp.where(few_keys, 0.0, cnt_u)) > topk

    def tie_plan():
        need = topk - count(lambda x, kc: x > thr)

        def chunk_body(kc, carry):
            seen, crit_chunk, seen_before = carry
            tie = jnp.where(sc_ref[kc] == thr, 1.0, 0.0)
            parts = [tie[g] for g in range(grp[0])]
            while len(parts) > 1:
                parts = [parts[i] + parts[i + 1] for i in range(0, len(parts), 2)]
            seen_now = seen + jnp.sum(parts[0], axis=0, keepdims=True)
            reached = jnp.where((crit_chunk == NO_CHUNK) & (seen_now >= need), 1.0, 0.0)
            crit_ref[...] = crit_ref[...] + rows(reached) * (tie - crit_ref[...])
            return (seen_now, jnp.where(reached > 0.5, kc, crit_chunk),
                    jnp.where(reached > 0.5, seen, seen_before))

        crit_ref[...] = jnp.zeros(crit_ref.shape, F32)
        zero = jnp.zeros((1, tq), F32)
        _, crit_chunk, seen_before = lax.fori_loop(
            0, nk, chunk_body, (zero, jnp.full((1, tq), NO_CHUNK, jnp.int32), zero))
        tie = crit_ref[...].reshape(tk, tq)
        lower = (lax.broadcasted_iota(jnp.int32, (tk, tk), 0)
                 >= lax.broadcasted_iota(jnp.int32, (tk, tk), 1))
        rank = jnp.dot(jnp.where(lower, 1.0, 0.0).astype(BF16), tie.astype(BF16),
                       preferred_element_type=F32)
        crit_ref[...] = jnp.where(rank <= need - seen_before, tie, 0.0).reshape(grp)
        return crit_chunk

    def no_excess_ties():
        crit_ref[...] = jnp.zeros(crit_ref.shape, F32)
        return jnp.full((1, tq), NO_CHUNK, jnp.int32)

    crit_chunk = lax.cond(has_excess_ties, tie_plan, no_excess_ties)

    def mask_bias(kc):
        x = sc_ref[kc]
        tie_kept = (rows(jnp.where(kc < crit_chunk, 1.0, 0.0))
                    + rows(jnp.where(kc == crit_chunk, 1.0, 0.0)) * crit_ref[...])
        keep = (x > thr) | ((x == thr) & (tie_kept > 0.5))
        return jnp.where(keep, 0.0, NEG_INF)

    def logits(kc, h, bias):
        k0 = pl.multiple_of(kc * tk, tk)
        kch = ka_ref[0, h // 2, pl.ds(k0, tk), :]
        s = jnp.dot(kch, qa_ref[0, h], preferred_element_type=F32)
        return s.reshape(grp) + bias

    def max_chunk(kc, m):
        bias = mask_bias(kc)
        return jnp.stack([jnp.maximum(m[h], jnp.max(logits(kc, h, bias), axis=0))
                          for h in range(ATT_HEADS)])

    def pv_chunk(kc):
        bias = mask_bias(kc)
        ss = [logits(kc, h, bias) for h in range(ATT_HEADS)]
        for h in range(ATT_HEADS):
            p = jnp.exp2(ss[h] - m_ref[h][None]).reshape(tk, tq).astype(BF16)
            acc_ref[h] += jnp.dot(vt_ref[0, h, kc], p, preferred_element_type=F32)

    def exp_pv_pass():
        acc_ref[...] = jnp.zeros(acc_ref.shape, F32)
        for_chunks(pv_chunk)

    ones_r = jnp.ones((SUB, LANES), BF16)
    lane_r = lax.broadcasted_iota(jnp.int32, (SUB, LANES), 1)

    @pl.when(qb == 0)
    def _():
        for h in range(ATT_HEADS):
            k = ka_ref[0, h // 2]
            head_lanes = jnp.where((lane_r >= 64) == (h % 2 == 1), 1.0, 0.0).astype(BF16)
            k_norm2 = lax.dot_general(head_lanes, k * k, _NT, preferred_element_type=F32)
            kmax_ref[h] = jnp.broadcast_to(jnp.max(k_norm2, axis=1, keepdims=True), (SUB, tq))

    for h in range(ATT_HEADS):
        q = qa_ref[0, h]
        q_norm2 = jnp.dot(ones_r, q * q, preferred_element_type=F32)
        m_ref[h] = jnp.sqrt(q_norm2 * kmax_ref[h])
    exp_pv_pass()
    denom_min = jnp.min(jnp.stack([acc_ref[h][64:65] for h in range(ATT_HEADS)]))

    @pl.when(jnp.logical_not(denom_min > DENOM_FLOOR))
    def _():
        m8 = lax.fori_loop(0, nk, max_chunk, jnp.full((ATT_HEADS, SUB, tq), NEG_INF, F32))
        m_ref[...] = jnp.broadcast_to(jnp.max(m8, axis=1, keepdims=True), m8.shape)
        exp_pv_pass()

    outs = []
    for h in range(ATT_HEADS):
        acc = acc_ref[h]
        outs.append(acc[:64] / acc[64:65])
    o_ref[0] = jnp.concatenate(outs, axis=0).T.astype(BF16)


def _dsa(qi, ki, wit, qa, ka, vt):
    bsz, seq, _ = ki.shape
    tq, tk = DSA_TQ, DSA_TK
    topk = min(TOPK_MAX, seq // 4)
    assert seq % tq == 0 and tq % tk == 0 and tk >= topk
    blk_q = lambda b, i: (b, 0, 0, i)
    return pl.pallas_call(
        functools.partial(_dsa_kernel, topk=topk),
        grid=(bsz, seq // tq),
        in_specs=[
            pl.BlockSpec((1, IDX_HEADS, LANES, tq), blk_q),
            pl.BlockSpec((1, seq, LANES), lambda b, i: (b, 0, 0)),
            pl.BlockSpec((1, IDX_HEADS, tq), lambda b, i: (b, 0, i)),
            pl.BlockSpec((1, ATT_HEADS, LANES, tq), blk_q),
            pl.BlockSpec((1, ATT_HEADS // 2, seq, LANES), lambda b, i: (b, 0, 0, 0)),
            pl.BlockSpec((1, ATT_HEADS, seq // tk, V_ROWS, tk), lambda b, i: (b, 0, 0, 0, 0)),
        ],
        out_specs=pl.BlockSpec((1, tq, ATT_WIDTH), lambda b, i: (b, i, 0)),
        out_shape=jax.ShapeDtypeStruct((bsz, seq, ATT_WIDTH), BF16),
        scratch_shapes=[
            pltpu.VMEM((seq // tk, tk // SUB, SUB, tq), F32),
            pltpu.VMEM((seq // tk, tk // (2 * SUB), 2 * SUB, tq), BF16),
            pltpu.VMEM((tk // SUB, SUB, tq), F32),
            pltpu.VMEM((ATT_HEADS, SUB, tq), F32),
            pltpu.VMEM((ATT_HEADS, V_ROWS, tq), F32),
            pltpu.VMEM((ATT_HEADS, SUB, tq), F32),
        ],
        compiler_params=_params("parallel", "arbitrary"),
        name="dsa",
    )(qi, ki, wit, qa, ka, vt)


RET_CHUNK = 256


def _ret_kernel(q_ref, k_ref, v_ref, g_ref, o_ref, state_ref, decay_ref):
    c = RET_CHUNK
    n = pl.program_id(1)
    row = lax.broadcasted_iota(jnp.int32, (c, c), 0)
    col = lax.broadcasted_iota(jnp.int32, (c, c), 1)
    idx = lax.broadcasted_iota(jnp.int32, (c, 1), 0).astype(F32)

    @pl.when(n == 0)
    def _():
        state_ref[...] = jnp.zeros_like(state_ref)
        for h in range(RET_HEADS):
            lg = math.log(1.0 - 2.0 ** (-5.0 - h))
            diff = (row - col).astype(F32)
            decay_ref[h] = jnp.where(row >= col, jnp.exp(lg * jnp.maximum(diff, 0.0)), 0.0)

    for h in range(RET_HEADS):
        lg = math.log(1.0 - 2.0 ** (-5.0 - h))
        sl = slice(h * LANES, (h + 1) * LANES)
        q = q_ref[0, :, sl]
        k = k_ref[0, :, sl]
        v = v_ref[0, :, sl]
        inner = lax.dot_general(q, k, _NT, preferred_element_type=F32) * decay_ref[h]
        o = jnp.dot(inner.astype(BF16), v, preferred_element_type=F32)
        qd = (q.astype(F32) * jnp.exp(lg * (idx + 1.0))).astype(BF16)
        state = state_ref[h]
        o = o + jnp.dot(qd, state.astype(BF16), preferred_element_type=F32)
        kd = (k.astype(F32) * jnp.exp(lg * (c - 1.0 - idx))).astype(BF16)
        kv = lax.dot_general(kd, v, _TN, preferred_element_type=F32)
        state_ref[h] = state * math.exp(lg * c) + kv
        gate = _silu(g_ref[0, :, sl].astype(F32))
        o_ref[0, :, sl] = (gate * _rms(o)).astype(BF16)


def _retention(qr, kr, vr, gr):
    bsz, seq, _ = qr.shape
    c = RET_CHUNK
    spec = pl.BlockSpec((1, c, RET_WIDTH), lambda b, i: (b, i, 0))
    return pl.pallas_call(
        _ret_kernel,
        grid=(bsz, seq // c),
        in_specs=[spec] * 4,
        out_specs=spec,
        out_shape=jax.ShapeDtypeStruct((bsz, seq, RET_WIDTH), BF16),
        scratch_shapes=[
            pltpu.VMEM((RET_HEADS, RET_HEAD_DIM, RET_HEAD_DIM), F32),
            pltpu.VMEM((RET_HEADS, c, c), F32),
        ],
        compiler_params=_params("arbitrary", "arbitrary"),
        name="ret",
    )(qr, kr, vr, gr)


ROUTER_E0 = N_GROUPS


def _mid_kernel(att_ref, ret_ref, x_ref, woa_ref, wor_ref, gpost_ref, gate1_ref,
                gpre_ref, scale2_ref, shift2_ref, wr_ref, br_ref,
                h1_ref, h2_ref, comb_ref):
    mix = jnp.dot(att_ref[0], woa_ref[...], preferred_element_type=F32)
    mix = mix + jnp.dot(ret_ref[0], wor_ref[...], preferred_element_type=F32)
    h1 = x_ref[0] + gate1_ref[0] * (_rms(mix) * gpost_ref[...])
    h1_ref[0] = h1
    h2 = (_rms(h1) * gpre_ref[...] * (1.0 + scale2_ref[0]) + shift2_ref[0]).astype(BF16)
    h2_ref[0] = h2

    logits = jnp.dot(h2, wr_ref[...], preferred_element_type=F32) + br_ref[...]
    lane = lax.broadcasted_iota(jnp.int32, logits.shape, 1)
    big = jnp.int32(LANES)

    def first_argmax(vals, vmax):
        return jnp.min(jnp.where(vals == vmax, lane, big), axis=1, keepdims=True)

    gl = jnp.where(lane < N_GROUPS, logits, NEG_INF)
    gexp = jnp.exp(gl - jnp.max(gl, axis=1, keepdims=True))
    gprob = gexp / jnp.sum(gexp, axis=1, keepdims=True)
    g_top = jnp.max(gprob, axis=1, keepdims=True)
    g_sel = first_argmax(gprob, g_top)
    e_lo = ROUTER_E0 + g_sel * EXPERTS_PER_GROUP
    in_group = (lane >= e_lo) & (lane < e_lo + EXPERTS_PER_GROUP)
    el = jnp.where(in_group, logits, NEG_INF)
    eexp = jnp.exp(el - jnp.max(el, axis=1, keepdims=True))
    eprob = jnp.where(in_group, eexp / jnp.sum(eexp, axis=1, keepdims=True), -1.0)
    top1 = jnp.max(eprob, axis=1, keepdims=True)
    idx1 = first_argmax(eprob, top1)
    rest = jnp.where(lane == idx1, -1.0, eprob)
    top2 = jnp.max(rest, axis=1, keepdims=True)
    idx2 = first_argmax(rest, top2)
    norm = g_top / (top1 + top2)
    comb_ref[0] = jnp.where(lane == idx1, top1 * norm, 0.0) + jnp.where(lane == idx2, top2 * norm, 0.0)


def _mid(att, ret, x, woa, wor, gpost, gate1, gpre, scale2, shift2, wr, br, ts):
    bsz, seq, _ = x.shape
    row = lambda b, i: (b, i, 0)
    per_b = lambda b, i: (b, 0, 0)
    const2 = lambda b, i: (0, 0)
    vec = pl.BlockSpec((1, D_MODEL), const2)
    bvec = pl.BlockSpec((1, 1, D_MODEL), per_b)
    return pl.pallas_call(
        _mid_kernel,
        grid=(bsz, seq // ts),
        in_specs=[
            pl.BlockSpec((1, ts, ATT_WIDTH), row),
            pl.BlockSpec((1, ts, RET_WIDTH), row),
            pl.BlockSpec((1, ts, D_MODEL), row),
            pl.BlockSpec(woa.shape, const2),
            pl.BlockSpec(wor.shape, const2),
            vec, bvec, vec, bvec, bvec,
            pl.BlockSpec(wr.shape, const2),
            pl.BlockSpec((1, LANES), const2),
        ],
        out_specs=[pl.BlockSpec((1, ts, D_MODEL), row),
                   pl.BlockSpec((1, ts, D_MODEL), row),
                   pl.BlockSpec((1, ts, LANES), row)],
        out_shape=[jax.ShapeDtypeStruct((bsz, seq, D_MODEL), F32),
                   jax.ShapeDtypeStruct((bsz, seq, D_MODEL), BF16),
                   jax.ShapeDtypeStruct((bsz, seq, LANES), F32)],
        compiler_params=_params("parallel", "parallel"),
        name="mid",
    )(att, ret, x, woa, wor, gpost, gate1, gpre, scale2, shift2, wr, br)


MOE_EPS = 4


def _moe_kernel(h2_ref, comb_ref, h1_ref, gate2_ref, gpost_ref, wg_ref, wu_ref, wd_ref,
                o_ref, acc_ref):
    step = pl.program_id(2)

    @pl.when(step == 0)
    def _():
        acc_ref[...] = jnp.zeros_like(acc_ref)

    t = h2_ref[0]
    comb = comb_ref[0]
    lane = lax.broadcasted_iota(jnp.int32, comb.shape, 1)
    hids = []
    for j in range(MOE_EPS):
        gate = jnp.dot(t, wg_ref[j].astype(BF16), preferred_element_type=F32)
        up = jnp.dot(t, wu_ref[j].astype(BF16), preferred_element_type=F32)
        hid = _silu(gate) * up
        e_lane = ROUTER_E0 + step * MOE_EPS + j
        cw = jnp.sum(jnp.where(lane == e_lane, comb, 0.0), axis=1, keepdims=True)
        hids.append((hid * cw).astype(BF16))
    wd = wd_ref[...].astype(BF16).reshape(MOE_EPS * D_EXPERT, D_MODEL)
    acc_ref[...] += jnp.dot(jnp.concatenate(hids, axis=1), wd, preferred_element_type=F32)

    @pl.when(step == N_EXPERTS // MOE_EPS - 1)
    def _():
        o_ref[0] = h1_ref[0] + gate2_ref[0] * (_rms(acc_ref[...]) * gpost_ref[...])


def _moe(h2, comb, h1, gate2, gpost, wg, wu, wd, tm):
    bsz, seq, _ = h2.shape
    row = lambda b, i, e: (b, i, 0)
    return pl.pallas_call(
        _moe_kernel,
        grid=(bsz, seq // tm, N_EXPERTS // MOE_EPS),
        in_specs=[
            pl.BlockSpec((1, tm, D_MODEL), row),
            pl.BlockSpec((1, tm, LANES), row),
            pl.BlockSpec((1, tm, D_MODEL), row),
            pl.BlockSpec((1, 1, D_MODEL), lambda b, i, e: (b, 0, 0)),
            pl.BlockSpec((1, D_MODEL), lambda b, i, e: (0, 0)),
            pl.BlockSpec((MOE_EPS, D_MODEL, D_EXPERT), lambda b, i, e: (e, 0, 0)),
            pl.BlockSpec((MOE_EPS, D_MODEL, D_EXPERT), lambda b, i, e: (e, 0, 0)),
            pl.BlockSpec((MOE_EPS, D_EXPERT, D_MODEL), lambda b, i, e: (e, 0, 0)),
        ],
        out_specs=pl.BlockSpec((1, tm, D_MODEL), row),
        out_shape=jax.ShapeDtypeStruct((bsz, seq, D_MODEL), F32),
        scratch_shapes=[pltpu.VMEM((tm, D_MODEL), F32)],
        compiler_params=_params("parallel", "parallel", "arbitrary"),
        name="moe",
    )(h2, comb, h1, gate2, gpost, wg, wu, wd)


def _layer(x, positions, mod, g_pre_mix, w_in, w_out, g_post_mix, g_pre_ffn,
           w_group, b_group, w_expert, b_expert, w_gate_exp, w_up_exp, w_down_exp, g_post_ffn):
    bsz, seq, _ = x.shape
    shift1, scale1, gate1, shift2, scale2, gate2 = [
        m.reshape(bsz, 1, D_MODEL) for m in jnp.split(mod, N_MOD, axis=-1)]
    vec = lambda g: g.reshape(1, D_MODEL)

    a, i8, r = ATT_WIDTH, IDX_HEADS, RET_WIDTH
    offs = np.cumsum([0, a, a, a, a, IDX_HEAD_DIM, i8, r, r, r, r])
    seg = lambda j: w_in[:, offs[j]:offs[j + 1]].astype(BF16)
    wki = jnp.concatenate([seg(4), seg(4), jnp.pad(seg(5), ((0, 0), (0, LANES - i8)))], axis=1)
    weights = [seg(0), seg(1), seg(2), seg(3), wki,
               seg(6), seg(7), seg(8), seg(9)]

    tables = _rope_tables(positions, 512)
    qa, ka, vt, qi, ki, wit, qr, kr, vr, gr = _inproj(
        x, shift1, scale1, vec(g_pre_mix), weights, tables, 512)
    att = _dsa(qi, ki, wit, qa, ka, vt)
    ret = _retention(qr, kr, vr, gr)

    w_router = jnp.pad(jnp.concatenate([w_group, w_expert], axis=1),
                       ((0, 0), (0, LANES - N_GROUPS - N_EXPERTS))).astype(BF16)
    b_router = jnp.pad(jnp.concatenate([b_group, b_expert]),
                       (0, LANES - N_GROUPS - N_EXPERTS)).reshape(1, LANES)
    h1, h2, comb = _mid(att, ret, x, w_out[:ATT_WIDTH].astype(BF16), w_out[ATT_WIDTH:].astype(BF16),
                        vec(g_post_mix), gate1, vec(g_pre_ffn), scale2, shift2, w_router, b_router, 1024)

    return _moe(h2, comb, h1, gate2, vec(g_post_ffn), w_gate_exp, w_up_exp, w_down_exp, 1024)


def kernel(x, c, positions, g_pre_mix, w_ada, b_ada, w_in, w_out, g_post_mix, g_pre_ffn,
           w_group, b_group, w_expert, b_expert, w_gate_exp, w_up_exp, w_down_exp, g_post_ffn):
    h = x
    for l in range(w_in.shape[0]):
        mod = _ada(c, w_ada[l], b_ada[l])
        h = _layer(h, positions, mod, g_pre_mix[l], w_in[l], w_out[l], g_post_mix[l], g_pre_ffn[l],
                   w_group[l], b_group[l], w_expert[l], b_expert[l],
                   w_gate_exp[l], w_up_exp[l], w_down_exp[l], g_post_ffn[l])
    return h
```
